```python
import math
import jax
import jax.numpy as jnp
from jax import lax
import numpy as np

D_MODEL = 1024
BATCH = 8
SEQ = 4096
DEPTH = 2

GRID_W = 64
CTX_LEN = 256
N_BRANCH = 4
BRANCH_W = 512
Q_BLOCK = 128
ROPE_THETA = 10000.0
NEG_INF = -1e30
LN_EPS = 1e-5
RMS_EPS = 1e-6

MLA_HEADS = 8
MLA_Q_RANK = 256
MLA_KV_RANK = 128
MLA_NOPE = 64
MLA_ROPE = 32
MLA_V = 64
MLA_SCALE = (MLA_NOPE + MLA_ROPE) ** -0.5

RWKV_HEADS = 8
RWKV_HEAD = 64
RWKV_W = RWKV_HEADS * RWKV_HEAD
RWKV_W_LORA = 64
RWKV_A_LORA = 64
RWKV_G_LORA = 128
RWKV_GN_EPS = 64e-5

GQA_HEADS = 8
GQA_KV_HEADS = 2
GQA_HEAD = 64
GQA_SCALE = GQA_HEAD ** -0.5

WIN_HEADS = 8
WIN_KV_HEADS = 2
WIN_HEAD = 64
WINDOW = 128
WIN_SCALE = WIN_HEAD ** -0.5

PEER_HEADS = 8
PEER_N_KEYS = 128
PEER_EXPERTS = PEER_N_KEYS * PEER_N_KEYS
PEER_TOPK = 16
PEER_DQ = 128
PEER_BLOCK = 128

DEEPNORM_ALPHA = (2 * DEPTH) ** 0.25
DEEPNORM_BETA = (8 * DEPTH) ** -0.25

MLA_IN = MLA_Q_RANK + MLA_KV_RANK + MLA_ROPE
RWKV_IN = 3 * RWKV_W + 2 * RWKV_W_LORA + 2 * RWKV_A_LORA + RWKV_G_LORA
GQA_IN = (GQA_HEADS + 2 * GQA_KV_HEADS) * GQA_HEAD
WIN_IN = (WIN_HEADS + 2 * WIN_KV_HEADS) * WIN_HEAD
GATE_IN = N_BRANCH * D_MODEL
GROUP_WIDTHS = (MLA_IN, RWKV_IN, GQA_IN, WIN_IN, GATE_IN)
IN_WIDTH = MLA_IN + RWKV_IN + GQA_IN + WIN_IN + GATE_IN

kernel_name = "hybrid_mla_rwkv7_gqa_swa_peer_dit"


def split_last(x, widths):
    out, start = [], 0
    for w in widths:
        out.append(x[..., start:start + w])
        start += w
    return out


def rms_norm(x, g):
    xf = x.astype(jnp.float32)
    y = xf * lax.rsqrt(jnp.mean(xf * xf, axis=-1, keepdims=True) + RMS_EPS) * g.astype(jnp.float32)
    return y.astype(x.dtype)


def layer_norm(x, g, b):
    xf = x.astype(jnp.float32)
    mu = jnp.mean(xf, axis=-1, keepdims=True)
    var = jnp.mean(jnp.square(xf - mu), axis=-1, keepdims=True)
    y = (xf - mu) * lax.rsqrt(var + LN_EPS) * g.astype(jnp.float32) + b.astype(jnp.float32)
    return y.astype(x.dtype)


def axial_rope_tables(rows, rot_dim):
    r_idx = jnp.repeat(jnp.arange(rows), GRID_W).astype(jnp.float32)
    c_idx = jnp.tile(jnp.arange(GRID_W), rows).astype(jnp.float32)
    n = rot_dim // 4
    inv = ROPE_THETA ** (-jnp.arange(n, dtype=jnp.float32) / n)
    ang = jnp.concatenate([r_idx[:, None] * inv, c_idx[:, None] * inv], axis=-1)
    return jnp.cos(ang), jnp.sin(ang)


def apply_rope(x, cos, sin):
    xf = x.astype(jnp.float32).reshape(x.shape[:-1] + (x.shape[-1] // 2, 2))
    x1, x2 = xf[..., 0], xf[..., 1]
    cs, sn = cos[None, :, None, :], sin[None, :, None, :]
    out = jnp.stack([x1 * cs - x2 * sn, x1 * sn + x2 * cs], axis=-1).reshape(x.shape)
    return out.astype(x.dtype)


def dense_attention(q, k, v, scale, sink=None):
    B, S, H, dk = q.shape
    Hk = k.shape[2]
    G = H // Hk
    dv = v.shape[-1]
    nb = S // Q_BLOCK
    qb = q.reshape(B, nb, Q_BLOCK, Hk, G, dk).swapaxes(0, 1)

    def one(qi):
        s = jnp.einsum('bqkgd,blkd->bkgql', qi, k).astype(jnp.float32) * scale
        if sink is not None:
            sk = jnp.broadcast_to(sink.astype(jnp.float32).reshape(Hk, G, 1, 1), s.shape[:-1] + (1,))
            p = jax.nn.softmax(jnp.concatenate([s, sk], axis=-1), axis=-1)[..., :-1]
        else:
            p = jax.nn.softmax(s, axis=-1)
        return jnp.einsum('bkgql,blkd->bqkgd', p.astype(v.dtype), v)

    o = lax.map(one, qb)
    return o.swapaxes(0, 1).reshape(B, S, H, dv)


def window_attention(q, k, v, k_ctx, v_ctx, sink, scale):
    B, S, H, d = q.shape
    Hk = k.shape[2]
    G = H // Hk
    C = k_ctx.shape[1]
    nb = S // WINDOW
    pad = ((0, 0), (WINDOW, WINDOW), (0, 0), (0, 0))
    kp, vp = jnp.pad(k, pad), jnp.pad(v, pad)
    k_win = jnp.concatenate([kp[:, i * WINDOW:i * WINDOW + S].reshape(B, nb, WINDOW, Hk, d) for i in range(3)], axis=2)
    v_win = jnp.concatenate([vp[:, i * WINDOW:i * WINDOW + S].reshape(B, nb, WINDOW, Hk, d) for i in range(3)], axis=2)
    q_pos = jnp.arange(S).reshape(nb, WINDOW)
    k_pos = q_pos[:, :1] - WINDOW + jnp.arange(3 * WINDOW)[None, :]
    valid = ((jnp.abs(q_pos[:, :, None] - k_pos[:, None, :]) <= WINDOW)
             & (k_pos[:, None, :] >= 0) & (k_pos[:, None, :] < S))
    qb = q.reshape(B, nb, WINDOW, Hk, G, d).swapaxes(0, 1)
    sink_b = sink.astype(jnp.float32).reshape(Hk, G, 1, 1)
    n_loc = 3 * WINDOW

    def one(args):
        qi, ki, vi, mi = args
        s_loc = jnp.einsum('bqkgd,blkd->bkgql', qi, ki).astype(jnp.float32) * scale
        s_loc = jnp.where(mi[None, None, None], s_loc, NEG_INF)
        s_ctx = jnp.einsum('bqkgd,blkd->bkgql', qi, k_ctx).astype(jnp.float32) * scale
        s_snk = jnp.broadcast_to(sink_b, s_loc.shape[:-1] + (1,))
        p = jax.nn.softmax(jnp.concatenate([s_loc, s_ctx, s_snk], axis=-1), axis=-1).astype(v.dtype)
        return (jnp.einsum('bkgql,blkd->bqkgd', p[..., :n_loc], vi)
                + jnp.einsum('bkgql,blkd->bqkgd', p[..., n_loc:n_loc + C], v_ctx))

    o = lax.map(one, (qb, k_win.swapaxes(0, 1), v_win.swapaxes(0, 1), valid))
    return o.swapaxes(0, 1).reshape(B, S, H, d)


def mla_qkv(p, q_norm, kv_norm, w_uq, w_ukv, rope):
    B, L, _ = p.shape
    dq, dkv, kr = split_last(p, (MLA_Q_RANK, MLA_KV_RANK, MLA_ROPE))
    q = (rms_norm(dq, q_norm) @ w_uq).reshape(B, L, MLA_HEADS, MLA_NOPE + MLA_ROPE)
    kv = (rms_norm(dkv, kv_norm) @ w_ukv).reshape(B, L, MLA_HEADS, MLA_NOPE + MLA_V)
    q_nope, q_rope = q[..., :MLA_NOPE], q[..., MLA_NOPE:]
    k_nope, v = kv[..., :MLA_NOPE], kv[..., MLA_NOPE:]
    k_rope = kr.reshape(B, L, 1, MLA_ROPE)
    if rope is not None:
        q_rope = apply_rope(q_rope, rope[0], rope[1])
        k_rope = apply_rope(k_rope, rope[0], rope[1])
    q = jnp.concatenate([q_nope, q_rope], axis=-1)
    k = jnp.concatenate([k_nope, jnp.broadcast_to(k_rope, (B, L, MLA_HEADS, MLA_ROPE))], axis=-1)
    return q, k, v


def gqa_qkv(p, n_heads, n_kv, hd):
    B, L, _ = p.shape
    q, k, v = split_last(p, (n_heads * hd, n_kv * hd, n_kv * hd))
    return q.reshape(B, L, n_heads, hd), k.reshape(B, L, n_kv, hd), v.reshape(B, L, n_kv, hd)


def rwkv7_features(p, mu, w0, w2, a0, a2, g2, k_k, k_a):
    B, L, _ = p.shape
    prev = jnp.pad(p[:, :-1], ((0, 0), (1, 0), (0, 0)))
    nxt = jnp.pad(p[:, 1:], ((0, 0), (0, 1), (0, 0)))
    p = (p + mu[0] * (prev - p) + mu[1] * (nxt - p)).astype(jnp.float32)
    r, k, v, wf, wb, af, ab, gi = split_last(
        p, (RWKV_W, RWKV_W, RWKV_W, RWKV_W_LORA, RWKV_W_LORA, RWKV_A_LORA, RWKV_A_LORA, RWKV_G_LORA))

    def heads(t):
        return t.reshape(B, L, RWKV_HEADS, RWKV_HEAD)

    kk = heads(k * k_k)
    kk = kk / jnp.maximum(jnp.sqrt(jnp.sum(kk * kk, axis=-1, keepdims=True)), 1e-12)
    decay, kd, ad = [], [], []
    for d, (w_in_d, a_in_d) in enumerate(((wf, af), (wb, ab))):
        lw = -jax.nn.softplus(-(w0[d] + jnp.tanh(w_in_d) @ w2[d])) - 0.5
        a = jax.nn.sigmoid(a0[d] + a_in_d @ a2[d])
        decay.append(heads(jnp.exp(-jnp.exp(lw))))
        ad.append(heads(a))
        kd.append(heads(k * (1.0 + (a - 1.0) * k_a)))
    g = jax.nn.sigmoid(gi) @ g2
    return {'r': heads(r), 'v': heads(v), 'kk': kk, 'g': g, 'decay': decay, 'k': kd, 'a': ad}


def wkv7_scan(state0, f, d, reverse):
    xs = tuple(t.swapaxes(0, 1) for t in (f['r'], f['decay'][d], f['k'][d], f['v'], f['kk'], f['a'][d]))

    def step(S, inp):
        r, w, k, v, kk, a = inp
        sa = jnp.einsum('bhij,bhj->bhi', S, kk)
        S = S * w[:, :, None, :] - sa[..., None] * (kk * a)[:, :, None, :] + v[..., None] * k[:, :, None, :]
        return S, jnp.einsum('bhij,bhj->bhi', S, r)

    S, ys = lax.scan(step, state0, xs, reverse=reverse)
    return S, ys.swapaxes(0, 1)


def rwkv7_output(f, o_f, o_b, r_k, ln_g, ln_b, dtype):
    o = o_f + o_b
    B, L = o.shape[0], o.shape[1]
    mu = jnp.mean(o, axis=-1, keepdims=True)
    var = jnp.mean(jnp.square(o - mu), axis=-1, keepdims=True)
    o = ((o - mu) * lax.rsqrt(var + RWKV_GN_EPS) * ln_g.reshape(RWKV_HEADS, RWKV_HEAD)
         + ln_b.reshape(RWKV_HEADS, RWKV_HEAD))
    bonus = jnp.sum(f['r'] * (f['k'][0] + f['k'][1]) * r_k, axis=-1, keepdims=True) * f['v']
    y = (o + bonus).reshape(B, L, RWKV_W) * f['g']
    return y.astype(dtype)


def merge_heads(t):
    return t.reshape(t.shape[0], t.shape[1], -1)


def merge_branches(ys, gate_pre, w_branch, w_out):
    B, L, _ = gate_pre.shape
    gates = jax.nn.sigmoid(gate_pre.astype(jnp.float32)).astype(gate_pre.dtype).reshape(B, L, N_BRANCH, D_MODEL)
    acc = gates[:, :, 0] * (ys[0] @ w_branch[0])
    for i in range(1, N_BRANCH):
        acc = acc + gates[:, :, i] * (ys[i] @ w_branch[i])
    return acc @ w_out


def peer_ffn(h, wq, keys, u_tab, v_tab):
    B, L, D = h.shape
    q = (h @ wq).reshape(B, L, PEER_HEADS, 2, PEER_DQ // 2).astype(jnp.float32)
    s1 = jnp.einsum('blhd,hkd->blhk', q[..., 0, :], keys[:, 0].astype(jnp.float32))
    s2 = jnp.einsum('blhd,hkd->blhk', q[..., 1, :], keys[:, 1].astype(jnp.float32))
    v1, i1 = lax.top_k(s1, PEER_TOPK)
    v2, i2 = lax.top_k(s2, PEER_TOPK)
    cand = (v1[..., :, None] + v2[..., None, :]).reshape(B, L, PEER_HEADS, PEER_TOPK * PEER_TOPK)
    cidx = (i1[..., :, None] * PEER_N_KEYS + i2[..., None, :]).reshape(B, L, PEER_HEADS, PEER_TOPK * PEER_TOPK)
    best, pos = lax.top_k(cand, PEER_TOPK)
    idx = jnp.take_along_axis(cidx, pos, axis=-1)
    wgt = jax.nn.softmax(best, axis=-1)
    E = PEER_HEADS * PEER_TOPK
    nblk = (B * L) // PEER_BLOCK
    hb = h.reshape(nblk, PEER_BLOCK, D)
    ib = idx.reshape(nblk, PEER_BLOCK, E)
    gb = wgt.reshape(nblk, PEER_BLOCK, E).astype(h.dtype)

    def one(args):
        ht, it, gt = args
        act = jax.nn.gelu(jnp.einsum('td,ted->te', ht, jnp.take(u_tab, it, axis=0)), approximate=False)
        return jnp.einsum('te,ted->td', act * gt, jnp.take(v_tab, it, axis=0))

    return lax.map(one, (hb, ib, gb)).reshape(B, L, D)


def setup_inputs(seed: int = 0) -> dict:
    key = jax.random.key(seed)
    ks = iter(jax.random.split(key, 64))
    L, D = DEPTH, D_MODEL

    def nrm(shape, scale):
        return scale * jax.random.normal(next(ks), shape, jnp.float32)

    def gain(shape):
        return 1.0 + nrm(shape, 0.02)

    return {
        'x': nrm((BATCH, SEQ, D), 1.0),
        'c': nrm((BATCH, D), 1.0),
        'ctx': nrm((BATCH, CTX_LEN, D), 1.0),
        'c_ctx': nrm((D,), 1.0),
        'ada_w': nrm((L, D, 6 * D), 0.5 * D ** -0.5),
        'ada_b': nrm((L, 6 * D), 0.02),
        'w_in': nrm((L, D, IN_WIDTH), D ** -0.5),
        'mla_q_norm': gain((L, MLA_Q_RANK)),
        'mla_kv_norm': gain((L, MLA_KV_RANK)),
        'mla_w_uq': nrm((L, MLA_Q_RANK, MLA_HEADS * (MLA_NOPE + MLA_ROPE)), MLA_Q_RANK ** -0.5),
        'mla_w_ukv': nrm((L, MLA_KV_RANK, MLA_HEADS * (MLA_NOPE + MLA_V)), MLA_KV_RANK ** -0.5),
        'rwkv_mu': jax.random.uniform(next(ks), (L, 2, RWKV_IN), jnp.float32, 0.0, 0.5),
        'rwkv_w0': jax.random.uniform(next(ks), (L, 2, RWKV_W), jnp.float32, -6.0, 1.0),
        'rwkv_w2': nrm((L, 2, RWKV_W_LORA, RWKV_W), 0.1 * RWKV_W_LORA ** -0.5),
        'rwkv_a0': nrm((L, 2, RWKV_W), 0.1),
        'rwkv_a2': nrm((L, 2, RWKV_A_LORA, RWKV_W), RWKV_A_LORA ** -0.5),
        'rwkv_g2': nrm((L, RWKV_G_LORA, RWKV_W), RWKV_G_LORA ** -0.5),
        'rwkv_k_k': 0.85 + nrm((L, RWKV_W), 0.02),
        'rwkv_k_a': gain((L, RWKV_W)),
        'rwkv_r_k': nrm((L, RWKV_HEADS, RWKV_HEAD), 0.1),
        'rwkv_ln_g': gain((L, RWKV_W)),
        'rwkv_ln_b': nrm((L, RWKV_W), 0.02),
        'gqa_q_norm': gain((L, GQA_HEAD)),
        'gqa_k_norm': gain((L, GQA_HEAD)),
        'win_sink': nrm((L, WIN_HEADS), 1.0),
        'w_branch': nrm((L, N_BRANCH, BRANCH_W, D), DEEPNORM_BETA * BRANCH_W ** -0.5),
        'w_out': nrm((L, D, D), DEEPNORM_BETA * D ** -0.5),
        'ln1_g': gain((L, D)),
        'ln1_b': nrm((L, D), 0.02),
        'peer_wq': nrm((L, D, PEER_HEADS * PEER_DQ), D ** -0.5),
        'peer_keys': nrm((L, PEER_HEADS, 2, PEER_N_KEYS, PEER_DQ // 2), (PEER_DQ // 2) ** -0.5),
        'peer_u': nrm((L, PEER_EXPERTS, D), D ** -0.5),
        'peer_v': nrm((L, PEER_EXPERTS, D), DEEPNORM_BETA),
        'ln2_g': gain((L, D)),
        'ln2_b': nrm((L, D), 0.02),
    }


def reference(x, c, ctx, c_ctx, ada_w, ada_b, w_in, mla_q_norm, mla_kv_norm, mla_w_uq, mla_w_ukv,
              rwkv_mu, rwkv_w0, rwkv_w2, rwkv_a0, rwkv_a2, rwkv_g2, rwkv_k_k, rwkv_k_a, rwkv_r_k,
              rwkv_ln_g, rwkv_ln_b, gqa_q_norm, gqa_k_norm, win_sink, w_branch, w_out, ln1_g, ln1_b,
              peer_wq, peer_keys, peer_u, peer_v, ln2_g, ln2_b):
    seq_len = x.shape[1]
    rows = seq_len // GRID_W
    rope_mla = axial_rope_tables(rows, MLA_ROPE)
    rope_head = axial_rope_tables(rows, GQA_HEAD)
    for l in range(DEPTH):
        mod_lat = jax.nn.silu(c) @ ada_w[l] + ada_b[l]
        mod_ctx = jax.nn.silu(c_ctx) @ ada_w[l] + ada_b[l]
        sh1, sc1, gt1, sh2, sc2, gt2 = jnp.split(mod_lat[:, None, :], 6, axis=-1)
        csh1, csc1, cgt1, csh2, csc2, cgt2 = jnp.split(mod_ctx, 6, axis=-1)

        p_lat = (x * (1 + sc1) + sh1) @ w_in[l]
        p_ctx = (ctx * (1 + csc1) + csh1) @ w_in[l]
        mla_lat, rwkv_lat, gqa_lat, win_lat, gate_lat = split_last(p_lat, GROUP_WIDTHS)
        mla_ctx, rwkv_ctx, gqa_ctx, win_ctx, gate_ctx = split_last(p_ctx, GROUP_WIDTHS)

        qa, ka, va = mla_qkv(mla_lat, mla_q_norm[l], mla_kv_norm[l], mla_w_uq[l], mla_w_ukv[l], rope_mla)
        qa_c, ka_c, va_c = mla_qkv(mla_ctx, mla_q_norm[l], mla_kv_norm[l], mla_w_uq[l], mla_w_ukv[l], None)
        ya = dense_attention(qa, jnp.concatenate([ka, ka_c], axis=1), jnp.concatenate([va, va_c], axis=1), MLA_SCALE)

        fb = rwkv7_features(rwkv_lat, rwkv_mu[l], rwkv_w0[l], rwkv_w2[l], rwkv_a0[l], rwkv_a2[l],
                            rwkv_g2[l], rwkv_k_k[l], rwkv_k_a[l])
        fb_c = rwkv7_features(rwkv_ctx, rwkv_mu[l], rwkv_w0[l], rwkv_w2[l], rwkv_a0[l], rwkv_a2[l],
                              rwkv_g2[l], rwkv_k_k[l], rwkv_k_a[l])
        s0 = jnp.zeros((ctx.shape[0], RWKV_HEADS, RWKV_HEAD, RWKV_HEAD), jnp.float32)
        sf_c, of_c = wkv7_scan(s0, fb_c, 0, False)
        sb_c, ob_c = wkv7_scan(s0, fb_c, 1, True)
        _, of = wkv7_scan(sf_c, fb, 0, False)
        _, ob = wkv7_scan(sb_c, fb, 1, True)
        yb = rwkv7_output(fb, of, ob, rwkv_r_k[l], rwkv_ln_g[l], rwkv_ln_b[l], x.dtype)

        qc, kc, vc = gqa_qkv(gqa_lat, GQA_HEADS, GQA_KV_HEADS, GQA_HEAD)
        qc_c, kc_c, vc_c = gqa_qkv(gqa_ctx, GQA_HEADS, GQA_KV_HEADS, GQA_HEAD)
        qc = apply_rope(rms_norm(qc, gqa_q_norm[l]), rope_head[0], rope_head[1])
        kc = apply_rope(rms_norm(kc, gqa_k_norm[l]), rope_head[0], rope_head[1])
        qc_c = rms_norm(qc_c, gqa_q_norm[l])
        kc_c = rms_norm(kc_c, gqa_k_norm[l])
        yc = dense_attention(qc, jnp.concatenate([kc, kc_c], axis=1), jnp.concatenate([vc, vc_c], axis=1), GQA_SCALE)

        qd, kd, vd = gqa_qkv(win_lat, WIN_HEADS, WIN_KV_HEADS, WIN_HEAD)
        qd_c, kd_c, vd_c = gqa_qkv(win_ctx, WIN_HEADS, WIN_KV_HEADS, WIN_HEAD)
        qd = apply_rope(qd, rope_head[0], rope_head[1])
        kd = apply_rope(kd, rope_head[0], rope_head[1])
        yd = window_attention(qd, kd, vd, kd_c, vd_c, win_sink[l], WIN_SCALE)

        mix = merge_branches((merge_heads(ya), yb, merge_heads(yc), merge_heads(yd)), gate_lat, w_branch[l], w_out[l])
        x_mid = layer_norm(DEEPNORM_ALPHA * x + gt1 * mix, ln1_g[l], ln1_b[l])
        ffn = peer_ffn(x_mid * (1 + sc2) + sh2, peer_wq[l], peer_keys[l], peer_u[l], peer_v[l])
        x_new = layer_norm(DEEPNORM_ALPHA * x_mid + gt2 * ffn, ln2_g[l], ln2_b[l])

        if l < DEPTH - 1:
            ya_c = dense_attention(qa_c, ka_c, va_c, MLA_SCALE)
            yb_c = rwkv7_output(fb_c, of_c, ob_c, rwkv_r_k[l], rwkv_ln_g[l], rwkv_ln_b[l], ctx.dtype)
            yc_c = dense_attention(qc_c, kc_c, vc_c, GQA_SCALE)
            yd_c = dense_attention(qd_c, kd_c, vd_c, WIN_SCALE, sink=win_sink[l])
            mix_c = merge_branches((merge_heads(ya_c), yb_c, merge_heads(yc_c), merge_heads(yd_c)),
                                   gate_ctx, w_branch[l], w_out[l])
            ctx_mid = layer_norm(DEEPNORM_ALPHA * ctx + cgt1 * mix_c, ln1_g[l], ln1_b[l])
            ffn_c = peer_ffn(ctx_mid * (1 + csc2) + csh2, peer_wq[l], peer_keys[l], peer_u[l], peer_v[l])
            ctx = layer_norm(DEEPNORM_ALPHA * ctx_mid + cgt2 * ffn_c, ln2_g[l], ln2_b[l])
        x = x_new
    return x
```

```python
import functools
import math

import jax
import jax.numpy as jnp
from jax import lax
from jax.experimental import pallas as pl
from jax.experimental.pallas import tpu as pltpu

F32 = jnp.float32
BF16 = jnp.bfloat16
I32 = jnp.int32
U32 = jnp.uint32

LANES = 128
GRID_W = 64
ROPE_THETA = 10000.0
NEG_INF = -1e30
LN_EPS = 1e-5
RMS_EPS = 1e-6

MLA_HEADS, MLA_Q_RANK, MLA_KV_RANK, MLA_NOPE, MLA_ROPE, MLA_V = 8, 256, 128, 64, 32, 64
MLA_SCALE = (MLA_NOPE + MLA_ROPE) ** -0.5
RWKV_HEADS, RWKV_HEAD = 8, 64
RWKV_W = RWKV_HEADS * RWKV_HEAD
RWKV_W_LORA, RWKV_A_LORA, RWKV_G_LORA = 64, 64, 128
RWKV_GN_EPS = 64e-5
RWKV_CHUNK = 128
GQA_HEADS, GQA_KV_HEADS, GQA_HEAD = 8, 2, 64
GQA_SCALE = GQA_HEAD ** -0.5
WIN_HEADS, WIN_KV_HEADS, WIN_HEAD, WINDOW = 8, 2, 64, 128
WIN_SCALE = WIN_HEAD ** -0.5
PEER_HEADS, PEER_N_KEYS, PEER_TOPK, PEER_DQ = 8, 128, 16, 128
N_BRANCH = 4

VMEM_LIMIT = 56 * 1024 * 1024


def _cparams(sem, vmem=None):
    return pltpu.CompilerParams(dimension_semantics=sem, vmem_limit_bytes=vmem or VMEM_LIMIT)


def _pick_tile(n, cap, mult=LANES):
    best = mult
    for t in range(mult, min(n, cap) + 1, mult):
        if n % t == 0:
            best = t
    return best


def _dot(a, b):
    return jnp.dot(a, b, preferred_element_type=F32)


def _dot_nt(a, b):
    return lax.dot_general(a, b, (((1,), (1,)), ((), ())), preferred_element_type=F32)


def _dot_tn(a, b):
    return lax.dot_general(a, b, (((0,), (0,)), ((), ())), preferred_element_type=F32)


def _layer_norm(y, g, b):
    mu = jnp.mean(y, axis=-1, keepdims=True)
    d = y - mu
    var = jnp.mean(d * d, axis=-1, keepdims=True)
    return d * lax.rsqrt(var + LN_EPS) * g + b


def _swap_pairs(x):
    n = x.shape[-1]
    lane = lax.broadcasted_iota(I32, x.shape, x.ndim - 1)
    nxt = pltpu.roll(x, n - 1, x.ndim - 1)
    prv = pltpu.roll(x, 1, x.ndim - 1)
    return jnp.where((lane & 1) == 0, nxt, prv)


def _rope(x, cos, sin_signed):
    return x * cos + _swap_pairs(x) * sin_signed


def _ada_kernel(c_ref, w_ref, b_ref, o_ref):
    c = c_ref[...]
    s = c * jax.nn.sigmoid(c)
    o_ref[...] = _dot(s.astype(BF16), w_ref[...].astype(BF16)) + b_ref[...]


def _ada(cvec, w, b):
    m, k = cvec.shape
    nc = w.shape[1]
    tn = _pick_tile(nc, 1536)
    return pl.pallas_call(
        _ada_kernel,
        out_shape=jax.ShapeDtypeStruct((m, nc), F32),
        grid=(nc // tn,),
        in_specs=[pl.BlockSpec((m, k), lambda j: (0, 0)),
                  pl.BlockSpec((k, tn), lambda j: (0, j)),
                  pl.BlockSpec((1, tn), lambda j: (0, j))],
        out_specs=pl.BlockSpec((m, tn), lambda j: (0, j)),
        compiler_params=_cparams(("parallel",)),
        name="ada_mod",
    )(cvec, w, b.reshape(1, nc))


def _mm_kernel(x_ref, mod_ref, w_ref, o_ref):
    m = mod_ref[0]
    xm = x_ref[...] * m[0:1, :] + m[1:2, :]
    o_ref[...] = _dot(xm.astype(BF16), w_ref[...]).astype(o_ref.dtype)


def _group_index(i, nb, nctx_b):
    return (i // nb) * 2 + (i % nb >= nctx_b).astype(I32)


def _modmm(x, mod, w, nb, nctx_b, tm):
    n, k = x.shape
    nc = w.shape[1]
    tn = _pick_tile(nc, 2048)
    return pl.pallas_call(
        _mm_kernel,
        out_shape=jax.ShapeDtypeStruct((n, nc), F32),
        grid=(nc // tn, n // tm),
        in_specs=[pl.BlockSpec((tm, k), lambda j, i: (i, 0)),
                  pl.BlockSpec((1, 2, k), lambda j, i: (_group_index(i, nb, nctx_b), 0, 0)),
                  pl.BlockSpec((k, tn), lambda j, i: (0, j))],
        out_specs=pl.BlockSpec((tm, tn), lambda j, i: (i, j)),
        compiler_params=_cparams(("parallel", "parallel")),
        name="in_proj",
    )(x, mod, w)


def _mla_prep_kernel(p_ref, qn_ref, kvn_ref, wq_ref, wk_ref, wv_ref, cos_ref, sin_ref,
                     q_ref, k_ref, v_ref):
    p = p_ref[...]
    dq = p[:, 0:MLA_Q_RANK]
    dkv = p[:, MLA_Q_RANK:MLA_Q_RANK + MLA_KV_RANK]
    krp = p[:, MLA_Q_RANK + MLA_KV_RANK:]
    qn = dq * lax.rsqrt(jnp.mean(dq * dq, axis=-1, keepdims=True) + RMS_EPS) * qn_ref[...]
    kvn = dkv * lax.rsqrt(jnp.mean(dkv * dkv, axis=-1, keepdims=True) + RMS_EPS) * kvn_ref[...]
    kvn = kvn.astype(BF16)
    cos = cos_ref[...]
    sin = sin_ref[...]
    cos_h = jnp.concatenate([cos] * MLA_HEADS, axis=1)
    sin_h = jnp.concatenate([sin] * MLA_HEADS, axis=1)
    q = _rope(_dot(qn.astype(BF16), wq_ref[...]), cos_h, sin_h) * MLA_SCALE
    kr = _rope(krp, cos, sin)
    k = _dot(kvn, wk_ref[...]) + jnp.concatenate([kr] * MLA_HEADS, axis=1)
    q_ref[...] = q.astype(BF16)
    k_ref[...] = k.astype(BF16)
    v_ref[...] = _dot(kvn, wv_ref[...]).astype(BF16)


def _mla_prep(pm, qn, kvn, wq, wk, wv, cos, sin, nb, tm):
    n = pm.shape[0]
    hw = MLA_HEADS * LANES
    full = lambda a: pl.BlockSpec(a.shape, lambda i: (0,) * a.ndim)
    out = jax.ShapeDtypeStruct((n, hw), BF16)
    return pl.pallas_call(
        _mla_prep_kernel,
        out_shape=(out, out, out),
        grid=(n // tm,),
        in_specs=[pl.BlockSpec((tm, pm.shape[1]), lambda i: (i, 0)),
                  full(qn), full(kvn), full(wq), full(wk), full(wv),
                  pl.BlockSpec((tm, LANES), lambda i: (i % nb, 0)),
                  pl.BlockSpec((tm, LANES), lambda i: (i % nb, 0))],
        out_specs=tuple(pl.BlockSpec((tm, hw), lambda i: (i, 0)) for _ in range(3)),
        compiler_params=_cparams(("parallel",)),
        name="mla_prep",
    )(pm, qn, kvn, wq, wk, wv, cos, sin)


def _gqa_prep_kernel(p_ref, qn_ref, kn_ref, cos_ref, sin_ref, q_ref, k_ref, v_ref,
                     *, n_q, n_kv, hd, scale, qk_norm):
    p = p_ref[...]
    cos = cos_ref[...]
    sin = sin_ref[...]

    def head(j, gain):
        x = p[:, j * LANES:(j + 1) * LANES]
        if qk_norm:
            ms = jnp.sum(x * x, axis=-1, keepdims=True) * (1.0 / hd)
            x = x * lax.rsqrt(ms + RMS_EPS) * gain
        return _rope(x, cos, sin)

    qg = qn_ref[...]
    kg = kn_ref[...]
    q = jnp.concatenate([head(j, qg) for j in range(n_q)], axis=1) * scale
    k = jnp.concatenate([head(n_q + j, kg) for j in range(n_kv)], axis=1)
    q_ref[...] = q.astype(BF16)
    k_ref[...] = k.astype(BF16)
    v_ref[...] = p[:, (n_q + n_kv) * LANES:].astype(BF16)


def _gqa_prep(pg, qn, kn, cos, sin, nb, tm, *, n_q, n_kv, hd, scale, qk_norm):
    n = pg.shape[0]
    full = lambda a: pl.BlockSpec(a.shape, lambda i: (0,) * a.ndim)
    kern = functools.partial(_gqa_prep_kernel, n_q=n_q, n_kv=n_kv, hd=hd, scale=scale,
                             qk_norm=qk_norm)
    return pl.pallas_call(
        kern,
        out_shape=(jax.ShapeDtypeStruct((n, n_q * LANES), BF16),
                   jax.ShapeDtypeStruct((n, n_kv * LANES), BF16),
                   jax.ShapeDtypeStruct((n, n_kv * LANES), BF16)),
        grid=(n // tm,),
        in_specs=[pl.BlockSpec((tm, pg.shape[1]), lambda i: (i, 0)),
                  full(qn), full(kn),
                  pl.BlockSpec((tm, LANES), lambda i: (i % nb, 0)),
                  pl.BlockSpec((tm, LANES), lambda i: (i % nb, 0))],
        out_specs=(pl.BlockSpec((tm, n_q * LANES), lambda i: (i, 0)),
                   pl.BlockSpec((tm, n_kv * LANES), lambda i: (i, 0)),
                   pl.BlockSpec((tm, n_kv * LANES), lambda i: (i, 0))),
        compiler_params=_cparams(("parallel",)),
        name="gqa_prep",
    )(pg, qn, kn, cos, sin)


def _attn_kernel(q_ref, k_ref, v_ref, o_ref, *, n_q, n_kv, tq, tk, nctx_q, nk_ctx, nk_all):
    qi = pl.program_id(1)
    nkb = jnp.where(qi < nctx_q, nk_ctx, nk_all)
    grp = n_q // n_kv
    rows = grp * tq
    for g in range(n_kv):
        qg = jnp.concatenate(
            [q_ref[0, :, (g * grp + a) * LANES:(g * grp + a + 1) * LANES] for a in range(grp)],
            axis=0)

        def body(kb, carry, g=g, qg=qg):
            m, l, acc = carry
            ks = pl.multiple_of(kb * tk, tk)
            kblk = k_ref[0, pl.ds(ks, tk), g * LANES:(g + 1) * LANES]
            vblk = v_ref[0, pl.ds(ks, tk), g * LANES:(g + 1) * LANES]
            s = _dot_nt(qg, kblk)
            m_new = jnp.maximum(m, jnp.max(s, axis=1, keepdims=True))
            a = jnp.exp(m - m_new)
            p = jnp.exp(s - m_new)
            l = a * l + jnp.sum(p, axis=1, keepdims=True)
            acc = a * acc + _dot(p.astype(BF16), vblk)
            return m_new, l, acc

        init = (jnp.full((rows, 1), NEG_INF, F32), jnp.zeros((rows, 1), F32),
                jnp.zeros((rows, LANES), F32))
        _, l, acc = lax.fori_loop(0, nkb, body, init)
        o = acc / l
        for a in range(grp):
            h = g * grp + a
            o_ref[0, :, h * LANES:(h + 1) * LANES] = o[a * tq:(a + 1) * tq].astype(o_ref.dtype)


def _attention(q, k, v, ctx_len, *, n_q, n_kv, tq, tk):
    b, l, _ = q.shape
    kern = functools.partial(_attn_kernel, n_q=n_q, n_kv=n_kv, tq=tq, tk=tk,
                             nctx_q=ctx_len // tq, nk_ctx=ctx_len // tk, nk_all=l // tk)
    return pl.pallas_call(
        kern,
        out_shape=jax.ShapeDtypeStruct((b, l, n_q * LANES), BF16),
        grid=(b, l // tq),
        in_specs=[pl.BlockSpec((1, tq, n_q * LANES), lambda bi, i: (bi, i, 0)),
                  pl.BlockSpec((1, l, n_kv * LANES), lambda bi, i: (bi, 0, 0)),
                  pl.BlockSpec((1, l, n_kv * LANES), lambda bi, i: (bi, 0, 0))],
        out_specs=pl.BlockSpec((1, tq, n_q * LANES), lambda bi, i: (bi, i, 0)),
        compiler_params=_cparams(("parallel", "arbitrary")),
        name="dense_attn",
    )(q, k, v)


def _win_kernel(sink_ref, q_ref, kc_ref, kp_ref, kcur_ref, kn_ref, vc_ref, vp_ref, vcur_ref,
                vn_ref, o_ref, *, n_q, n_kv, nctx_b, nb, ctx_len):
    qi = pl.program_id(1)
    w = WINDOW
    is_lat = (qi >= nctx_b).astype(I32)
    prev_ok = is_lat * (qi - 1 >= nctx_b).astype(I32)
    next_ok = is_lat * (qi + 1 < nb).astype(I32)
    nk = ctx_len + 3 * w
    r = lax.broadcasted_iota(I32, (w, nk), 0)
    c2 = lax.broadcasted_iota(I32, (w, nk), 1)
    c = c2 - ctx_len
    near = jnp.abs(r - (c - w)) <= w
    blk_ok = jnp.where(c < w, prev_ok, jnp.where(c < 2 * w, is_lat, next_ok)) > 0
    valid = jnp.logical_or(c2 < ctx_len, jnp.logical_and(near, blk_ok))
    bias = jnp.where(valid, 0.0, NEG_INF).astype(F32)
    grp = n_q // n_kv
    bias = jnp.concatenate([bias] * grp, axis=0)
    for g in range(n_kv):
        ls = slice(g * LANES, (g + 1) * LANES)
        qg = jnp.concatenate(
            [q_ref[0, :, (g * grp + a) * LANES:(g * grp + a + 1) * LANES] for a in range(grp)],
            axis=0)
        kcat = jnp.concatenate([kc_ref[0, :, ls], kp_ref[0, :, ls], kcur_ref[0, :, ls],
                                kn_ref[0, :, ls]], axis=0)
        vcat = jnp.concatenate([vc_ref[0, :, ls], vp_ref[0, :, ls], vcur_ref[0, :, ls],
                                vn_ref[0, :, ls]], axis=0)
        s = _dot_nt(qg, kcat) + bias
        sk = jnp.concatenate([jnp.full((w, 1), sink_ref[g * grp + a], F32) for a in range(grp)],
                             axis=0)
        m = jnp.maximum(jnp.max(s, axis=1, keepdims=True), sk)
        p = jnp.exp(s - m)
        den = jnp.sum(p, axis=1, keepdims=True) + jnp.exp(sk - m)
        o = _dot(p.astype(BF16), vcat) / den
        for a in range(grp):
            h = g * grp + a
            o_ref[0, :, h * LANES:(h + 1) * LANES] = o[a * w:(a + 1) * w].astype(o_ref.dtype)


def _window_attention(q, k, v, sink, ctx_len, *, n_q, n_kv):
    b, l, _ = q.shape
    w = WINDOW
    nb = l // w
    nctx_b = ctx_len // w
    kw = n_kv * LANES
    kern = functools.partial(_win_kernel, n_q=n_q, n_kv=n_kv, nctx_b=nctx_b, nb=nb,
                             ctx_len=ctx_len)
    ctx_spec = pl.BlockSpec((1, ctx_len, kw), lambda bi, i: (bi, 0, 0))
    prev_spec = pl.BlockSpec((1, w, kw), lambda bi, i: (bi, jnp.maximum(i - 1, 0), 0))
    cur_spec = pl.BlockSpec((1, w, kw), lambda bi, i: (bi, i, 0))
    next_spec = pl.BlockSpec((1, w, kw), lambda bi, i: (bi, jnp.minimum(i + 1, nb - 1), 0))
    return pl.pallas_call(
        kern,
        out_shape=jax.ShapeDtypeStruct((b, l, n_q * LANES), BF16),
        grid=(b, nb),
        in_specs=[pl.BlockSpec(memory_space=pltpu.SMEM),
                  pl.BlockSpec((1, w, n_q * LANES), lambda bi, i: (bi, i, 0)),
                  ctx_spec, prev_spec, cur_spec, next_spec,
                  ctx_spec, prev_spec, cur_spec, next_spec],
        out_specs=pl.BlockSpec((1, w, n_q * LANES), lambda bi, i: (bi, i, 0)),
        compiler_params=_cparams(("parallel", "arbitrary")),
        name="window_attn",
    )(sink, q, k, k, k, k, v, v, v, v)


def _rwkv_feat_kernel(p_ref, hp_ref, hn_ref, mu_ref, w0_ref, w2_ref, a0_ref, a2_ref, g2_ref,
                      kk_ref, ka_ref, rk_ref,
                      r_out, v_out, kk_out, g_out, bonus_out, lw_out, k_out, b_out,
                      *, tm, nb, nctx_b):
    i = pl.program_id(0) % nb
    seq_start = jnp.logical_or(i == 0, i == nctx_b)
    seq_end = jnp.logical_or(i == nctx_b - 1, i == nb - 1)
    p = p_ref[...]
    row = lax.broadcasted_iota(I32, p.shape, 0)
    first = jnp.where(seq_start, 0.0, hp_ref[7:8, :])
    last = jnp.where(seq_end, 0.0, hn_ref[0:1, :])
    prev = jnp.where(row == 0, first, pltpu.roll(p, 1, 0))
    nxt = jnp.where(row == tm - 1, last, pltpu.roll(p, tm - 1, 0))
    mu = mu_ref[...]
    ps = p + mu[0:1, :] * (prev - p) + mu[1:2, :] * (nxt - p)
    hw = RWKV_HEADS * LANES
    r = ps[:, 0:hw]
    k = ps[:, hw:2 * hw]
    v = ps[:, 2 * hw:3 * hw]
    wfb = ps[:, 3 * hw:3 * hw + LANES]
    afb = ps[:, 3 * hw + LANES:3 * hw + 2 * LANES]
    gi = ps[:, 3 * hw + 2 * LANES:]
    kkr = k * kk_ref[...]
    parts = []
    for h in range(RWKV_HEADS):
        x = kkr[:, h * LANES:(h + 1) * LANES]
        nrm = jnp.sqrt(jnp.sum(x * x, axis=-1, keepdims=True))
        parts.append(x / jnp.maximum(nrm, 1e-12))
    kk = jnp.concatenate(parts, axis=1)
    z = w0_ref[...] + _dot(jnp.tanh(wfb).astype(BF16), w2_ref[...])
    lw = -math.exp(-0.5) * jax.nn.sigmoid(z)
    a = jax.nn.sigmoid(a0_ref[...] + _dot(afb.astype(BF16), a2_ref[...]))
    ka = ka_ref[...]
    k0 = k * (1.0 + (a[:, 0:hw] - 1.0) * ka)
    k1 = k * (1.0 + (a[:, hw:] - 1.0) * ka)
    rkk = r * (k0 + k1) * rk_ref[...]
    bparts = []
    for h in range(RWKV_HEADS):
        sl = slice(h * LANES, (h + 1) * LANES)
        bparts.append(jnp.sum(rkk[:, sl], axis=-1, keepdims=True) * v[:, sl])
    r_out[...] = r
    v_out[...] = v
    kk_out[...] = kk
    g_out[...] = _dot(jax.nn.sigmoid(gi).astype(BF16), g2_ref[...])
    bonus_out[...] = jnp.concatenate(bparts, axis=1)
    lw_out[0] = lw[:, 0:hw]
    lw_out[1] = lw[:, hw:]
    k_out[0] = k0
    k_out[1] = k1
    b_out[0] = a[:, 0:hw] * kk
    b_out[1] = a[:, hw:] * kk


def _rwkv_features(pr, mu, w0, w2, a0, a2, g2, kk, ka, rk, nb, nctx_b, tm):
    n, wid = pr.shape
    hw = RWKV_HEADS * LANES
    full = lambda a: pl.BlockSpec(a.shape, lambda i: (0,) * a.ndim)
    kern = functools.partial(_rwkv_feat_kernel, tm=tm, nb=nb, nctx_b=nctx_b)
    one = jax.ShapeDtypeStruct((n, hw), F32)
    two = jax.ShapeDtypeStruct((2, n, hw), F32)
    s1 = pl.BlockSpec((tm, hw), lambda i: (i, 0))
    s2 = pl.BlockSpec((2, tm, hw), lambda i: (0, i, 0))
    t8 = tm // 8
    return pl.pallas_call(
        kern,
        out_shape=(one, one, one, one, one, two, two, two),
        grid=(n // tm,),
        in_specs=[pl.BlockSpec((tm, wid), lambda i: (i, 0)),
                  pl.BlockSpec((8, wid), lambda i: (jnp.maximum(i * t8 - 1, 0), 0)),
                  pl.BlockSpec((8, wid), lambda i: (jnp.minimum((i + 1) * t8, n // 8 - 1), 0)),
                  full(mu), full(w0), full(w2), full(a0), full(a2), full(g2),
                  full(kk), full(ka), full(rk)],
        out_specs=(s1, s1, s1, s1, s1, s2, s2, s2),
        compiler_params=_cparams(("parallel",)),
        name="rwkv_features",
    )(pr, pr, pr, mu, w0, w2, a0, a2, g2, kk, ka, rk)


def _split3_dot(mask_bf16, x):
    x1 = x.astype(BF16)
    r1 = x - x1.astype(F32)
    x2 = r1.astype(BF16)
    x3 = (r1 - x2.astype(F32)).astype(BF16)
    return _dot(mask_bf16, x1) + _dot(mask_bf16, x2) + _dot(mask_bf16, x3)


def _rwkv_scan_kernel(r_ref, v_ref, kk_ref, lw_ref, k_ref, b_ref, o_ref, s_ref):
    d = pl.program_id(1)
    c = pl.program_id(2)
    cs = RWKV_CHUNK

    @pl.when(c == 0)
    def _():
        s_ref[...] = jnp.zeros_like(s_ref)

    rev = d == 1
    t_i = lax.broadcasted_iota(I32, (cs, cs), 0)
    s_i = lax.broadcasted_iota(I32, (cs, cs), 1)
    order = jnp.where(rev, t_i - s_i, s_i - t_i)
    incl = order <= 0
    strict = order < 0
    eye = jnp.where(t_i == s_i, 1.0, 0.0).astype(F32)
    off_masks = []
    for lvl in range(int(math.log2(cs))):
        pair = (t_i >> (lvl + 1)) == (s_i >> (lvl + 1))
        half = (t_i >> lvl) != (s_i >> lvl)
        off = jnp.logical_and(jnp.logical_and(pair, half), strict)
        off_masks.append(jnp.where(off, 1.0, 0.0).astype(F32))
    lw = lw_ref[0]
    cum = _split3_dot(jnp.where(incl, 1.0, 0.0).astype(BF16), lw)
    p_in = jnp.exp(cum)
    p_inv = jnp.exp(-cum)
    p_ex = jnp.exp(cum - lw)
    tot = jnp.where(rev, cum[0:1, :], cum[cs - 1:cs, :])
    p_all = jnp.exp(tot)
    a_t = -kk_ref[...] * p_ex
    r_t = r_ref[...] * p_in
    k_t = k_ref[0] * p_inv
    b_t = b_ref[0] * p_inv
    v = v_ref[...]
    for h in range(RWKV_HEADS):
        sl = slice(h * LANES, (h + 1) * LANES)
        ah, rh, kh, bh, vh = a_t[:, sl], r_t[:, sl], k_t[:, sl], b_t[:, sl], v[:, sl]
        vb = vh.astype(BF16)
        bk = jnp.concatenate([bh, kh], axis=0).astype(BF16)
        gm = _dot_nt(jnp.concatenate([ah, rh], axis=0).astype(BF16), bk)
        m_ab = jnp.where(strict, gm[0:cs, 0:cs], 0.0)
        m_ak = jnp.where(strict, gm[0:cs, cs:], 0.0)
        n_rb = jnp.where(incl, gm[cs:, 0:cs], 0.0)
        n_rk = jnp.where(incl, gm[cs:, cs:], 0.0)
        x = eye + m_ab * off_masks[0]
        for lvl in range(1, len(off_masks)):
            xb = x.astype(BF16)
            x = x + _dot(_dot(xb, (m_ab * off_masks[lvl]).astype(BF16)).astype(BF16), xb)
        z0 = jnp.concatenate([ah, _dot(m_ak.astype(BF16), vb)], axis=1)
        z = _dot(x.astype(BF16), z0.astype(BF16))
        zb = z.astype(BF16)
        gy = _dot(n_rb.astype(BF16), zb)
        g_mat = rh + gy[:, 0:LANES]
        y0 = _dot(n_rk.astype(BF16), vb) + gy[:, LANES:]
        s0 = s_ref[h]
        s0b = s0.astype(BF16)
        u = _dot_nt(zb[:, 0:LANES], s0b) + z[:, LANES:]
        o_ref[0, :, sl] = _dot_nt(g_mat.astype(BF16), s0b) + y0
        uv = jnp.concatenate([u, vh], axis=0).astype(BF16)
        s_ref[h] = (s0 + _dot_tn(uv, bk)) * p_all[:, sl]


def _rwkv_scan(r, v, kk, lw, k, bb, bsz, nctx_c):
    n, hw = r.shape
    cs = RWKV_CHUNK
    nc = n // bsz // cs

    def blk(bi, d, c):
        rc = jnp.where(c < nctx_c, nctx_c - 1 - c, nc - 1 - (c - nctx_c))
        return bi * nc + jnp.where(d == 0, c, rc)

    s1 = pl.BlockSpec((cs, hw), lambda bi, d, c: (blk(bi, d, c), 0))
    s2 = pl.BlockSpec((1, cs, hw), lambda bi, d, c: (d, blk(bi, d, c), 0))
    return pl.pallas_call(
        _rwkv_scan_kernel,
        out_shape=jax.ShapeDtypeStruct((2, n, hw), F32),
        grid=(bsz, 2, nc),
        in_specs=[s1, s1, s1, s2, s2, s2],
        out_specs=s2,
        scratch_shapes=[pltpu.VMEM((RWKV_HEADS, LANES, LANES), F32)],
        compiler_params=_cparams(("parallel", "parallel", "arbitrary")),
        name="rwkv_scan",
    )(r, v, kk, lw, k, bb)


def _rwkv_out_kernel(o_ref, bonus_ref, g_ref, lng_ref, lnb_ref, y_ref):
    o = o_ref[0] + o_ref[1]
    lane = lax.broadcasted_iota(I32, (1, LANES), 1)
    real = lane < RWKV_HEAD
    lng = lng_ref[...]
    lnb = lnb_ref[...]
    parts = []
    for h in range(RWKV_HEADS):
        sl = slice(h * LANES, (h + 1) * LANES)
        x = o[:, sl]
        mu = jnp.sum(x, axis=-1, keepdims=True) * (1.0 / RWKV_HEAD)
        dlt = jnp.where(real, x - mu, 0.0)
        var = jnp.sum(dlt * dlt, axis=-1, keepdims=True) * (1.0 / RWKV_HEAD)
        parts.append(dlt * lax.rsqrt(var + RWKV_GN_EPS) * lng[:, sl] + lnb[:, sl])
    y = (jnp.concatenate(parts, axis=1) + bonus_ref[...]) * g_ref[...]
    y_ref[...] = y.astype(y_ref.dtype)


def _rwkv_out(o, bonus, g, lng, lnb, tm):
    _, n, hw = o.shape
    full = lambda a: pl.BlockSpec(a.shape, lambda i: (0,) * a.ndim)
    s1 = pl.BlockSpec((tm, hw), lambda i: (i, 0))
    return pl.pallas_call(
        _rwkv_out_kernel,
        out_shape=jax.ShapeDtypeStruct((n, hw), BF16),
        grid=(n // tm,),
        in_specs=[pl.BlockSpec((2, tm, hw), lambda i: (0, i, 0)), s1, s1, full(lng), full(lnb)],
        out_specs=s1,
        compiler_params=_cparams(("parallel",)),
        name="rwkv_out",
    )(o, bonus, g, lng, lnb)


def _merge_kernel(ya_ref, yb_ref, yc_ref, yd_ref, gate_ref, x_ref, gt_ref, wb_ref, wo_ref,
                  lng_ref, lnb_ref, o_ref, *, alpha, d):
    ys = (ya_ref, yb_ref, yc_ref, yd_ref)
    acc = None
    for i in range(N_BRANCH):
        gate = jax.nn.sigmoid(gate_ref[:, i * d:(i + 1) * d])
        term = gate * _dot(ys[i][...], wb_ref[i])
        acc = term if acc is None else acc + term
    mix = _dot(acc.astype(BF16), wo_ref[...])
    y = alpha * x_ref[...] + gt_ref[0] * mix
    o_ref[...] = _layer_norm(y, lng_ref[...], lnb_ref[...])


def _merge(ya, yb, yc, yd, gate, x, gt, wb, wo, lng, lnb, nb, nctx_b, tm, alpha):
    n, d = x.shape
    hw = ya.shape[1]
    full = lambda a: pl.BlockSpec(a.shape, lambda i: (0,) * a.ndim)
    sy = pl.BlockSpec((tm, hw), lambda i: (i, 0))
    kern = functools.partial(_merge_kernel, alpha=alpha, d=d)
    return pl.pallas_call(
        kern,
        out_shape=jax.ShapeDtypeStruct((n, d), F32),
        grid=(n // tm,),
        in_specs=[sy, sy, sy, sy,
                  pl.BlockSpec((tm, N_BRANCH * d), lambda i: (i, 0)),
                  pl.BlockSpec((tm, d), lambda i: (i, 0)),
                  pl.BlockSpec((1, 1, d), lambda i: (_group_index(i, nb, nctx_b), 0, 0)),
                  full(wb), full(wo), full(lng), full(lnb)],
        out_specs=pl.BlockSpec((tm, d), lambda i: (i, 0)),
        compiler_params=_cparams(("parallel",)),
        name="merge",
    )(ya, yb, yc, yd, gate, x, gt, wb, wo, lng, lnb)


def _extract_topk(src_ref, n_rows, k, val_ref, pos_ref):
    s = src_ref[...]
    rio = lax.broadcasted_iota(I32, s.shape, 0).astype(F32)
    for rnk in range(k):
        m = jnp.max(s, axis=0, keepdims=True)
        pos = jnp.min(jnp.where(s == m, rio, float(n_rows)), axis=0, keepdims=True)
        val_ref[rnk:rnk + 1, :] = m
        pos_ref[rnk:rnk + 1, :] = pos
        s = jnp.where(rio == pos, -jnp.inf, s)


def _peer_topk_kernel(x_ref, mod_ref, wq_ref, k1_ref, k2_ref, h_ref, idx_ref, wgt_ref,
                      q_s, s_s, v1_s, i1_s, v2_s, i2_s, cand_s, cidx_s, best_s, pos_s):
    hd = pl.program_id(1)

    @pl.when(hd == 0)
    def _():
        m = mod_ref[0]
        hh = x_ref[...] * m[0:1, :] + m[1:2, :]
        h_ref[...] = hh
        q = _dot(hh.astype(BF16), wq_ref[...])
        for a in range(PEER_HEADS):
            q_s[a] = q[:, a * LANES:(a + 1) * LANES].astype(BF16)

    qh = q_s[hd]
    s_s[...] = _dot_nt(k1_ref[0], qh)
    _extract_topk(s_s, PEER_N_KEYS, PEER_TOPK, v1_s, i1_s)
    s_s[...] = _dot_nt(k2_ref[0], qh)
    _extract_topk(s_s, PEER_N_KEYS, PEER_TOPK, v2_s, i2_s)
    v2 = v2_s[...]
    i2 = i2_s[...]
    for a in range(PEER_TOPK):
        cand_s[a * PEER_TOPK:(a + 1) * PEER_TOPK, :] = v1_s[a:a + 1, :] + v2
        cidx_s[a * PEER_TOPK:(a + 1) * PEER_TOPK, :] = i1_s[a:a + 1, :] * float(PEER_N_KEYS) + i2
    _extract_topk(cand_s, PEER_TOPK * PEER_TOPK, PEER_TOPK, best_s, pos_s)
    cidx = cidx_s[...]
    rio = lax.broadcasted_iota(I32, cidx.shape, 0).astype(F32)
    for rnk in range(PEER_TOPK):
        sel = rio == pos_s[rnk:rnk + 1, :]
        e_id = jnp.max(jnp.where(sel, cidx, -1.0), axis=0, keepdims=True)
        idx_ref[0, rnk:rnk + 1, :] = e_id.astype(I32)
    best = best_s[...]
    e = jnp.exp(best - best[0:1, :])
    wgt_ref[0] = e / jnp.sum(e, axis=0, keepdims=True)


def _peer_topk(x, mod, wq, k1, k2, nb, nctx_b, tm):
    n, d = x.shape
    tk = PEER_TOPK
    full = lambda a: pl.BlockSpec(a.shape, lambda i, h: (0,) * a.ndim)
    return pl.pallas_call(
        _peer_topk_kernel,
        out_shape=(jax.ShapeDtypeStruct((n, d), F32),
                   jax.ShapeDtypeStruct((PEER_HEADS, tk, n), I32),
                   jax.ShapeDtypeStruct((PEER_HEADS, tk, n), F32)),
        grid=(n // tm, PEER_HEADS),
        in_specs=[pl.BlockSpec((tm, d), lambda i, h: (i, 0)),
                  pl.BlockSpec((1, 2, d), lambda i, h: (_group_index(i, nb, nctx_b), 0, 0)),
                  full(wq),
                  pl.BlockSpec((1, PEER_N_KEYS, LANES), lambda i, h: (h, 0, 0)),
                  pl.BlockSpec((1, PEER_N_KEYS, LANES), lambda i, h: (h, 0, 0))],
        out_specs=(pl.BlockSpec((tm, d), lambda i, h: (i, 0)),
                   pl.BlockSpec((1, tk, tm), lambda i, h: (h, 0, i)),
                   pl.BlockSpec((1, tk, tm), lambda i, h: (h, 0, i))),
        scratch_shapes=[pltpu.VMEM((PEER_HEADS, tm, LANES), BF16),
                        pltpu.VMEM((PEER_N_KEYS, tm), F32),
                        pltpu.VMEM((tk, tm), F32), pltpu.VMEM((tk, tm), F32),
                        pltpu.VMEM((tk, tm), F32), pltpu.VMEM((tk, tm), F32),
                        pltpu.VMEM((tk * tk, tm), F32), pltpu.VMEM((tk * tk, tm), F32),
                        pltpu.VMEM((tk, tm), F32), pltpu.VMEM((tk, tm), F32)],
        compiler_params=_cparams(("parallel", "arbitrary")),
        name="peer_topk",
    )(x, mod, wq, k1, k2)


def _unpack_pair(w):
    lo = lax.bitcast_convert_type(w << 16, F32)
    hi = lax.bitcast_convert_type(w & jnp.uint32(0xFFFF0000), F32)
    return lo, hi


def _load_table(tab_hbm, tab_v, sem):
    @pl.when(pl.program_id(0) == 0)
    def _():
        cp = pltpu.make_async_copy(tab_hbm, tab_v, sem)
        cp.start()
        cp.wait()


def _peer_u_kernel(idx_ref, h_ref, par_ref, wgt_ref, exp_ref, gsel_ref, tab_hbm, c_ref,
                   tab_v, sem, stage, rowsum, *, tb):
    _load_table(tab_hbm, tab_v, sem)
    ne = PEER_HEADS * PEER_TOPK
    ones8 = jnp.ones((8, LANES), BF16)

    def token(t, carry):
        hv = h_ref[t]
        hlo = jnp.concatenate([hv[0:4], hv[0:4]], axis=0)
        hhi = jnp.concatenate([hv[4:8], hv[4:8]], axis=0)

        def pair(k, carry2):
            e = idx_ref[t * ne + k]
            lo, hi = _unpack_pair(tab_v[e >> 1])
            stage[k] = lo * hlo + hi * hhi
            return carry2

        lax.fori_loop(0, ne, pair, 0, unroll=8)
        prod = stage[...].reshape(ne * 8, LANES).astype(BF16)
        rowsum[pl.ds(t, 1), :] = _dot_nt(ones8, prod)[0:1, :]
        return carry

    lax.fori_loop(0, tb, token, 0)
    rs = rowsum[...]
    parx = _dot(par_ref[...].astype(BF16), exp_ref[...])
    lane = lax.broadcasted_iota(I32, (1, ne * 8), 1)
    half = ((lane & 7) >> 2).astype(F32)
    rm = jnp.where(parx == half, rs, 0.0)
    r1 = rm.astype(BF16)
    r2 = (rm - r1.astype(F32)).astype(BF16)
    gsel = gsel_ref[...]
    act = _dot(r1, gsel) + _dot(r2, gsel)
    gelu = 0.5 * act * (1.0 + lax.erf(act * (2.0 ** -0.5)))
    c_ref[...] = gelu * wgt_ref[...]


def _peer_u(idx_flat, h3, par, wgt, expand, gsel, tab, tb):
    n = h3.shape[0]
    ne = PEER_HEADS * PEER_TOPK
    full = lambda a: pl.BlockSpec(a.shape, lambda i: (0,) * a.ndim)
    kern = functools.partial(_peer_u_kernel, tb=tb)
    return pl.pallas_call(
        kern,
        out_shape=jax.ShapeDtypeStruct((n, ne), F32),
        grid=(n // tb,),
        in_specs=[pl.BlockSpec((tb * ne,), lambda i: (i,), memory_space=pltpu.SMEM),
                  pl.BlockSpec((tb, 8, LANES), lambda i: (i, 0, 0)),
                  pl.BlockSpec((tb, ne), lambda i: (i, 0)),
                  pl.BlockSpec((tb, ne), lambda i: (i, 0)),
                  full(expand), full(gsel),
                  pl.BlockSpec(memory_space=pl.ANY)],
        out_specs=pl.BlockSpec((tb, ne), lambda i: (i, 0)),
        scratch_shapes=[pltpu.VMEM(tab.shape, U32), pltpu.SemaphoreType.DMA,
                        pltpu.VMEM((ne, 8, LANES), F32), pltpu.VMEM((tb, ne * 8), F32)],
        compiler_params=_cparams(("arbitrary",)),
        name="peer_u",
    )(idx_flat, h3, par, wgt, expand, gsel, tab)


def _peer_v_kernel(idx_ref, ce_ref, co_ref, tab_hbm, o_ref, tab_v, sem, *, tb):
    _load_table(tab_hbm, tab_v, sem)
    ne = PEER_HEADS * PEER_TOPK
    low = lax.broadcasted_iota(I32, (8, LANES), 0) < 4

    def token(t, carry):
        def pair(k, acc):
            alo, ahi = acc
            j = t * ne + k
            lo, hi = _unpack_pair(tab_v[idx_ref[j] >> 1])
            cv = jnp.where(low, ce_ref[j], co_ref[j])
            return alo + cv * lo, ahi + cv * hi

        zero = jnp.zeros((8, LANES), F32)
        alo, ahi = lax.fori_loop(0, ne, pair, (zero, zero), unroll=8)
        o_ref[t] = jnp.concatenate([alo[0:4] + alo[4:8], ahi[0:4] + ahi[4:8]], axis=0)
        return carry

    lax.fori_loop(0, tb, token, 0)


def _peer_v(idx_flat, c_even, c_odd, tab, n, tb):
    ne = PEER_HEADS * PEER_TOPK
    sm = pl.BlockSpec((tb * ne,), lambda i: (i,), memory_space=pltpu.SMEM)
    kern = functools.partial(_peer_v_kernel, tb=tb)
    return pl.pallas_call(
        kern,
        out_shape=jax.ShapeDtypeStruct((n, 8, LANES), F32),
        grid=(n // tb,),
        in_specs=[sm, sm, sm, pl.BlockSpec(memory_space=pl.ANY)],
        out_specs=pl.BlockSpec((tb, 8, LANES), lambda i: (i, 0, 0)),
        scratch_shapes=[pltpu.VMEM(tab.shape, U32), pltpu.SemaphoreType.DMA],
        compiler_params=_cparams(("arbitrary",)),
        name="peer_v",
    )(idx_flat, c_even, c_odd, tab)


def _pack_table(tab):
    e, d = tab.shape
    bits = lax.bitcast_convert_type(tab.astype(BF16), jnp.uint16).astype(U32)
    words = bits[:, :d // 2] | (bits[:, d // 2:] << 16)
    return words.reshape(e // 2, 8, LANES)


def _ln_res_kernel(x_ref, f_ref, gt_ref, lng_ref, lnb_ref, o_ref, *, alpha):
    y = alpha * x_ref[...] + gt_ref[0] * f_ref[...]
    o_ref[...] = _layer_norm(y, lng_ref[...], lnb_ref[...])


def _ln_res(x, f, gt, lng, lnb, nb, nctx_b, tm, alpha):
    n, d = x.shape
    full = lambda a: pl.BlockSpec(a.shape, lambda i: (0,) * a.ndim)
    s1 = pl.BlockSpec((tm, d), lambda i: (i, 0))
    return pl.pallas_call(
        functools.partial(_ln_res_kernel, alpha=alpha),
        out_shape=jax.ShapeDtypeStruct((n, d), F32),
        grid=(n // tm,),
        in_specs=[s1, s1,
                  pl.BlockSpec((1, 1, d), lambda i: (_group_index(i, nb, nctx_b), 0, 0)),
                  full(lng), full(lnb)],
        out_specs=s1,
        compiler_params=_cparams(("parallel",)),
        name="ln_res",
    )(x, f, gt, lng, lnb)


def _pad_heads(w, n_heads, hd):
    lead = w.shape[:-1]
    w = w.reshape(lead + (n_heads, hd))
    w = jnp.pad(w, [(0, 0)] * len(lead) + [(0, 0), (0, LANES - hd)])
    return w.reshape(lead + (n_heads * LANES,))


def _pad_head_rows(w, n_heads, hd):
    d = w.shape[-1]
    w = w.reshape(n_heads, hd, d)
    w = jnp.pad(w, [(0, 0), (0, LANES - hd), (0, 0)])
    return w.reshape(n_heads * LANES, d)


def _rope_tables(rows, rot_dim, lane_off, ctx_len):
    r_idx = jnp.repeat(jnp.arange(rows), GRID_W).astype(F32)
    c_idx = jnp.tile(jnp.arange(GRID_W), rows).astype(F32)
    n = rot_dim // 4
    inv = ROPE_THETA ** (-jnp.arange(n, dtype=F32) / n)
    ang = jnp.concatenate([r_idx[:, None] * inv, c_idx[:, None] * inv], axis=-1)
    cos = jnp.repeat(jnp.cos(ang), 2, axis=-1)
    sin = jnp.repeat(jnp.sin(ang), 2, axis=-1) * jnp.tile(jnp.array([-1.0, 1.0], F32), rot_dim // 2)
    s = ang.shape[0]
    cos_t = jnp.ones((ctx_len + s, LANES), F32).at[ctx_len:, lane_off:lane_off + rot_dim].set(cos)
    sin_t = jnp.zeros((ctx_len + s, LANES), F32).at[ctx_len:, lane_off:lane_off + rot_dim].set(sin)
    return cos_t, sin_t


def _split_cols(w, widths):
    out, start = [], 0
    for wd in widths:
        out.append(w[..., start:start + wd])
        start += wd
    return out


def _forward(x, c, ctx, c_ctx, ada_w, ada_b, w_in, mla_q_norm, mla_kv_norm, mla_w_uq, mla_w_ukv,
             rwkv_mu, rwkv_w0, rwkv_w2, rwkv_a0, rwkv_a2, rwkv_g2, rwkv_k_k, rwkv_k_a, rwkv_r_k,
             rwkv_ln_g, rwkv_ln_b, gqa_q_norm, gqa_k_norm, win_sink, w_branch, w_out, ln1_g, ln1_b,
             peer_wq, peer_keys, peer_u, peer_v, ln2_g, ln2_b):
    stages = []
    bsz, seq, d = x.shape
    ctx_len = ctx.shape[1]
    depth = ada_w.shape[0]
    alpha = (2 * depth) ** 0.25
    l_tot = ctx_len + seq
    n = bsz * l_tot
    tm = 256
    tm_feat = 128
    assert ctx_len % tm == 0 and seq % tm == 0 and seq % GRID_W == 0
    nb = l_tot // tm
    nctx_b = ctx_len // tm
    rows = seq // GRID_W
    ne = PEER_HEADS * PEER_TOPK

    xs = jnp.concatenate([ctx, x], axis=1).reshape(n, d)
    cos_m, sin_m = _rope_tables(rows, MLA_ROPE, MLA_NOPE, ctx_len)
    cos_h, sin_h = _rope_tables(rows, GQA_HEAD, 0, ctx_len)

    m_rows = 16
    cvec = jnp.zeros((m_rows, d), F32).at[:bsz].set(c).at[bsz].set(c_ctx)
    mla_in = MLA_Q_RANK + MLA_KV_RANK + MLA_ROPE
    rwkv_in = 3 * RWKV_W + 2 * RWKV_W_LORA + 2 * RWKV_A_LORA + RWKV_G_LORA
    gqa_in = (GQA_HEADS + 2 * GQA_KV_HEADS) * GQA_HEAD
    win_in = (WIN_HEADS + 2 * WIN_KV_HEADS) * WIN_HEAD
    rw_widths = (RWKV_W, RWKV_W, RWKV_W, RWKV_W_LORA, RWKV_W_LORA, RWKV_A_LORA, RWKV_A_LORA,
                 RWKV_G_LORA)

    lane8 = jnp.arange(ne * 8)
    expand = (lane8[None, :] // 8 == jnp.arange(ne)[:, None]).astype(BF16)
    gsel = expand.T

    for lyr in range(depth):
        mod = _ada(cvec, ada_w[lyr], ada_b[lyr])
        chunks = [mod[:, i * d:(i + 1) * d] for i in range(6)]

        def table(ch):
            lat = ch[:bsz]
            cx = jnp.broadcast_to(ch[bsz][None], (bsz, d))
            return jnp.stack([cx, lat], axis=1).reshape(bsz * 2, d)

        sh1, sc1, gt1, sh2, sc2, gt2 = [table(ch) for ch in chunks]
        mod1 = jnp.stack([1.0 + sc1, sh1], axis=1)
        mod2 = jnp.stack([1.0 + sc2, sh2], axis=1)
        gt1 = gt1[:, None, :]
        gt2 = gt2[:, None, :]

        wi = w_in[lyr]
        w_mla, w_rw, w_gq, w_wn, w_gate = _split_cols(wi, (mla_in, rwkv_in, gqa_in, win_in, N_BRANCH * d))
        zc = lambda k: jnp.zeros((d, k), F32)
        w_mla_p = jnp.concatenate([w_mla[:, :MLA_Q_RANK + MLA_KV_RANK], zc(MLA_NOPE),
                                   w_mla[:, MLA_Q_RANK + MLA_KV_RANK:], zc(LANES - MLA_NOPE - MLA_ROPE)],
                                  axis=1)
        rr, rk_, rv, rwf, rwb, raf, rab, rgi = _split_cols(w_rw, rw_widths)
        hp = lambda w: _pad_heads(w, RWKV_HEADS, RWKV_HEAD)
        w_rw_p = jnp.concatenate([hp(rr), hp(rk_), hp(rv), rwf, rwb, raf, rab, rgi], axis=1)
        mu_parts = _split_cols(rwkv_mu[lyr], rw_widths)
        mu_p = jnp.concatenate([hp(mu_parts[0]), hp(mu_parts[1]), hp(mu_parts[2])] + mu_parts[3:], axis=1)

        def gqa_cols(w, nq, nkv, hd):
            q_, k_, v_ = _split_cols(w, (nq * hd, nkv * hd, nkv * hd))
            return jnp.concatenate([_pad_heads(q_, nq, hd), _pad_heads(k_, nkv, hd),
                                    _pad_heads(v_, nkv, hd)], axis=1)

        w_gq_p = gqa_cols(w_gq, GQA_HEADS, GQA_KV_HEADS, GQA_HEAD)
        w_wn_p = gqa_cols(w_wn, WIN_HEADS, WIN_KV_HEADS, WIN_HEAD)

        pm = _modmm(xs, mod1, w_mla_p.astype(BF16), nb, nctx_b, tm)
        pr = _modmm(xs, mod1, w_rw_p.astype(BF16), nb, nctx_b, tm)
        pg = _modmm(xs, mod1, w_gq_p.astype(BF16), nb, nctx_b, tm)
        pw = _modmm(xs, mod1, w_wn_p.astype(BF16), nb, nctx_b, tm)
        gate = _modmm(xs, mod1, w_gate.astype(BF16), nb, nctx_b, tm)

        uq = mla_w_uq[lyr].reshape(MLA_Q_RANK, MLA_HEADS, MLA_NOPE + MLA_ROPE)
        uq = jnp.pad(uq, [(0, 0), (0, 0), (0, LANES - MLA_NOPE - MLA_ROPE)]).reshape(MLA_Q_RANK, -1)
        ukv = mla_w_ukv[lyr].reshape(MLA_KV_RANK, MLA_HEADS, MLA_NOPE + MLA_V)
        uk = jnp.pad(ukv[:, :, :MLA_NOPE], [(0, 0), (0, 0), (0, LANES - MLA_NOPE)]).reshape(MLA_KV_RANK, -1)
        uv = jnp.pad(ukv[:, :, MLA_NOPE:], [(0, 0), (0, 0), (0, LANES - MLA_V)]).reshape(MLA_KV_RANK, -1)
        qa, ka, va = _mla_prep(pm, mla_q_norm[lyr][None], mla_kv_norm[lyr][None], uq.astype(BF16),
                               uk.astype(BF16), uv.astype(BF16), cos_m, sin_m, nb, tm)
        r3 = lambda a: a.reshape(bsz, l_tot, a.shape[-1])
        ya = _attention(r3(qa), r3(ka), r3(va), ctx_len, n_q=MLA_HEADS, n_kv=MLA_HEADS, tq=256, tk=256)

        zl = jnp.zeros((RWKV_W_LORA, RWKV_HEADS * LANES), F32)
        w2c = jnp.concatenate([jnp.concatenate([hp(rwkv_w2[lyr, 0]), zl], axis=1),
                               jnp.concatenate([zl, hp(rwkv_w2[lyr, 1])], axis=1)], axis=0)
        a2c = jnp.concatenate([jnp.concatenate([hp(rwkv_a2[lyr, 0]), zl], axis=1),
                               jnp.concatenate([zl, hp(rwkv_a2[lyr, 1])], axis=1)], axis=0)
        w0c = jnp.concatenate([hp(rwkv_w0[lyr, 0]), hp(rwkv_w0[lyr, 1])])[None]
        a0c = jnp.concatenate([hp(rwkv_a0[lyr, 0]), hp(rwkv_a0[lyr, 1])])[None]
        feats = _rwkv_features(pr, mu_p, w0c, w2c.astype(BF16), a0c, a2c.astype(BF16),
                               hp(rwkv_g2[lyr]).astype(BF16), hp(rwkv_k_k[lyr])[None],
                               hp(rwkv_k_a[lyr])[None], hp(rwkv_r_k[lyr].reshape(-1))[None],
                               l_tot // tm_feat, ctx_len // tm_feat, tm_feat)
        f_r, f_v, f_kk, f_g, f_bonus, f_lw, f_k, f_b = feats
        o_scan = _rwkv_scan(f_r, f_v, f_kk, f_lw, f_k, f_b, bsz, ctx_len // RWKV_CHUNK)
        yb = _rwkv_out(o_scan, f_bonus, f_g, hp(rwkv_ln_g[lyr])[None], hp(rwkv_ln_b[lyr])[None], tm)

        pad_g = lambda g: jnp.pad(g, (0, LANES - g.shape[0]))[None]
        qc, kc, vc = _gqa_prep(pg, pad_g(gqa_q_norm[lyr]), pad_g(gqa_k_norm[lyr]), cos_h, sin_h,
                               nb, tm, n_q=GQA_HEADS, n_kv=GQA_KV_HEADS, hd=GQA_HEAD,
                               scale=GQA_SCALE, qk_norm=True)
        yc = _attention(r3(qc), r3(kc), r3(vc), ctx_len, n_q=GQA_HEADS, n_kv=GQA_KV_HEADS, tq=128, tk=256)

        ones_g = jnp.ones((1, LANES), F32)
        qd, kd, vd = _gqa_prep(pw, ones_g, ones_g, cos_h, sin_h, nb, tm, n_q=WIN_HEADS,
                               n_kv=WIN_KV_HEADS, hd=WIN_HEAD, scale=WIN_SCALE, qk_norm=False)
        yd = _window_attention(r3(qd), r3(kd), r3(vd), win_sink[lyr], ctx_len,
                               n_q=WIN_HEADS, n_kv=WIN_KV_HEADS)

        wb = jnp.stack([_pad_head_rows(w_branch[lyr, i], 8, 64) for i in range(N_BRANCH)]).astype(BF16)
        x_mid = _merge(ya.reshape(n, -1), yb, yc.reshape(n, -1), yd.reshape(n, -1), gate, xs, gt1,
                       wb, w_out[lyr].astype(BF16), ln1_g[lyr][None], ln1_b[lyr][None],
                       nb, nctx_b, tm, alpha)

        keys = peer_keys[lyr]
        half = PEER_DQ // 2
        k1 = jnp.pad(keys[:, 0], [(0, 0), (0, 0), (0, LANES - half)]).astype(BF16)
        k2 = jnp.pad(keys[:, 1], [(0, 0), (0, 0), (LANES - half, 0)]).astype(BF16)
        h_in, idx_t, wgt_t = _peer_topk(x_mid, mod2, peer_wq[lyr].astype(BF16), k1, k2, nb, nctx_b, tm)
        idx = idx_t.reshape(ne, n).T
        wgt = wgt_t.reshape(ne, n).T
        idx_flat = idx.reshape(n * ne)
        par = (idx & 1).astype(F32)
        tb = 32
        cw = _peer_u(idx_flat, h_in.reshape(n, 8, LANES), par, wgt, expand, gsel,
                     _pack_table(peer_u[lyr]), tb)
        c_even = jnp.where(par == 0.0, cw, 0.0).reshape(n * ne)
        c_odd = jnp.where(par == 1.0, cw, 0.0).reshape(n * ne)
        ffn = _peer_v(idx_flat, c_even, c_odd, _pack_table(peer_v[lyr]), n, tb).reshape(n, d)
        xs = _ln_res(x_mid, ffn, gt2, ln2_g[lyr][None], ln2_b[lyr][None], nb, nctx_b, tm, alpha)
        stages.append(dict(ya=ya, yb=yb, yc=yc, yd=yd, x_mid=x_mid, idx=idx, wgt=wgt, ffn=ffn,
                           x_out=xs))

    return xs.reshape(bsz, l_tot, d)[:, ctx_len:, :], stages


def kernel(x, c, ctx, c_ctx, ada_w, ada_b, w_in, mla_q_norm, mla_kv_norm, mla_w_uq, mla_w_ukv,
           rwkv_mu, rwkv_w0, rwkv_w2, rwkv_a0, rwkv_a2, rwkv_g2, rwkv_k_k, rwkv_k_a, rwkv_r_k,
           rwkv_ln_g, rwkv_ln_b, gqa_q_norm, gqa_k_norm, win_sink, w_branch, w_out, ln1_g, ln1_b,
           peer_wq, peer_keys, peer_u, peer_v, ln2_g, ln2_b):
    out, _ = _forward(x, c, ctx, c_ctx, ada_w, ada_b, w_in, mla_q_norm, mla_kv_norm, mla_w_uq,
                      mla_w_ukv, rwkv_mu, rwkv_w0, rwkv_w2, rwkv_a0, rwkv_a2, rwkv_g2, rwkv_k_k,
                      rwkv_k_a, rwkv_r_k, rwkv_ln_g, rwkv_ln_b, gqa_q_norm, gqa_k_norm, win_sink,
                      w_branch, w_out, ln1_g, ln1_b, peer_wq, peer_keys, peer_u, peer_v, ln2_g,
                      ln2_b)
    return out
```

```python
import functools
import math

import jax
import jax.numpy as jnp
from jax import lax
from jax.experimental import pallas as pl
from jax.experimental.pallas import tpu as pltpu

F32 = jnp.float32
BF16 = jnp.bfloat16
I32 = jnp.int32
U32 = jnp.uint32

LANES = 128
GRID_W = 64
ROPE_THETA = 10000.0
NEG_INF = -1e30
LN_EPS = 1e-5
RMS_EPS = 1e-6

MLA_HEADS, MLA_Q_RANK, MLA_KV_RANK, MLA_NOPE, MLA_ROPE, MLA_V = 8, 256, 128, 64, 32, 64
MLA_SCALE = (MLA_NOPE + MLA_ROPE) ** -0.5
RWKV_HEADS, RWKV_HEAD = 8, 64
RWKV_W = RWKV_HEADS * RWKV_HEAD
RWKV_W_LORA, RWKV_A_LORA, RWKV_G_LORA = 64, 64, 128
RWKV_GN_EPS = 64e-5
RWKV_CHUNK = 128
GQA_HEADS, GQA_KV_HEADS, GQA_HEAD = 8, 2, 64
GQA_SCALE = GQA_HEAD ** -0.5
WIN_HEADS, WIN_KV_HEADS, WIN_HEAD, WINDOW = 8, 2, 64, 128
WIN_SCALE = WIN_HEAD ** -0.5
PEER_HEADS, PEER_N_KEYS, PEER_TOPK, PEER_DQ = 8, 128, 16, 128
N_BRANCH = 4

VMEM_LIMIT = 56 * 1024 * 1024


def _cparams(sem, vmem=None):
    return pltpu.CompilerParams(dimension_semantics=sem, vmem_limit_bytes=vmem or VMEM_LIMIT)


def _pick_tile(n, cap, mult=LANES):
    best = mult
    for t in range(mult, min(n, cap) + 1, mult):
        if n % t == 0:
            best = t
    return best


def _dot(a, b):
    return jnp.dot(a, b, preferred_element_type=F32)


def _dot_nt(a, b):
    return lax.dot_general(a, b, (((1,), (1,)), ((), ())), preferred_element_type=F32)


def _dot_tn(a, b):
    return lax.dot_general(a, b, (((0,), (0,)), ((), ())), preferred_element_type=F32)


ONES_LANE = 64


def _with_ones_lane(v):
    lane = lax.broadcasted_iota(I32, v.shape, v.ndim - 1)
    return jnp.where((lane & (LANES - 1)) == ONES_LANE, 1.0, v)


def _layer_norm(y, g, b):
    mu = jnp.mean(y, axis=-1, keepdims=True)
    d = y - mu
    var = jnp.mean(d * d, axis=-1, keepdims=True)
    return d * lax.rsqrt(var + LN_EPS) * g + b


def _swap_pairs(x):
    n = x.shape[-1]
    lane = lax.broadcasted_iota(I32, x.shape, x.ndim - 1)
    nxt = pltpu.roll(x, n - 1, x.ndim - 1)
    prv = pltpu.roll(x, 1, x.ndim - 1)
    return jnp.where((lane & 1) == 0, nxt, prv)


def _rope(x, cos, sin_signed):
    return x * cos + _swap_pairs(x) * sin_signed


def _ada_kernel(c_ref, w_ref, b_ref, o_ref):
    c = c_ref[...]
    s = c * jax.nn.sigmoid(c)
    o_ref[...] = _dot(s.astype(BF16), w_ref[...].astype(BF16)) + b_ref[...]


def _ada(cvec, w, b):
    m, k = cvec.shape
    nc = w.shape[1]
    tn = _pick_tile(nc, 1536)
    return pl.pallas_call(
        _ada_kernel,
        out_shape=jax.ShapeDtypeStruct((m, nc), F32),
        grid=(nc // tn,),
        in_specs=[pl.BlockSpec((m, k), lambda j: (0, 0)),
                  pl.BlockSpec((k, tn), lambda j: (0, j)),
                  pl.BlockSpec((1, tn), lambda j: (0, j))],
        out_specs=pl.BlockSpec((m, tn), lambda j: (0, j)),
        compiler_params=_cparams(("parallel",)),
        name="ada_mod",
    )(cvec, w, b.reshape(1, nc))


def _mm_kernel(x_ref, mod_ref, w_ref, o_ref):
    m = mod_ref[0]
    xm = x_ref[...] * m[0:1, :] + m[1:2, :]
    o_ref[...] = _dot(xm.astype(BF16), w_ref[...]).astype(o_ref.dtype)


def _group_index(i, nb, nctx_b):
    return (i // nb) * 2 + (i % nb >= nctx_b).astype(I32)


def _modmm(x, mod, w, nb, nctx_b, tm):
    n, k = x.shape
    nc = w.shape[1]
    tn = _pick_tile(nc, 2048)
    return pl.pallas_call(
        _mm_kernel,
        out_shape=jax.ShapeDtypeStruct((n, nc), F32),
        grid=(nc // tn, n // tm),
        in_specs=[pl.BlockSpec((tm, k), lambda j, i: (i, 0)),
                  pl.BlockSpec((1, 2, k), lambda j, i: (_group_index(i, nb, nctx_b), 0, 0)),
                  pl.BlockSpec((k, tn), lambda j, i: (0, j))],
        out_specs=pl.BlockSpec((tm, tn), lambda j, i: (i, j)),
        compiler_params=_cparams(("parallel", "parallel")),
        name="in_proj",
    )(x, mod, w)


def _mla_prep_kernel(p_ref, qn_ref, kvn_ref, wq_ref, wk_ref, wv_ref, cos_ref, sin_ref,
                     q_ref, k_ref, v_ref):
    p = p_ref[...]
    dq = p[:, 0:MLA_Q_RANK]
    dkv = p[:, MLA_Q_RANK:MLA_Q_RANK + MLA_KV_RANK]
    krp = p[:, MLA_Q_RANK + MLA_KV_RANK:]
    qn = dq * lax.rsqrt(jnp.mean(dq * dq, axis=-1, keepdims=True) + RMS_EPS) * qn_ref[...]
    kvn = dkv * lax.rsqrt(jnp.mean(dkv * dkv, axis=-1, keepdims=True) + RMS_EPS) * kvn_ref[...]
    kvn = kvn.astype(BF16)
    cos = cos_ref[...]
    sin = sin_ref[...]
    cos_h = jnp.concatenate([cos] * MLA_HEADS, axis=1)
    sin_h = jnp.concatenate([sin] * MLA_HEADS, axis=1)
    q = _rope(_dot(qn.astype(BF16), wq_ref[...]), cos_h, sin_h) * MLA_SCALE
    kr = _rope(krp, cos, sin)
    k = _dot(kvn, wk_ref[...]) + jnp.concatenate([kr] * MLA_HEADS, axis=1)
    q_ref[...] = q.astype(BF16)
    k_ref[...] = k.astype(BF16)
    v_ref[...] = _with_ones_lane(_dot(kvn, wv_ref[...])).astype(BF16)


def _mla_prep(pm, qn, kvn, wq, wk, wv, cos, sin, nb, tm):
    n = pm.shape[0]
    hw = MLA_HEADS * LANES
    full = lambda a: pl.BlockSpec(a.shape, lambda i: (0,) * a.ndim)
    out = jax.ShapeDtypeStruct((n, hw), BF16)
    return pl.pallas_call(
        _mla_prep_kernel,
        out_shape=(out, out, out),
        grid=(n // tm,),
        in_specs=[pl.BlockSpec((tm, pm.shape[1]), lambda i: (i, 0)),
                  full(qn), full(kvn), full(wq), full(wk), full(wv),
                  pl.BlockSpec((tm, LANES), lambda i: (i % nb, 0)),
                  pl.BlockSpec((tm, LANES), lambda i: (i % nb, 0))],
        out_specs=tuple(pl.BlockSpec((tm, hw), lambda i: (i, 0)) for _ in range(3)),
        compiler_params=_cparams(("parallel",)),
        name="mla_prep",
    )(pm, qn, kvn, wq, wk, wv, cos, sin)


def _gqa_prep_kernel(p_ref, qn_ref, kn_ref, cos_ref, sin_ref, q_ref, k_ref, v_ref,
                     *, n_q, n_kv, hd, scale, qk_norm):
    p = p_ref[...]
    cos = cos_ref[...]
    sin = sin_ref[...]

    def head(j, gain):
        x = p[:, j * LANES:(j + 1) * LANES]
        if qk_norm:
            ms = jnp.sum(x * x, axis=-1, keepdims=True) * (1.0 / hd)
            x = x * lax.rsqrt(ms + RMS_EPS) * gain
        return _rope(x, cos, sin)

    qg = qn_ref[...]
    kg = kn_ref[...]
    q = jnp.concatenate([head(j, qg) for j in range(n_q)], axis=1) * scale
    k = jnp.concatenate([head(n_q + j, kg) for j in range(n_kv)], axis=1)
    q_ref[...] = q.astype(BF16)
    k_ref[...] = k.astype(BF16)
    v_ref[...] = _with_ones_lane(p[:, (n_q + n_kv) * LANES:]).astype(BF16)


def _gqa_prep(pg, qn, kn, cos, sin, nb, tm, *, n_q, n_kv, hd, scale, qk_norm):
    n = pg.shape[0]
    full = lambda a: pl.BlockSpec(a.shape, lambda i: (0,) * a.ndim)
    kern = functools.partial(_gqa_prep_kernel, n_q=n_q, n_kv=n_kv, hd=hd, scale=scale,
                             qk_norm=qk_norm)
    return pl.pallas_call(
        kern,
        out_shape=(jax.ShapeDtypeStruct((n, n_q * LANES), BF16),
                   jax.ShapeDtypeStruct((n, n_kv * LANES), BF16),
                   jax.ShapeDtypeStruct((n, n_kv * LANES), BF16)),
        grid=(n // tm,),
        in_specs=[pl.BlockSpec((tm, pg.shape[1]), lambda i: (i, 0)),
                  full(qn), full(kn),
                  pl.BlockSpec((tm, LANES), lambda i: (i % nb, 0)),
                  pl.BlockSpec((tm, LANES), lambda i: (i % nb, 0))],
        out_specs=(pl.BlockSpec((tm, n_q * LANES), lambda i: (i, 0)),
                   pl.BlockSpec((tm, n_kv * LANES), lambda i: (i, 0)),
                   pl.BlockSpec((tm, n_kv * LANES), lambda i: (i, 0))),
        compiler_params=_cparams(("parallel",)),
        name="gqa_prep",
    )(pg, qn, kn, cos, sin)


def _attn_kernel(q_ref, k_ref, v_ref, o_ref, m_s, acc_s, *, n_q, n_kv, tq, tk, nctx_q,
                 nk_ctx, nk_all):
    qi = pl.program_id(1)
    nkb = jnp.where(qi < nctx_q, nk_ctx, nk_all)
    grp = n_q // n_kv
    gs = range(n_kv)
    m_s[...] = jnp.full(m_s.shape, NEG_INF, F32)
    acc_s[...] = jnp.zeros(acc_s.shape, F32)

    def body(kb, carry):
        ks = pl.multiple_of(kb * tk, tk)
        s = []
        for g in gs:
            qg = jnp.concatenate(
                [q_ref[0, :, (g * grp + a) * LANES:(g * grp + a + 1) * LANES] for a in range(grp)],
                axis=0)
            s.append(_dot_nt(qg, k_ref[0, pl.ds(ks, tk), g * LANES:(g + 1) * LANES]))
        m_old = [m_s[g] for g in gs]
        m_new = [jnp.maximum(m_old[g], jnp.max(s[g], axis=1, keepdims=True)) for g in gs]
        p = [jnp.exp(s[g] - m_new[g]) for g in gs]
        a = [jnp.exp(m_old[g] - m_new[g]) for g in gs]
        pv = [_dot(p[g].astype(BF16), v_ref[0, pl.ds(ks, tk), g * LANES:(g + 1) * LANES])
              for g in gs]
        for g in gs:
            m_s[g] = m_new[g]
            acc_s[g] = a[g] * acc_s[g] + pv[g]
        return carry

    lax.fori_loop(0, nkb, body, 0)
    lane = lax.broadcasted_iota(I32, (1, LANES), 1)
    for g in gs:
        acc = acc_s[g]
        o = jnp.where(lane < ONES_LANE, acc / acc[:, ONES_LANE:ONES_LANE + 1], 0.0)
        for a in range(grp):
            h = g * grp + a
            o_ref[0, :, h * LANES:(h + 1) * LANES] = o[a * tq:(a + 1) * tq].astype(o_ref.dtype)


def _attention(q, k, v, ctx_len, *, n_q, n_kv, tq, tk):
    b, l, _ = q.shape
    rows = (n_q // n_kv) * tq
    kern = functools.partial(_attn_kernel, n_q=n_q, n_kv=n_kv, tq=tq, tk=tk,
                             nctx_q=ctx_len // tq, nk_ctx=ctx_len // tk, nk_all=l // tk)
    return pl.pallas_call(
        kern,
        out_shape=jax.ShapeDtypeStruct((b, l, n_q * LANES), BF16),
        grid=(b, l // tq),
        in_specs=[pl.BlockSpec((1, tq, n_q * LANES), lambda bi, i: (bi, i, 0)),
                  pl.BlockSpec((1, l, n_kv * LANES), lambda bi, i: (bi, 0, 0)),
                  pl.BlockSpec((1, l, n_kv * LANES), lambda bi, i: (bi, 0, 0))],
        out_specs=pl.BlockSpec((1, tq, n_q * LANES), lambda bi, i: (bi, i, 0)),
        scratch_shapes=[pltpu.VMEM((n_kv, rows, 1), F32), pltpu.VMEM((n_kv, rows, LANES), F32)],
        compiler_params=_cparams(("parallel", "arbitrary")),
        name="dense_attn",
    )(q, k, v)


def _win_kernel(sink_ref, q_ref, kc_ref, kp_ref, kcur_ref, kn_ref, vc_ref, vp_ref, vcur_ref,
                vn_ref, o_ref, *, n_q, n_kv, nctx_b, nb, ctx_len):
    qi = pl.program_id(1)
    w = WINDOW
    is_lat = (qi >= nctx_b).astype(I32)
    prev_ok = is_lat * (qi - 1 >= nctx_b).astype(I32)
    next_ok = is_lat * (qi + 1 < nb).astype(I32)
    nk = ctx_len + 3 * w
    r = lax.broadcasted_iota(I32, (w, nk), 0)
    c2 = lax.broadcasted_iota(I32, (w, nk), 1)
    c = c2 - ctx_len
    near = jnp.abs(r - (c - w)) <= w
    blk_ok = jnp.where(c < w, prev_ok, jnp.where(c < 2 * w, is_lat, next_ok)) > 0
    valid = jnp.logical_or(c2 < ctx_len, jnp.logical_and(near, blk_ok))
    bias = jnp.where(valid, 0.0, NEG_INF).astype(F32)
    grp = n_q // n_kv
    bias = jnp.concatenate([bias] * grp, axis=0)
    for g in range(n_kv):
        ls = slice(g * LANES, (g + 1) * LANES)
        qg = jnp.concatenate(
            [q_ref[0, :, (g * grp + a) * LANES:(g * grp + a + 1) * LANES] for a in range(grp)],
            axis=0)
        kcat = jnp.concatenate([kc_ref[0, :, ls], kp_ref[0, :, ls], kcur_ref[0, :, ls],
                                kn_ref[0, :, ls]], axis=0)
        vcat = jnp.concatenate([vc_ref[0, :, ls], vp_ref[0, :, ls], vcur_ref[0, :, ls],
                                vn_ref[0, :, ls]], axis=0)
        s = _dot_nt(qg, kcat) + bias
        sk = jnp.concatenate([jnp.full((w, 1), sink_ref[g * grp + a], F32) for a in range(grp)],
                             axis=0)
        m = jnp.maximum(jnp.max(s, axis=1, keepdims=True), sk)
        p = jnp.exp(s - m)
        den = jnp.sum(p, axis=1, keepdims=True) + jnp.exp(sk - m)
        o = _dot(p.astype(BF16), vcat) / den
        for a in range(grp):
            h = g * grp + a
            o_ref[0, :, h * LANES:(h + 1) * LANES] = o[a * w:(a + 1) * w].astype(o_ref.dtype)


def _window_attention(q, k, v, sink, ctx_len, *, n_q, n_kv):
    b, l, _ = q.shape
    w = WINDOW
    nb = l // w
    nctx_b = ctx_len // w
    kw = n_kv * LANES
    kern = functools.partial(_win_kernel, n_q=n_q, n_kv=n_kv, nctx_b=nctx_b, nb=nb,
                             ctx_len=ctx_len)
    ctx_spec = pl.BlockSpec((1, ctx_len, kw), lambda bi, i: (bi, 0, 0))
    prev_spec = pl.BlockSpec((1, w, kw), lambda bi, i: (bi, jnp.maximum(i - 1, 0), 0))
    cur_spec = pl.BlockSpec((1, w, kw), lambda bi, i: (bi, i, 0))
    next_spec = pl.BlockSpec((1, w, kw), lambda bi, i: (bi, jnp.minimum(i + 1, nb - 1), 0))
    return pl.pallas_call(
        kern,
        out_shape=jax.ShapeDtypeStruct((b, l, n_q * LANES), BF16),
        grid=(b, nb),
        in_specs=[pl.BlockSpec(memory_space=pltpu.SMEM),
                  pl.BlockSpec((1, w, n_q * LANES), lambda bi, i: (bi, i, 0)),
                  ctx_spec, prev_spec, cur_spec, next_spec,
                  ctx_spec, prev_spec, cur_spec, next_spec],
        out_specs=pl.BlockSpec((1, w, n_q * LANES), lambda bi, i: (bi, i, 0)),
        compiler_params=_cparams(("parallel", "arbitrary")),
        name="window_attn",
    )(sink, q, k, k, k, k, v, v, v, v)


def _rwkv_feat_kernel(p_ref, hp_ref, hn_ref, mu_ref, w0_ref, w2_ref, a0_ref, a2_ref, g2_ref,
                      kk_ref, ka_ref, rk_ref,
                      r_out, v_out, kk_out, g_out, bonus_out, lw_out, k_out, b_out,
                      *, tm, nb, nctx_b):
    i = pl.program_id(0) % nb
    seq_start = jnp.logical_or(i == 0, i == nctx_b)
    seq_end = jnp.logical_or(i == nctx_b - 1, i == nb - 1)
    p = p_ref[...]
    row = lax.broadcasted_iota(I32, p.shape, 0)
    first = jnp.where(seq_start, 0.0, hp_ref[7:8, :])
    last = jnp.where(seq_end, 0.0, hn_ref[0:1, :])
    prev = jnp.where(row == 0, first, pltpu.roll(p, 1, 0))
    nxt = jnp.where(row == tm - 1, last, pltpu.roll(p, tm - 1, 0))
    mu = mu_ref[...]
    ps = p + mu[0:1, :] * (prev - p) + mu[1:2, :] * (nxt - p)
    hw = RWKV_HEADS * LANES
    r = ps[:, 0:hw]
    k = ps[:, hw:2 * hw]
    v = ps[:, 2 * hw:3 * hw]
    wfb = ps[:, 3 * hw:3 * hw + LANES]
    afb = ps[:, 3 * hw + LANES:3 * hw + 2 * LANES]
    gi = ps[:, 3 * hw + 2 * LANES:]
    kkr = k * kk_ref[...]
    parts = []
    for h in range(RWKV_HEADS):
        x = kkr[:, h * LANES:(h + 1) * LANES]
        nrm = jnp.sqrt(jnp.sum(x * x, axis=-1, keepdims=True))
        parts.append(x / jnp.maximum(nrm, 1e-12))
    kk = jnp.concatenate(parts, axis=1)
    z = w0_ref[...] + _dot(jnp.tanh(wfb).astype(BF16), w2_ref[...])
    lw = -math.exp(-0.5) * jax.nn.sigmoid(z)
    a = jax.nn.sigmoid(a0_ref[...] + _dot(afb.astype(BF16), a2_ref[...]))
    ka = ka_ref[...]
    k0 = k * (1.0 + (a[:, 0:hw] - 1.0) * ka)
    k1 = k * (1.0 + (a[:, hw:] - 1.0) * ka)
    rkk = r * (k0 + k1) * rk_ref[...]
    bparts = []
    for h in range(RWKV_HEADS):
        sl = slice(h * LANES, (h + 1) * LANES)
        bparts.append(jnp.sum(rkk[:, sl], axis=-1, keepdims=True) * v[:, sl])
    r_out[...] = r
    v_out[...] = v
    kk_out[...] = kk
    g_out[...] = _dot(jax.nn.sigmoid(gi).astype(BF16), g2_ref[...])
    bonus_out[...] = jnp.concatenate(bparts, axis=1)
    lw_out[0] = lw[:, 0:hw]
    lw_out[1] = lw[:, hw:]
    k_out[0] = k0
    k_out[1] = k1
    b_out[0] = a[:, 0:hw] * kk
    b_out[1] = a[:, hw:] * kk


def _rwkv_features(pr, mu, w0, w2, a0, a2, g2, kk, ka, rk, nb, nctx_b, tm):
    n, wid = pr.shape
    hw = RWKV_HEADS * LANES
    full = lambda a: pl.BlockSpec(a.shape, lambda i: (0,) * a.ndim)
    kern = functools.partial(_rwkv_feat_kernel, tm=tm, nb=nb, nctx_b=nctx_b)
    one = jax.ShapeDtypeStruct((n, hw), F32)
    two = jax.ShapeDtypeStruct((2, n, hw), F32)
    s1 = pl.BlockSpec((tm, hw), lambda i: (i, 0))
    s2 = pl.BlockSpec((2, tm, hw), lambda i: (0, i, 0))
    t8 = tm // 8
    return pl.pallas_call(
        kern,
        out_shape=(one, one, one, one, one, two, two, two),
        grid=(n // tm,),
        in_specs=[pl.BlockSpec((tm, wid), lambda i: (i, 0)),
                  pl.BlockSpec((8, wid), lambda i: (jnp.maximum(i * t8 - 1, 0), 0)),
                  pl.BlockSpec((8, wid), lambda i: (jnp.minimum((i + 1) * t8, n // 8 - 1), 0)),
                  full(mu), full(w0), full(w2), full(a0), full(a2), full(g2),
                  full(kk), full(ka), full(rk)],
        out_specs=(s1, s1, s1, s1, s1, s2, s2, s2),
        compiler_params=_cparams(("parallel",)),
        name="rwkv_features",
    )(pr, pr, pr, mu, w0, w2, a0, a2, g2, kk, ka, rk)


def _split3_dot(mask_bf16, x):
    x1 = x.astype(BF16)
    r1 = x - x1.astype(F32)
    x2 = r1.astype(BF16)
    x3 = (r1 - x2.astype(F32)).astype(BF16)
    return _dot(mask_bf16, x1) + _dot(mask_bf16, x2) + _dot(mask_bf16, x3)


def _rwkv_scan_kernel(r_ref, v_ref, kk_ref, lw_ref, k_ref, b_ref, o_ref, s_ref):
    d = pl.program_id(1)
    c = pl.program_id(2)
    cs = RWKV_CHUNK

    @pl.when(c == 0)
    def _():
        s_ref[...] = jnp.zeros_like(s_ref)

    rev = d == 1
    t_i = lax.broadcasted_iota(I32, (cs, cs), 0)
    s_i = lax.broadcasted_iota(I32, (cs, cs), 1)
    order = jnp.where(rev, t_i - s_i, s_i - t_i)
    incl = order <= 0
    strict = order < 0
    eye = jnp.where(t_i == s_i, 1.0, 0.0).astype(F32)
    off_masks = []
    for lvl in range(int(math.log2(cs))):
        pair = (t_i >> (lvl + 1)) == (s_i >> (lvl + 1))
        half = (t_i >> lvl) != (s_i >> lvl)
        off = jnp.logical_and(jnp.logical_and(pair, half), strict)
        off_masks.append(jnp.where(off, 1.0, 0.0).astype(F32))
    lw = lw_ref[0]
    cum = _split3_dot(jnp.where(incl, 1.0, 0.0).astype(BF16), lw)
    p_in = jnp.exp(cum)
    p_inv = jnp.exp(-cum)
    p_ex = jnp.exp(cum - lw)
    tot = jnp.where(rev, cum[0:1, :], cum[cs - 1:cs, :])
    p_all = jnp.exp(tot)
    a_t = -kk_ref[...] * p_ex
    r_t = r_ref[...] * p_in
    k_t = k_ref[0] * p_inv
    b_t = b_ref[0] * p_inv
    v = v_ref[...]
    hs = range(RWKV_HEADS)
    sls = [slice(h * LANES, (h + 1) * LANES) for h in hs]
    vb = [v[:, sl].astype(BF16) for sl in sls]
    bk = [jnp.concatenate([b_t[:, sl], k_t[:, sl]], axis=0).astype(BF16) for sl in sls]
    gm = [_dot_nt(jnp.concatenate([a_t[:, sls[h]], r_t[:, sls[h]]], axis=0).astype(BF16), bk[h])
          for h in hs]
    m_ab = [jnp.where(strict, g[0:cs, 0:cs], 0.0) for g in gm]
    mkv = [_dot(jnp.where(strict, gm[h][0:cs, cs:], 0.0).astype(BF16), vb[h]) for h in hs]
    x = [eye + m * off_masks[0] for m in m_ab]
    for lvl in range(1, len(off_masks)):
        xb = [xx.astype(BF16) for xx in x]
        t1 = [_dot(xb[h], (m_ab[h] * off_masks[lvl]).astype(BF16)).astype(BF16) for h in hs]
        x = [x[h] + _dot(t1[h], xb[h]) for h in hs]
    z = [_dot(x[h].astype(BF16),
              jnp.concatenate([a_t[:, sls[h]], mkv[h]], axis=1).astype(BF16)) for h in hs]
    zb = [zz.astype(BF16) for zz in z]
    gy = [_dot(jnp.where(incl, gm[h][cs:, 0:cs], 0.0).astype(BF16), zb[h]) for h in hs]
    y0 = [_dot(jnp.where(incl, gm[h][cs:, cs:], 0.0).astype(BF16), vb[h]) + gy[h][:, LANES:]
          for h in hs]
    s0 = [s_ref[h] for h in hs]
    s0b = [s.astype(BF16) for s in s0]
    u = [_dot_nt(zb[h][:, 0:LANES], s0b[h]) + z[h][:, LANES:] for h in hs]
    for h in hs:
        g_mat = r_t[:, sls[h]] + gy[h][:, 0:LANES]
        o_ref[0, :, sls[h]] = _dot_nt(g_mat.astype(BF16), s0b[h]) + y0[h]
    for h in hs:
        uv = jnp.concatenate([u[h], v[:, sls[h]]], axis=0).astype(BF16)
        s_ref[h] = (s0[h] + _dot_tn(uv, bk[h])) * p_all[:, sls[h]]


def _rwkv_scan(r, v, kk, lw, k, bb, bsz, nctx_c):
    n, hw = r.shape
    cs = RWKV_CHUNK
    nc = n // bsz // cs

    def blk(bi, d, c):
        rc = jnp.where(c < nctx_c, nctx_c - 1 - c, nc - 1 - (c - nctx_c))
        return bi * nc + jnp.where(d == 0, c, rc)

    s1 = pl.BlockSpec((cs, hw), lambda bi, d, c: (blk(bi, d, c), 0))
    s2 = pl.BlockSpec((1, cs, hw), lambda bi, d, c: (d, blk(bi, d, c), 0))
    return pl.pallas_call(
        _rwkv_scan_kernel,
        out_shape=jax.ShapeDtypeStruct((2, n, hw), F32),
        grid=(bsz, 2, nc),
        in_specs=[s1, s1, s1, s2, s2, s2],
        out_specs=s2,
        scratch_shapes=[pltpu.VMEM((RWKV_HEADS, LANES, LANES), F32)],
        compiler_params=_cparams(("parallel", "parallel", "arbitrary")),
        name="rwkv_scan",
    )(r, v, kk, lw, k, bb)


def _rwkv_out_kernel(o_ref, bonus_ref, g_ref, lng_ref, lnb_ref, y_ref):
    o = o_ref[0] + o_ref[1]
    lane = lax.broadcasted_iota(I32, (1, LANES), 1)
    real = lane < RWKV_HEAD
    lng = lng_ref[...]
    lnb = lnb_ref[...]
    parts = []
    for h in range(RWKV_HEADS):
        sl = slice(h * LANES, (h + 1) * LANES)
        x = o[:, sl]
        mu = jnp.sum(x, axis=-1, keepdims=True) * (1.0 / RWKV_HEAD)
        dlt = jnp.where(real, x - mu, 0.0)
        var = jnp.sum(dlt * dlt, axis=-1, keepdims=True) * (1.0 / RWKV_HEAD)
        parts.append(dlt * lax.rsqrt(var + RWKV_GN_EPS) * lng[:, sl] + lnb[:, sl])
    y = (jnp.concatenate(parts, axis=1) + bonus_ref[...]) * g_ref[...]
    y_ref[...] = y.astype(y_ref.dtype)


def _rwkv_out(o, bonus, g, lng, lnb, tm):
    _, n, hw = o.shape
    full = lambda a: pl.BlockSpec(a.shape, lambda i: (0,) * a.ndim)
    s1 = pl.BlockSpec((tm, hw), lambda i: (i, 0))
    return pl.pallas_call(
        _rwkv_out_kernel,
        out_shape=jax.ShapeDtypeStruct((n, hw), BF16),
        grid=(n // tm,),
        in_specs=[pl.BlockSpec((2, tm, hw), lambda i: (0, i, 0)), s1, s1, full(lng), full(lnb)],
        out_specs=s1,
        compiler_params=_cparams(("parallel",)),
        name="rwkv_out",
    )(o, bonus, g, lng, lnb)


def _merge_kernel(ya_ref, yb_ref, yc_ref, yd_ref, gate_ref, x_ref, gt_ref, wb_ref, wo_ref,
                  lng_ref, lnb_ref, o_ref, *, alpha, d):
    ys = (ya_ref, yb_ref, yc_ref, yd_ref)
    acc = None
    for i in range(N_BRANCH):
        gate = jax.nn.sigmoid(gate_ref[:, i * d:(i + 1) * d])
        term = gate * _dot(ys[i][...], wb_ref[i])
        acc = term if acc is None else acc + term
    mix = _dot(acc.astype(BF16), wo_ref[...])
    y = alpha * x_ref[...] + gt_ref[0] * mix
    o_ref[...] = _layer_norm(y, lng_ref[...], lnb_ref[...])


def _merge(ya, yb, yc, yd, gate, x, gt, wb, wo, lng, lnb, nb, nctx_b, tm, alpha):
    n, d = x.shape
    hw = ya.shape[1]
    full = lambda a: pl.BlockSpec(a.shape, lambda i: (0,) * a.ndim)
    sy = pl.BlockSpec((tm, hw), lambda i: (i, 0))
    kern = functools.partial(_merge_kernel, alpha=alpha, d=d)
    return pl.pallas_call(
        kern,
        out_shape=jax.ShapeDtypeStruct((n, d), F32),
        grid=(n // tm,),
        in_specs=[sy, sy, sy, sy,
                  pl.BlockSpec((tm, N_BRANCH * d), lambda i: (i, 0)),
                  pl.BlockSpec((tm, d), lambda i: (i, 0)),
                  pl.BlockSpec((1, 1, d), lambda i: (_group_index(i, nb, nctx_b), 0, 0)),
                  full(wb), full(wo), full(lng), full(lnb)],
        out_specs=pl.BlockSpec((tm, d), lambda i: (i, 0)),
        compiler_params=_cparams(("parallel",)),
        name="merge",
    )(ya, yb, yc, yd, gate, x, gt, wb, wo, lng, lnb)


def _extract_topk(src_ref, n_rows, k, val_ref, pos_ref):
    s = src_ref[...]
    rio = lax.broadcasted_iota(I32, s.shape, 0).astype(F32)
    for rnk in range(k):
        m = jnp.max(s, axis=0, keepdims=True)
        pos = jnp.min(jnp.where(s == m, rio, float(n_rows)), axis=0, keepdims=True)
        val_ref[rnk:rnk + 1, :] = m
        pos_ref[rnk:rnk + 1, :] = pos
        s = jnp.where(rio == pos, -jnp.inf, s)


def _peer_topk_kernel(x_ref, mod_ref, wq_ref, k1_ref, k2_ref, h_ref, idx_ref, wgt_ref,
                      q_s, s_s, v1_s, i1_s, v2_s, i2_s, cand_s, cidx_s, best_s, pos_s):
    hd = pl.program_id(1)

    @pl.when(hd == 0)
    def _():
        m = mod_ref[0]
        hh = x_ref[...] * m[0:1, :] + m[1:2, :]
        h_ref[...] = hh
        q = _dot(hh.astype(BF16), wq_ref[...])
        for a in range(PEER_HEADS):
            q_s[a] = q[:, a * LANES:(a + 1) * LANES].astype(BF16)

    qh = q_s[hd]
    s_s[...] = _dot_nt(k1_ref[0], qh)
    _extract_topk(s_s, PEER_N_KEYS, PEER_TOPK, v1_s, i1_s)
    s_s[...] = _dot_nt(k2_ref[0], qh)
    _extract_topk(s_s, PEER_N_KEYS, PEER_TOPK, v2_s, i2_s)
    v2 = v2_s[...]
    i2 = i2_s[...]
    for a in range(PEER_TOPK):
        cand_s[a * PEER_TOPK:(a + 1) * PEER_TOPK, :] = v1_s[a:a + 1, :] + v2
        cidx_s[a * PEER_TOPK:(a + 1) * PEER_TOPK, :] = i1_s[a:a + 1, :] * float(PEER_N_KEYS) + i2
    _extract_topk(cand_s, PEER_TOPK * PEER_TOPK, PEER_TOPK, best_s, pos_s)
    cidx = cidx_s[...]
    rio = lax.broadcasted_iota(I32, cidx.shape, 0).astype(F32)
    for rnk in range(PEER_TOPK):
        sel = rio == pos_s[rnk:rnk + 1, :]
        e_id = jnp.max(jnp.where(sel, cidx, -1.0), axis=0, keepdims=True)
        idx_ref[0, rnk:rnk + 1, :] = e_id.astype(I32)
    best = best_s[...]
    e = jnp.exp(best - best[0:1, :])
    wgt_ref[0] = e / jnp.sum(e, axis=0, keepdims=True)


def _peer_topk(x, mod, wq, k1, k2, nb, nctx_b, tm):
    n, d = x.shape
    tk = PEER_TOPK
    full = lambda a: pl.BlockSpec(a.shape, lambda i, h: (0,) * a.ndim)
    return pl.pallas_call(
        _peer_topk_kernel,
        out_shape=(jax.ShapeDtypeStruct((n, d), F32),
                   jax.ShapeDtypeStruct((PEER_HEADS, tk, n), I32),
                   jax.ShapeDtypeStruct((PEER_HEADS, tk, n), F32)),
        grid=(n // tm, PEER_HEADS),
        in_specs=[pl.BlockSpec((tm, d), lambda i, h: (i, 0)),
                  pl.BlockSpec((1, 2, d), lambda i, h: (_group_index(i, nb, nctx_b), 0, 0)),
                  full(wq),
                  pl.BlockSpec((1, PEER_N_KEYS, LANES), lambda i, h: (h, 0, 0)),
                  pl.BlockSpec((1, PEER_N_KEYS, LANES), lambda i, h: (h, 0, 0))],
        out_specs=(pl.BlockSpec((tm, d), lambda i, h: (i, 0)),
                   pl.BlockSpec((1, tk, tm), lambda i, h: (h, 0, i)),
                   pl.BlockSpec((1, tk, tm), lambda i, h: (h, 0, i))),
        scratch_shapes=[pltpu.VMEM((PEER_HEADS, tm, LANES), BF16),
                        pltpu.VMEM((PEER_N_KEYS, tm), F32),
                        pltpu.VMEM((tk, tm), F32), pltpu.VMEM((tk, tm), F32),
                        pltpu.VMEM((tk, tm), F32), pltpu.VMEM((tk, tm), F32),
                        pltpu.VMEM((tk * tk, tm), F32), pltpu.VMEM((tk * tk, tm), F32),
                        pltpu.VMEM((tk, tm), F32), pltpu.VMEM((tk, tm), F32)],
        compiler_params=_cparams(("parallel", "arbitrary")),
        name="peer_topk",
    )(x, mod, wq, k1, k2)


PEER_NE = PEER_HEADS * PEER_TOPK
PEER_COLS = PEER_NE * 16
PEER_TOK_UNROLL = 4


def _table_spec(tab):
    return pl.BlockSpec(tab.shape, lambda i: (0, 0), pipeline_mode=pl.Buffered(1))


def _gather_view(off_ref, tab_v, t):
    tiles = []
    for j in range(PEER_NE // 2):
        w = off_ref[t * (PEER_NE // 2) + j]
        o0 = pl.multiple_of(w & 0xFFFF, 8)
        o1 = pl.multiple_of(lax.shift_right_logical(w, 16), 8)
        tiles.append(tab_v[pl.ds(o0, 8), :])
        tiles.append(tab_v[pl.ds(o1, 8), :])
    return pltpu.bitcast(jnp.concatenate(tiles, axis=0), BF16)


def _select_mask(px_row):
    shp = (8, PEER_COLS)
    row = lax.broadcasted_iota(I32, shp, 0)
    col = lax.broadcasted_iota(I32, shp, 1)
    sub = (col >> 1) & 7
    fixed = jnp.logical_and((col & 1) == (row >> 2), (sub & 3) == (row & 3))
    return jnp.logical_and(fixed, (sub >> 2).astype(F32) == px_row)


def _split2(x):
    x1 = x.astype(BF16)
    return x1, (x - x1.astype(F32)).astype(BF16)


def _peer_u_kernel(off_ref, h_ref, par_ref, wgt_ref, e16_ref, g16_ref, tab_v, c_ref,
                   px_s, d_s, *, tb):
    px_s[...] = _dot(par_ref[...].astype(BF16), e16_ref[...])

    def tokens(i, carry):
        for u in range(PEER_TOK_UNROLL):
            t = i * PEER_TOK_UNROLL + u
            view = _gather_view(off_ref, tab_v, t)
            h1, h2 = _split2(h_ref[t])
            dd = _dot_nt(jnp.concatenate([h1, h2], axis=0), view)
            dd = jnp.where(_select_mask(px_s[pl.ds(t, 1), :]), dd[0:8] + dd[8:16], 0.0)
            d_s[pl.ds(t, 1), :] = jnp.sum(dd, axis=0, keepdims=True)
        return carry

    lax.fori_loop(0, tb // PEER_TOK_UNROLL, tokens, 0)
    d1, d2 = _split2(d_s[...])
    g16 = g16_ref[...]
    act = _dot(d1, g16) + _dot(d2, g16)
    gelu = 0.5 * act * (1.0 + lax.erf(act * (2.0 ** -0.5)))
    c_ref[...] = gelu * wgt_ref[...]


def _peer_u(off_flat, h3, par, wgt, e16, g16, tab, tb):
    n = h3.shape[0]
    full = lambda a: pl.BlockSpec(a.shape, lambda i: (0,) * a.ndim)
    kern = functools.partial(_peer_u_kernel, tb=tb)
    tok = pl.BlockSpec((tb, PEER_NE), lambda i: (i, 0))
    return pl.pallas_call(
        kern,
        out_shape=jax.ShapeDtypeStruct((n, PEER_NE), F32),
        grid=(n // tb,),
        in_specs=[pl.BlockSpec((tb * PEER_NE // 2,), lambda i: (i,), memory_space=pltpu.SMEM),
                  pl.BlockSpec((tb, 8, LANES), lambda i: (i, 0, 0)),
                  tok, tok, full(e16), full(g16), _table_spec(tab)],
        out_specs=tok,
        scratch_shapes=[pltpu.VMEM((tb, PEER_COLS), F32), pltpu.VMEM((tb, PEER_COLS), F32)],
        compiler_params=_cparams(("arbitrary",)),
        name="peer_u",
    )(off_flat, h3, par, wgt, e16, g16, tab)


def _peer_v_kernel(off_ref, c_ref, par_ref, e16_ref, tab_v, o_ref, px_s, c1_s, c2_s, *, tb):
    e16 = e16_ref[...]
    px_s[...] = _dot(par_ref[...].astype(BF16), e16)
    c1, c2 = _split2(c_ref[...])
    c1_s[...] = _dot(c1, e16)
    c2_s[...] = _dot(c2, e16)

    def tokens(i, carry):
        for u in range(PEER_TOK_UNROLL):
            t = i * PEER_TOK_UNROLL + u
            view = _gather_view(off_ref, tab_v, t)
            sel = _select_mask(px_s[pl.ds(t, 1), :])
            lhs = jnp.concatenate([jnp.where(sel, c1_s[pl.ds(t, 1), :], 0.0),
                                   jnp.where(sel, c2_s[pl.ds(t, 1), :], 0.0)], axis=0)
            out = _dot(lhs.astype(BF16), view)
            o_ref[t] = out[0:8] + out[8:16]
        return carry

    lax.fori_loop(0, tb // PEER_TOK_UNROLL, tokens, 0)


def _peer_v(off_flat, cw, par, e16, tab, tb):
    n = cw.shape[0]
    full = lambda a: pl.BlockSpec(a.shape, lambda i: (0,) * a.ndim)
    kern = functools.partial(_peer_v_kernel, tb=tb)
    tok = pl.BlockSpec((tb, PEER_NE), lambda i: (i, 0))
    return pl.pallas_call(
        kern,
        out_shape=jax.ShapeDtypeStruct((n, 8, LANES), F32),
        grid=(n // tb,),
        in_specs=[pl.BlockSpec((tb * PEER_NE // 2,), lambda i: (i,), memory_space=pltpu.SMEM),
                  tok, tok, full(e16), _table_spec(tab)],
        out_specs=pl.BlockSpec((tb, 8, LANES), lambda i: (i, 0, 0)),
        scratch_shapes=[pltpu.VMEM((tb, PEER_COLS), F32), pltpu.VMEM((tb, PEER_COLS), F32),
                        pltpu.VMEM((tb, PEER_COLS), F32)],
        compiler_params=_cparams(("arbitrary",)),
        name="peer_v",
    )(off_flat, cw, par, e16, tab)


def _pack_table(tab):
    e, d = tab.shape
    bits = lax.bitcast_convert_type(tab.astype(BF16), jnp.uint16).astype(U32)
    words = bits[:, :d // 2] | (bits[:, d // 2:] << 16)
    return words.reshape(e * 4, LANES)


def _pack_offsets(idx):
    off = (idx >> 1) * 8
    return (off[:, 0::2] | (off[:, 1::2] << 16)).reshape(-1)


def _ln_res_kernel(x_ref, f_ref, gt_ref, lng_ref, lnb_ref, o_ref, *, alpha):
    y = alpha * x_ref[...] + gt_ref[0] * f_ref[...]
    o_ref[...] = _layer_norm(y, lng_ref[...], lnb_ref[...])


def _ln_res(x, f, gt, lng, lnb, nb, nctx_b, tm, alpha):
    n, d = x.shape
    full = lambda a: pl.BlockSpec(a.shape, lambda i: (0,) * a.ndim)
    s1 = pl.BlockSpec((tm, d), lambda i: (i, 0))
    return pl.pallas_call(
        functools.partial(_ln_res_kernel, alpha=alpha),
        out_shape=jax.ShapeDtypeStruct((n, d), F32),
        grid=(n // tm,),
        in_specs=[s1, s1,
                  pl.BlockSpec((1, 1, d), lambda i: (_group_index(i, nb, nctx_b), 0, 0)),
                  full(lng), full(lnb)],
        out_specs=s1,
        compiler_params=_cparams(("parallel",)),
        name="ln_res",
    )(x, f, gt, lng, lnb)


def _pad_heads(w, n_heads, hd):
    lead = w.shape[:-1]
    w = w.reshape(lead + (n_heads, hd))
    w = jnp.pad(w, [(0, 0)] * len(lead) + [(0, 0), (0, LANES - hd)])
    return w.reshape(lead + (n_heads * LANES,))


def _pad_head_rows(w, n_heads, hd):
    d = w.shape[-1]
    w = w.reshape(n_heads, hd, d)
    w = jnp.pad(w, [(0, 0), (0, LANES - hd), (0, 0)])
    return w.reshape(n_heads * LANES, d)


def _rope_tables(rows, rot_dim, lane_off, ctx_len):
    r_idx = jnp.repeat(jnp.arange(rows), GRID_W).astype(F32)
    c_idx = jnp.tile(jnp.arange(GRID_W), rows).astype(F32)
    n = rot_dim // 4
    inv = ROPE_THETA ** (-jnp.arange(n, dtype=F32) / n)
    ang = jnp.concatenate([r_idx[:, None] * inv, c_idx[:, None] * inv], axis=-1)
    cos = jnp.repeat(jnp.cos(ang), 2, axis=-1)
    sin = jnp.repeat(jnp.sin(ang), 2, axis=-1) * jnp.tile(jnp.array([-1.0, 1.0], F32), rot_dim // 2)
    s = ang.shape[0]
    cos_t = jnp.ones((ctx_len + s, LANES), F32).at[ctx_len:, lane_off:lane_off + rot_dim].set(cos)
    sin_t = jnp.zeros((ctx_len + s, LANES), F32).at[ctx_len:, lane_off:lane_off + rot_dim].set(sin)
    return cos_t, sin_t


def _split_cols(w, widths):
    out, start = [], 0
    for wd in widths:
        out.append(w[..., start:start + wd])
        start += wd
    return out


def _forward(x, c, ctx, c_ctx, ada_w, ada_b, w_in, mla_q_norm, mla_kv_norm, mla_w_uq, mla_w_ukv,
             rwkv_mu, rwkv_w0, rwkv_w2, rwkv_a0, rwkv_a2, rwkv_g2, rwkv_k_k, rwkv_k_a, rwkv_r_k,
             rwkv_ln_g, rwkv_ln_b, gqa_q_norm, gqa_k_norm, win_sink, w_branch, w_out, ln1_g, ln1_b,
             peer_wq, peer_keys, peer_u, peer_v, ln2_g, ln2_b):
    stages = []
    bsz, seq, d = x.shape
    ctx_len = ctx.shape[1]
    depth = ada_w.shape[0]
    alpha = (2 * depth) ** 0.25
    l_tot = ctx_len + seq
    n = bsz * l_tot
    tm = 256
    tm_feat = 128
    assert ctx_len % tm == 0 and seq % tm == 0 and seq % GRID_W == 0
    nb = l_tot // tm
    nctx_b = ctx_len // tm
    rows = seq // GRID_W
    ne = PEER_HEADS * PEER_TOPK

    xs = jnp.concatenate([ctx, x], axis=1).reshape(n, d)
    cos_m, sin_m = _rope_tables(rows, MLA_ROPE, MLA_NOPE, ctx_len)
    cos_h, sin_h = _rope_tables(rows, GQA_HEAD, 0, ctx_len)

    m_rows = 16
    cvec = jnp.zeros((m_rows, d), F32).at[:bsz].set(c).at[bsz].set(c_ctx)
    mla_in = MLA_Q_RANK + MLA_KV_RANK + MLA_ROPE
    rwkv_in = 3 * RWKV_W + 2 * RWKV_W_LORA + 2 * RWKV_A_LORA + RWKV_G_LORA
    gqa_in = (GQA_HEADS + 2 * GQA_KV_HEADS) * GQA_HEAD
    win_in = (WIN_HEADS + 2 * WIN_KV_HEADS) * WIN_HEAD
    rw_widths = (RWKV_W, RWKV_W, RWKV_W, RWKV_W_LORA, RWKV_W_LORA, RWKV_A_LORA, RWKV_A_LORA,
                 RWKV_G_LORA)

    col16 = jnp.arange(PEER_COLS)
    e16 = (col16[None, :] // 16 == jnp.arange(ne)[:, None]).astype(BF16)
    g16 = e16.T

    for lyr in range(depth):
        mod = _ada(cvec, ada_w[lyr], ada_b[lyr])
        chunks = [mod[:, i * d:(i + 1) * d] for i in range(6)]

        def table(ch):
            lat = ch[:bsz]
            cx = jnp.broadcast_to(ch[bsz][None], (bsz, d))
            return jnp.stack([cx, lat], axis=1).reshape(bsz * 2, d)

        sh1, sc1, gt1, sh2, sc2, gt2 = [table(ch) for ch in chunks]
        mod1 = jnp.stack([1.0 + sc1, sh1], axis=1)
        mod2 = jnp.stack([1.0 + sc2, sh2], axis=1)
        gt1 = gt1[:, None, :]
        gt2 = gt2[:, None, :]

        wi = w_in[lyr]
        w_mla, w_rw, w_gq, w_wn, w_gate = _split_cols(wi, (mla_in, rwkv_in, gqa_in, win_in, N_BRANCH * d))
        zc = lambda k: jnp.zeros((d, k), F32)
        w_mla_p = jnp.concatenate([w_mla[:, :MLA_Q_RANK + MLA_KV_RANK], zc(MLA_NOPE),
                                   w_mla[:, MLA_Q_RANK + MLA_KV_RANK:], zc(LANES - MLA_NOPE - MLA_ROPE)],
                                  axis=1)
        rr, rk_, rv, rwf, rwb, raf, rab, rgi = _split_cols(w_rw, rw_widths)
        hp = lambda w: _pad_heads(w, RWKV_HEADS, RWKV_HEAD)
        w_rw_p = jnp.concatenate([hp(rr), hp(rk_), hp(rv), rwf, rwb, raf, rab, rgi], axis=1)
        mu_parts = _split_cols(rwkv_mu[lyr], rw_widths)
        mu_p = jnp.concatenate([hp(mu_parts[0]), hp(mu_parts[1]), hp(mu_parts[2])] + mu_parts[3:], axis=1)

        def gqa_cols(w, nq, nkv, hd):
            q_, k_, v_ = _split_cols(w, (nq * hd, nkv * hd, nkv * hd))
            return jnp.concatenate([_pad_heads(q_, nq, hd), _pad_heads(k_, nkv, hd),
                                    _pad_heads(v_, nkv, hd)], axis=1)

        w_gq_p = gqa_cols(w_gq, GQA_HEADS, GQA_KV_HEADS, GQA_HEAD)
        w_wn_p = gqa_cols(w_wn, WIN_HEADS, WIN_KV_HEADS, WIN_HEAD)

        pm = _modmm(xs, mod1, w_mla_p.astype(BF16), nb, nctx_b, tm)
        pr = _modmm(xs, mod1, w_rw_p.astype(BF16), nb, nctx_b, tm)
        pg = _modmm(xs, mod1, w_gq_p.astype(BF16), nb, nctx_b, tm)
        pw = _modmm(xs, mod1, w_wn_p.astype(BF16), nb, nctx_b, tm)
        gate = _modmm(xs, mod1, w_gate.astype(BF16), nb, nctx_b, tm)

        uq = mla_w_uq[lyr].reshape(MLA_Q_RANK, MLA_HEADS, MLA_NOPE + MLA_ROPE)
        uq = jnp.pad(uq, [(0, 0), (0, 0), (0, LANES - MLA_NOPE - MLA_ROPE)]).reshape(MLA_Q_RANK, -1)
        ukv = mla_w_ukv[lyr].reshape(MLA_KV_RANK, MLA_HEADS, MLA_NOPE + MLA_V)
        uk = jnp.pad(ukv[:, :, :MLA_NOPE], [(0, 0), (0, 0), (0, LANES - MLA_NOPE)]).reshape(MLA_KV_RANK, -1)
        uv = jnp.pad(ukv[:, :, MLA_NOPE:], [(0, 0), (0, 0), (0, LANES - MLA_V)]).reshape(MLA_KV_RANK, -1)
        qa, ka, va = _mla_prep(pm, mla_q_norm[lyr][None], mla_kv_norm[lyr][None], uq.astype(BF16),
                               uk.astype(BF16), uv.astype(BF16), cos_m, sin_m, nb, tm)
        r3 = lambda a: a.reshape(bsz, l_tot, a.shape[-1])
        ya = _attention(r3(qa), r3(ka), r3(va), ctx_len, n_q=MLA_HEADS, n_kv=MLA_HEADS, tq=256, tk=256)

        zl = jnp.zeros((RWKV_W_LORA, RWKV_HEADS * LANES), F32)
        w2c = jnp.concatenate([jnp.concatenate([hp(rwkv_w2[lyr, 0]), zl], axis=1),
                               jnp.concatenate([zl, hp(rwkv_w2[lyr, 1])], axis=1)], axis=0)
        a2c = jnp.concatenate([jnp.concatenate([hp(rwkv_a2[lyr, 0]), zl], axis=1),
                               jnp.concatenate([zl, hp(rwkv_a2[lyr, 1])], axis=1)], axis=0)
        w0c = jnp.concatenate([hp(rwkv_w0[lyr, 0]), hp(rwkv_w0[lyr, 1])])[None]
        a0c = jnp.concatenate([hp(rwkv_a0[lyr, 0]), hp(rwkv_a0[lyr, 1])])[None]
        feats = _rwkv_features(pr, mu_p, w0c, w2c.astype(BF16), a0c, a2c.astype(BF16),
                               hp(rwkv_g2[lyr]).astype(BF16), hp(rwkv_k_k[lyr])[None],
                               hp(rwkv_k_a[lyr])[None], hp(rwkv_r_k[lyr].reshape(-1))[None],
                               l_tot // tm_feat, ctx_len // tm_feat, tm_feat)
        f_r, f_v, f_kk, f_g, f_bonus, f_lw, f_k, f_b = feats
        o_scan = _rwkv_scan(f_r, f_v, f_kk, f_lw, f_k, f_b, bsz, ctx_len // RWKV_CHUNK)
        yb = _rwkv_out(o_scan, f_bonus, f_g, hp(rwkv_ln_g[lyr])[None], hp(rwkv_ln_b[lyr])[None], tm)

        pad_g = lambda g: jnp.pad(g, (0, LANES - g.shape[0]))[None]
        qc, kc, vc = _gqa_prep(pg, pad_g(gqa_q_norm[lyr]), pad_g(gqa_k_norm[lyr]), cos_h, sin_h,
                               nb, tm, n_q=GQA_HEADS, n_kv=GQA_KV_HEADS, hd=GQA_HEAD,
                               scale=GQA_SCALE, qk_norm=True)
        yc = _attention(r3(qc), r3(kc), r3(vc), ctx_len, n_q=GQA_HEADS, n_kv=GQA_KV_HEADS, tq=128, tk=256)

        ones_g = jnp.ones((1, LANES), F32)
        qd, kd, vd = _gqa_prep(pw, ones_g, ones_g, cos_h, sin_h, nb, tm, n_q=WIN_HEADS,
                               n_kv=WIN_KV_HEADS, hd=WIN_HEAD, scale=WIN_SCALE, qk_norm=False)
        yd = _window_attention(r3(qd), r3(kd), r3(vd), win_sink[lyr], ctx_len,
                               n_q=WIN_HEADS, n_kv=WIN_KV_HEADS)

        wb = jnp.stack([_pad_head_rows(w_branch[lyr, i], 8, 64) for i in range(N_BRANCH)]).astype(BF16)
        x_mid = _merge(ya.reshape(n, -1), yb, yc.reshape(n, -1), yd.reshape(n, -1), gate, xs, gt1,
                       wb, w_out[lyr].astype(BF16), ln1_g[lyr][None], ln1_b[lyr][None],
                       nb, nctx_b, tm, alpha)

        keys = peer_keys[lyr]
        half = PEER_DQ // 2
        k1 = jnp.pad(keys[:, 0], [(0, 0), (0, 0), (0, LANES - half)]).astype(BF16)
        k2 = jnp.pad(keys[:, 1], [(0, 0), (0, 0), (LANES - half, 0)]).astype(BF16)
        h_in, idx_t, wgt_t = _peer_topk(x_mid, mod2, peer_wq[lyr].astype(BF16), k1, k2, nb, nctx_b, tm)
        idx = idx_t.reshape(ne, n).T
        wgt = wgt_t.reshape(ne, n).T
        off_flat = _pack_offsets(idx)
        par = (idx & 1).astype(F32)
        tb = 64
        cw = _peer_u(off_flat, h_in.reshape(n, 8, LANES), par, wgt, e16, g16,
                     _pack_table(peer_u[lyr]), tb)
        ffn = _peer_v(off_flat, cw, par, e16, _pack_table(peer_v[lyr]), tb).reshape(n, d)
        xs = _ln_res(x_mid, ffn, gt2, ln2_g[lyr][None], ln2_b[lyr][None], nb, nctx_b, tm, alpha)
        stages.append(dict(ya=ya, yb=yb, yc=yc, yd=yd, x_mid=x_mid, idx=idx, wgt=wgt, ffn=ffn,
                           x_out=xs))

    return xs.reshape(bsz, l_tot, d)[:, ctx_len:, :], stages


def kernel(x, c, ctx, c_ctx, ada_w, ada_b, w_in, mla_q_norm, mla_kv_norm, mla_w_uq, mla_w_ukv,
           rwkv_mu, rwkv_w0, rwkv_w2, rwkv_a0, rwkv_a2, rwkv_g2, rwkv_k_k, rwkv_k_a, rwkv_r_k,
           rwkv_ln_g, rwkv_ln_b, gqa_q_norm, gqa_k_norm, win_sink, w_branch, w_out, ln1_g, ln1_b,
           peer_wq, peer_keys, peer_u, peer_v, ln2_g, ln2_b):
    out, _ = _forward(x, c, ctx, c_ctx, ada_w, ada_b, w_in, mla_q_norm, mla_kv_norm, mla_w_uq,
                      mla_w_ukv, rwkv_mu, rwkv_w0, rwkv_w2, rwkv_a0, rwkv_a2, rwkv_g2, rwkv_k_k,
                      rwkv_k_a, rwkv_r_k, rwkv_ln_g, rwkv_ln_b, gqa_q_norm, gqa_k_norm, win_sink,
                      w_branch, w_out, ln1_g, ln1_b, peer_wq, peer_keys, peer_u, peer_v, ln2_g,
                      ln2_b)
    return out
```

```python
import functools
import math

import jax
import jax.numpy as jnp
from jax import lax
from jax.experimental import pallas as pl
from jax.experimental.pallas import tpu as pltpu

F32 = jnp.float32
BF16 = jnp.bfloat16
I32 = jnp.int32
U32 = jnp.uint32

LANES = 128
GRID_W = 64
ROPE_THETA = 10000.0
NEG_INF = -1e30
LN_EPS = 1e-5
RMS_EPS = 1e-6

MLA_HEADS, MLA_Q_RANK, MLA_KV_RANK, MLA_NOPE, MLA_ROPE, MLA_V = 8, 256, 128, 64, 32, 64
MLA_SCALE = (MLA_NOPE + MLA_ROPE) ** -0.5
RWKV_HEADS, RWKV_HEAD = 8, 64
RWKV_W = RWKV_HEADS * RWKV_HEAD
RWKV_W_LORA, RWKV_A_LORA, RWKV_G_LORA = 64, 64, 128
RWKV_GN_EPS = 64e-5
RWKV_CHUNK = 128
GQA_HEADS, GQA_KV_HEADS, GQA_HEAD = 8, 2, 64
GQA_SCALE = GQA_HEAD ** -0.5
WIN_HEADS, WIN_KV_HEADS, WIN_HEAD, WINDOW = 8, 2, 64, 128
WIN_SCALE = WIN_HEAD ** -0.5
PEER_HEADS, PEER_N_KEYS, PEER_TOPK, PEER_DQ = 8, 128, 16, 128
N_BRANCH = 4

VMEM_LIMIT = 56 * 1024 * 1024


def _cparams(sem, vmem=None):
    return pltpu.CompilerParams(dimension_semantics=sem, vmem_limit_bytes=vmem or VMEM_LIMIT)


def _pick_tile(n, cap, mult=LANES):
    best = mult
    for t in range(mult, min(n, cap) + 1, mult):
        if n % t == 0:
            best = t
    return best


def _dot(a, b):
    return jnp.dot(a, b, preferred_element_type=F32)


def _dot_nt(a, b):
    return lax.dot_general(a, b, (((1,), (1,)), ((), ())), preferred_element_type=F32)


def _dot_tn(a, b):
    return lax.dot_general(a, b, (((0,), (0,)), ((), ())), preferred_element_type=F32)


ONES_LANE = 64


def _with_ones_lane(v):
    lane = lax.broadcasted_iota(I32, v.shape, v.ndim - 1)
    return jnp.where((lane & (LANES - 1)) == ONES_LANE, 1.0, v)


def _heads_t(x):
    return jnp.concatenate([x[:, h * LANES:(h + 1) * LANES].T for h in range(x.shape[1] // LANES)],
                           axis=0)


def _layer_norm(y, g, b):
    mu = jnp.mean(y, axis=-1, keepdims=True)
    d = y - mu
    var = jnp.mean(d * d, axis=-1, keepdims=True)
    return d * lax.rsqrt(var + LN_EPS) * g + b


def _swap_pairs(x):
    n = x.shape[-1]
    lane = lax.broadcasted_iota(I32, x.shape, x.ndim - 1)
    nxt = pltpu.roll(x, n - 1, x.ndim - 1)
    prv = pltpu.roll(x, 1, x.ndim - 1)
    return jnp.where((lane & 1) == 0, nxt, prv)


def _rope(x, cos, sin_signed):
    return x * cos + _swap_pairs(x) * sin_signed


def _ada_kernel(c_ref, w_ref, b_ref, o_ref):
    c = c_ref[...]
    s = c * jax.nn.sigmoid(c)
    o_ref[...] = _dot(s.astype(BF16), w_ref[...].astype(BF16)) + b_ref[...]


def _ada(cvec, w, b):
    m, k = cvec.shape
    nc = w.shape[1]
    tn = _pick_tile(nc, 1536)
    return pl.pallas_call(
        _ada_kernel,
        out_shape=jax.ShapeDtypeStruct((m, nc), F32),
        grid=(nc // tn,),
        in_specs=[pl.BlockSpec((m, k), lambda j: (0, 0)),
                  pl.BlockSpec((k, tn), lambda j: (0, j)),
                  pl.BlockSpec((1, tn), lambda j: (0, j))],
        out_specs=pl.BlockSpec((m, tn), lambda j: (0, j)),
        compiler_params=_cparams(("parallel",)),
        name="ada_mod",
    )(cvec, w, b.reshape(1, nc))


def _mm_kernel(x_ref, mod_ref, w_ref, o_ref):
    m = mod_ref[0]
    xm = x_ref[...] * m[0:1, :] + m[1:2, :]
    o_ref[...] = _dot(xm.astype(BF16), w_ref[...]).astype(o_ref.dtype)


def _group_index(i, nb, nctx_b):
    return (i // nb) * 2 + (i % nb >= nctx_b).astype(I32)


def _modmm(x, mod, w, nb, nctx_b, tm):
    n, k = x.shape
    nc = w.shape[1]
    tn = _pick_tile(nc, 2048)
    return pl.pallas_call(
        _mm_kernel,
        out_shape=jax.ShapeDtypeStruct((n, nc), F32),
        grid=(nc // tn, n // tm),
        in_specs=[pl.BlockSpec((tm, k), lambda j, i: (i, 0)),
                  pl.BlockSpec((1, 2, k), lambda j, i: (_group_index(i, nb, nctx_b), 0, 0)),
                  pl.BlockSpec((k, tn), lambda j, i: (0, j))],
        out_specs=pl.BlockSpec((tm, tn), lambda j, i: (i, j)),
        compiler_params=_cparams(("parallel", "parallel")),
        name="in_proj",
    )(x, mod, w)


def _mla_prep_kernel(p_ref, qn_ref, kvn_ref, wq_ref, wk_ref, wv_ref, cos_ref, sin_ref,
                     qt_ref, k_ref, vt_ref):
    p = p_ref[...]
    dq = p[:, 0:MLA_Q_RANK]
    dkv = p[:, MLA_Q_RANK:MLA_Q_RANK + MLA_KV_RANK]
    krp = p[:, MLA_Q_RANK + MLA_KV_RANK:]
    qn = dq * lax.rsqrt(jnp.mean(dq * dq, axis=-1, keepdims=True) + RMS_EPS) * qn_ref[...]
    kvn = dkv * lax.rsqrt(jnp.mean(dkv * dkv, axis=-1, keepdims=True) + RMS_EPS) * kvn_ref[...]
    kvn = kvn.astype(BF16)
    cos = cos_ref[...]
    sin = sin_ref[...]
    cos_h = jnp.concatenate([cos] * MLA_HEADS, axis=1)
    sin_h = jnp.concatenate([sin] * MLA_HEADS, axis=1)
    q = _rope(_dot(qn.astype(BF16), wq_ref[...]), cos_h, sin_h) * MLA_SCALE
    kr = _rope(krp, cos, sin)
    k = _dot(kvn, wk_ref[...]) + jnp.concatenate([kr] * MLA_HEADS, axis=1)
    qt_ref[0] = _heads_t(q).astype(BF16)
    k_ref[...] = k.astype(BF16)
    vt_ref[0, 0] = _heads_t(_with_ones_lane(_dot(kvn, wv_ref[...]))).astype(BF16)


def _mla_prep(pm, qn, kvn, wq, wk, wv, cos, sin, nb, tm):
    n = pm.shape[0]
    hw = MLA_HEADS * LANES
    bsz = n // (nb * tm)
    full = lambda a: pl.BlockSpec(a.shape, lambda i: (0,) * a.ndim)
    return pl.pallas_call(
        _mla_prep_kernel,
        out_shape=(jax.ShapeDtypeStruct((bsz, hw, nb * tm), BF16),
                   jax.ShapeDtypeStruct((n, hw), BF16),
                   jax.ShapeDtypeStruct((bsz, nb, hw, tm), BF16)),
        grid=(n // tm,),
        in_specs=[pl.BlockSpec((tm, pm.shape[1]), lambda i: (i, 0)),
                  full(qn), full(kvn), full(wq), full(wk), full(wv),
                  pl.BlockSpec((tm, LANES), lambda i: (i % nb, 0)),
                  pl.BlockSpec((tm, LANES), lambda i: (i % nb, 0))],
        out_specs=(pl.BlockSpec((1, hw, tm), lambda i: (i // nb, 0, i % nb)),
                   pl.BlockSpec((tm, hw), lambda i: (i, 0)),
                   pl.BlockSpec((1, 1, hw, tm), lambda i: (i // nb, i % nb, 0, 0))),
        compiler_params=_cparams(("parallel",)),
        name="mla_prep",
    )(pm, qn, kvn, wq, wk, wv, cos, sin)


def _gqa_prep_kernel(p_ref, qn_ref, kn_ref, cos_ref, sin_ref, q_ref, k_ref, v_ref,
                     *, n_q, n_kv, hd, scale, qk_norm, transposed):
    p = p_ref[...]
    cos = cos_ref[...]
    sin = sin_ref[...]

    def head(j, gain):
        x = p[:, j * LANES:(j + 1) * LANES]
        if qk_norm:
            ms = jnp.sum(x * x, axis=-1, keepdims=True) * (1.0 / hd)
            x = x * lax.rsqrt(ms + RMS_EPS) * gain
        return _rope(x, cos, sin)

    qg = qn_ref[...]
    kg = kn_ref[...]
    q = jnp.concatenate([head(j, qg) for j in range(n_q)], axis=1) * scale
    k = jnp.concatenate([head(n_q + j, kg) for j in range(n_kv)], axis=1)
    v = _with_ones_lane(p[:, (n_q + n_kv) * LANES:])
    k_ref[...] = k.astype(BF16)
    if transposed:
        q_ref[0] = _heads_t(q).astype(BF16)
        v_ref[0, 0] = _heads_t(v).astype(BF16)
    else:
        q_ref[...] = q.astype(BF16)
        v_ref[...] = v.astype(BF16)


def _gqa_prep(pg, qn, kn, cos, sin, nb, tm, *, n_q, n_kv, hd, scale, qk_norm, transposed):
    n = pg.shape[0]
    bsz = n // (nb * tm)
    full = lambda a: pl.BlockSpec(a.shape, lambda i: (0,) * a.ndim)
    kern = functools.partial(_gqa_prep_kernel, n_q=n_q, n_kv=n_kv, hd=hd, scale=scale,
                             qk_norm=qk_norm, transposed=transposed)
    if transposed:
        q_shape = jax.ShapeDtypeStruct((bsz, n_q * LANES, nb * tm), BF16)
        v_shape = jax.ShapeDtypeStruct((bsz, nb, n_kv * LANES, tm), BF16)
        q_spec = pl.BlockSpec((1, n_q * LANES, tm), lambda i: (i // nb, 0, i % nb))
        v_spec = pl.BlockSpec((1, 1, n_kv * LANES, tm), lambda i: (i // nb, i % nb, 0, 0))
    else:
        q_shape = jax.ShapeDtypeStruct((n, n_q * LANES), BF16)
        v_shape = jax.ShapeDtypeStruct((n, n_kv * LANES), BF16)
        q_spec = pl.BlockSpec((tm, n_q * LANES), lambda i: (i, 0))
        v_spec = pl.BlockSpec((tm, n_kv * LANES), lambda i: (i, 0))
    return pl.pallas_call(
        kern,
        out_shape=(q_shape, jax.ShapeDtypeStruct((n, n_kv * LANES), BF16), v_shape),
        grid=(n // tm,),
        in_specs=[pl.BlockSpec((tm, pg.shape[1]), lambda i: (i, 0)),
                  full(qn), full(kn),
                  pl.BlockSpec((tm, LANES), lambda i: (i % nb, 0)),
                  pl.BlockSpec((tm, LANES), lambda i: (i % nb, 0))],
        out_specs=(q_spec, pl.BlockSpec((tm, n_kv * LANES), lambda i: (i, 0)), v_spec),
        compiler_params=_cparams(("parallel",)),
        name="gqa_prep",
    )(pg, qn, kn, cos, sin)


def _attn_kernel(qt_ref, k_ref, vt_ref, o_ref, m_s, acc_s, *, n_q, n_kv, tq, tk, nctx_q,
                 nk_ctx, nk_all):
    qi = pl.program_id(1)
    nkb = jnp.where(qi < nctx_q, nk_ctx, nk_all)
    grp = n_q // n_kv
    gs = range(n_kv)
    m_s[...] = jnp.full(m_s.shape, NEG_INF, F32)
    acc_s[...] = jnp.zeros(acc_s.shape, F32)

    def body(kb, carry):
        ks = pl.multiple_of(kb * tk, tk)
        s = []
        for g in gs:
            qtg = jnp.concatenate(
                [qt_ref[0, (g * grp + a) * LANES:(g * grp + a + 1) * LANES, :] for a in range(grp)],
                axis=1)
            s.append(_dot(k_ref[0, pl.ds(ks, tk), g * LANES:(g + 1) * LANES], qtg))
        m_old = [m_s[g] for g in gs]
        m_new = [jnp.maximum(m_old[g], jnp.max(s[g], axis=0, keepdims=True)) for g in gs]
        p = [jnp.exp(s[g] - m_new[g]).astype(BF16) for g in gs]
        a = [jnp.exp(m_old[g] - m_new[g]) for g in gs]
        pv = [_dot(vt_ref[0, kb, g * LANES:(g + 1) * LANES, :], p[g]) for g in gs]
        for g in gs:
            m_s[g] = m_new[g]
            acc_s[g] = a[g] * acc_s[g] + pv[g]
        return carry

    lax.fori_loop(0, nkb, body, 0)
    row = lax.broadcasted_iota(I32, (LANES, 1), 0)
    for g in gs:
        acc = acc_s[g]
        ot = jnp.where(row < ONES_LANE, acc / acc[ONES_LANE:ONES_LANE + 1, :], 0.0)
        for a in range(grp):
            h = g * grp + a
            o_ref[0, :, h * LANES:(h + 1) * LANES] = ot[:, a * tq:(a + 1) * tq].T.astype(o_ref.dtype)


def _attention(qt, k, vt, ctx_len, *, n_q, n_kv, tq):
    b, l, _ = k.shape
    tk = vt.shape[-1]
    rows = (n_q // n_kv) * tq
    kern = functools.partial(_attn_kernel, n_q=n_q, n_kv=n_kv, tq=tq, tk=tk,
                             nctx_q=ctx_len // tq, nk_ctx=ctx_len // tk, nk_all=l // tk)
    return pl.pallas_call(
        kern,
        out_shape=jax.ShapeDtypeStruct((b, l, n_q * LANES), BF16),
        grid=(b, l // tq),
        in_specs=[pl.BlockSpec((1, n_q * LANES, tq), lambda bi, i: (bi, 0, i)),
                  pl.BlockSpec((1, l, n_kv * LANES), lambda bi, i: (bi, 0, 0)),
                  pl.BlockSpec((1, l // tk, n_kv * LANES, tk), lambda bi, i: (bi, 0, 0, 0))],
        out_specs=pl.BlockSpec((1, tq, n_q * LANES), lambda bi, i: (bi, i, 0)),
        scratch_shapes=[pltpu.VMEM((n_kv, 1, rows), F32), pltpu.VMEM((n_kv, LANES, rows), F32)],
        compiler_params=_cparams(("parallel", "arbitrary")),
        name="dense_attn",
    )(qt, k, vt)


def _win_kernel(sink_ref, q_ref, kc_ref, kp_ref, kcur_ref, kn_ref, vc_ref, vp_ref, vcur_ref,
                vn_ref, o_ref, *, n_q, n_kv, nctx_b, nb, ctx_len):
    qi = pl.program_id(1)
    w = WINDOW
    is_lat = (qi >= nctx_b).astype(I32)
    prev_ok = is_lat * (qi - 1 >= nctx_b).astype(I32)
    next_ok = is_lat * (qi + 1 < nb).astype(I32)
    nk = ctx_len + 3 * w
    r = lax.broadcasted_iota(I32, (w, nk), 0)
    c2 = lax.broadcasted_iota(I32, (w, nk), 1)
    c = c2 - ctx_len
    near = jnp.abs(r - (c - w)) <= w
    blk_ok = jnp.where(c < w, prev_ok, jnp.where(c < 2 * w, is_lat, next_ok)) > 0
    valid = jnp.logical_or(c2 < ctx_len, jnp.logical_and(near, blk_ok))
    bias = jnp.where(valid, 0.0, NEG_INF).astype(F32)
    grp = n_q // n_kv
    bias = jnp.concatenate([bias] * grp, axis=0)
    for g in range(n_kv):
        ls = slice(g * LANES, (g + 1) * LANES)
        qg = jnp.concatenate(
            [q_ref[0, :, (g * grp + a) * LANES:(g * grp + a + 1) * LANES] for a in range(grp)],
            axis=0)
        kcat = jnp.concatenate([kc_ref[0, :, ls], kp_ref[0, :, ls], kcur_ref[0, :, ls],
                                kn_ref[0, :, ls]], axis=0)
        vcat = jnp.concatenate([vc_ref[0, :, ls], vp_ref[0, :, ls], vcur_ref[0, :, ls],
                                vn_ref[0, :, ls]], axis=0)
        s = _dot_nt(qg, kcat) + bias
        sk = jnp.concatenate([jnp.full((w, 1), sink_ref[g * grp + a], F32) for a in range(grp)],
                             axis=0)
        m = jnp.maximum(jnp.max(s, axis=1, keepdims=True), sk)
        p = jnp.exp(s - m)
        den = jnp.sum(p, axis=1, keepdims=True) + jnp.exp(sk - m)
        o = _dot(p.astype(BF16), vcat) / den
        for a in range(grp):
            h = g * grp + a
            o_ref[0, :, h * LANES:(h + 1) * LANES] = o[a * w:(a + 1) * w].astype(o_ref.dtype)


def _window_attention(q, k, v, sink, ctx_len, *, n_q, n_kv):
    b, l, _ = q.shape
    w = WINDOW
    nb = l // w
    nctx_b = ctx_len // w
    kw = n_kv * LANES
    kern = functools.partial(_win_kernel, n_q=n_q, n_kv=n_kv, nctx_b=nctx_b, nb=nb,
                             ctx_len=ctx_len)
    ctx_spec = pl.BlockSpec((1, ctx_len, kw), lambda bi, i: (bi, 0, 0))
    prev_spec = pl.BlockSpec((1, w, kw), lambda bi, i: (bi, jnp.maximum(i - 1, 0), 0))
    cur_spec = pl.BlockSpec((1, w, kw), lambda bi, i: (bi, i, 0))
    next_spec = pl.BlockSpec((1, w, kw), lambda bi, i: (bi, jnp.minimum(i + 1, nb - 1), 0))
    return pl.pallas_call(
        kern,
        out_shape=jax.ShapeDtypeStruct((b, l, n_q * LANES), BF16),
        grid=(b, nb),
        in_specs=[pl.BlockSpec(memory_space=pltpu.SMEM),
                  pl.BlockSpec((1, w, n_q * LANES), lambda bi, i: (bi, i, 0)),
                  ctx_spec, prev_spec, cur_spec, next_spec,
                  ctx_spec, prev_spec, cur_spec, next_spec],
        out_specs=pl.BlockSpec((1, w, n_q * LANES), lambda bi, i: (bi, i, 0)),
        compiler_params=_cparams(("parallel", "arbitrary")),
        name="window_attn",
    )(sink, q, k, k, k, k, v, v, v, v)


def _rwkv_feat_kernel(p_ref, hp_ref, hn_ref, mu_ref, w0_ref, w2_ref, a0_ref, a2_ref, g2_ref,
                      kk_ref, ka_ref, rk_ref,
                      r_out, v_out, kk_out, g_out, bonus_out, lw_out, k_out, b_out,
                      *, tm, nb, nctx_b):
    i = pl.program_id(0) % nb
    seq_start = jnp.logical_or(i == 0, i == nctx_b)
    seq_end = jnp.logical_or(i == nctx_b - 1, i == nb - 1)
    p = p_ref[...]
    row = lax.broadcasted_iota(I32, p.shape, 0)
    first = jnp.where(seq_start, 0.0, hp_ref[7:8, :])
    last = jnp.where(seq_end, 0.0, hn_ref[0:1, :])
    prev = jnp.where(row == 0, first, pltpu.roll(p, 1, 0))
    nxt = jnp.where(row == tm - 1, last, pltpu.roll(p, tm - 1, 0))
    mu = mu_ref[...]
    ps = p + mu[0:1, :] * (prev - p) + mu[1:2, :] * (nxt - p)
    hw = RWKV_HEADS * LANES
    r = ps[:, 0:hw]
    k = ps[:, hw:2 * hw]
    v = ps[:, 2 * hw:3 * hw]
    wfb = ps[:, 3 * hw:3 * hw + LANES]
    afb = ps[:, 3 * hw + LANES:3 * hw + 2 * LANES]
    gi = ps[:, 3 * hw + 2 * LANES:]
    kkr = k * kk_ref[...]
    parts = []
    for h in range(RWKV_HEADS):
        x = kkr[:, h * LANES:(h + 1) * LANES]
        nrm = jnp.sqrt(jnp.sum(x * x, axis=-1, keepdims=True))
        parts.append(x / jnp.maximum(nrm, 1e-12))
    kk = jnp.concatenate(parts, axis=1)
    z = w0_ref[...] + _dot(jnp.tanh(wfb).astype(BF16), w2_ref[...])
    lw = -math.exp(-0.5) * jax.nn.sigmoid(z)
    a = jax.nn.sigmoid(a0_ref[...] + _dot(afb.astype(BF16), a2_ref[...]))
    ka = ka_ref[...]
    k0 = k * (1.0 + (a[:, 0:hw] - 1.0) * ka)
    k1 = k * (1.0 + (a[:, hw:] - 1.0) * ka)
    rkk = r * (k0 + k1) * rk_ref[...]
    bparts = []
    for h in range(RWKV_HEADS):
        sl = slice(h * LANES, (h + 1) * LANES)
        bparts.append(jnp.sum(rkk[:, sl], axis=-1, keepdims=True) * v[:, sl])
    r_out[...] = r
    v_out[...] = v
    kk_out[...] = kk
    g_out[...] = _dot(jax.nn.sigmoid(gi).astype(BF16), g2_ref[...])
    bonus_out[...] = jnp.concatenate(bparts, axis=1)
    lw_out[0] = lw[:, 0:hw]
    lw_out[1] = lw[:, hw:]
    k_out[0] = k0
    k_out[1] = k1
    b_out[0] = a[:, 0:hw] * kk
    b_out[1] = a[:, hw:] * kk


def _rwkv_features(pr, mu, w0, w2, a0, a2, g2, kk, ka, rk, nb, nctx_b, tm):
    n, wid = pr.shape
    hw = RWKV_HEADS * LANES
    full = lambda a: pl.BlockSpec(a.shape, lambda i: (0,) * a.ndim)
    kern = functools.partial(_rwkv_feat_kernel, tm=tm, nb=nb, nctx_b=nctx_b)
    one = jax.ShapeDtypeStruct((n, hw), F32)
    two = jax.ShapeDtypeStruct((2, n, hw), F32)
    s1 = pl.BlockSpec((tm, hw), lambda i: (i, 0))
    s2 = pl.BlockSpec((2, tm, hw), lambda i: (0, i, 0))
    t8 = tm // 8
    return pl.pallas_call(
        kern,
        out_shape=(one, one, one, one, one, two, two, two),
        grid=(n // tm,),
        in_specs=[pl.BlockSpec((tm, wid), lambda i: (i, 0)),
                  pl.BlockSpec((8, wid), lambda i: (jnp.maximum(i * t8 - 1, 0), 0)),
                  pl.BlockSpec((8, wid), lambda i: (jnp.minimum((i + 1) * t8, n // 8 - 1), 0)),
                  full(mu), full(w0), full(w2), full(a0), full(a2), full(g2),
                  full(kk), full(ka), full(rk)],
        out_specs=(s1, s1, s1, s1, s1, s2, s2, s2),
        compiler_params=_cparams(("parallel",)),
        name="rwkv_features",
    )(pr, pr, pr, mu, w0, w2, a0, a2, g2, kk, ka, rk)


def _split3_dot(mask_bf16, x):
    x1 = x.astype(BF16)
    r1 = x - x1.astype(F32)
    x2 = r1.astype(BF16)
    x3 = (r1 - x2.astype(F32)).astype(BF16)
    return _dot(mask_bf16, x1) + _dot(mask_bf16, x2) + _dot(mask_bf16, x3)


def _rwkv_scan_kernel(r_ref, v_ref, kk_ref, lw_ref, k_ref, b_ref, o_ref, s_ref):
    d = pl.program_id(1)
    c = pl.program_id(2)
    cs = RWKV_CHUNK

    @pl.when(c == 0)
    def _():
        s_ref[...] = jnp.zeros_like(s_ref)

    rev = d == 1
    t_i = lax.broadcasted_iota(I32, (cs, cs), 0)
    s_i = lax.broadcasted_iota(I32, (cs, cs), 1)
    order = jnp.where(rev, t_i - s_i, s_i - t_i)
    incl = order <= 0
    strict = order < 0
    eye = jnp.where(t_i == s_i, 1.0, 0.0).astype(F32)
    off_masks = []
    for lvl in range(int(math.log2(cs))):
        pair = (t_i >> (lvl + 1)) == (s_i >> (lvl + 1))
        half = (t_i >> lvl) != (s_i >> lvl)
        off = jnp.logical_and(jnp.logical_and(pair, half), strict)
        off_masks.append(jnp.where(off, 1.0, 0.0).astype(F32))
    lw = lw_ref[0]
    cum = _split3_dot(jnp.where(incl, 1.0, 0.0).astype(BF16), lw)
    p_in = jnp.exp(cum)
    p_inv = jnp.exp(-cum)
    p_ex = jnp.exp(cum - lw)
    tot = jnp.where(rev, cum[0:1, :], cum[cs - 1:cs, :])
    p_all = jnp.exp(tot)
    a_t = -kk_ref[...] * p_ex
    r_t = r_ref[...] * p_in
    k_t = k_ref[0] * p_inv
    b_t = b_ref[0] * p_inv
    v = v_ref[...]
    hs = range(RWKV_HEADS)
    sls = [slice(h * LANES, (h + 1) * LANES) for h in hs]
    vb = [v[:, sl].astype(BF16) for sl in sls]
    bk = [jnp.concatenate([b_t[:, sl], k_t[:, sl]], axis=0).astype(BF16) for sl in sls]
    gm = [_dot_nt(jnp.concatenate([a_t[:, sls[h]], r_t[:, sls[h]]], axis=0).astype(BF16), bk[h])
          for h in hs]
    m_ab = [jnp.where(strict, g[0:cs, 0:cs], 0.0) for g in gm]
    mkv = [_dot(jnp.where(strict, gm[h][0:cs, cs:], 0.0).astype(BF16), vb[h]) for h in hs]
    x = [eye + m * off_masks[0] for m in m_ab]
    for lvl in range(1, len(off_masks)):
        xb = [xx.astype(BF16) for xx in x]
        t1 = [_dot(xb[h], (m_ab[h] * off_masks[lvl]).astype(BF16)).astype(BF16) for h in hs]
        x = [x[h] + _dot(t1[h], xb[h]) for h in hs]
    z = [_dot(x[h].astype(BF16),
              jnp.concatenate([a_t[:, sls[h]], mkv[h]], axis=1).astype(BF16)) for h in hs]
    zb = [zz.astype(BF16) for zz in z]
    gy = [_dot(jnp.where(incl, gm[h][cs:, 0:cs], 0.0).astype(BF16), zb[h]) for h in hs]
    y0 = [_dot(jnp.where(incl, gm[h][cs:, cs:], 0.0).astype(BF16), vb[h]) + gy[h][:, LANES:]
          for h in hs]
    s0 = [s_ref[h] for h in hs]
    s0b = [s.astype(BF16) for s in s0]
    u = [_dot_nt(zb[h][:, 0:LANES], s0b[h]) + z[h][:, LANES:] for h in hs]
    for h in hs:
        g_mat = r_t[:, sls[h]] + gy[h][:, 0:LANES]
        o_ref[0, :, sls[h]] = _dot_nt(g_mat.astype(BF16), s0b[h]) + y0[h]
    for h in hs:
        uv = jnp.concatenate([u[h], v[:, sls[h]]], axis=0).astype(BF16)
        s_ref[h] = (s0[h] + _dot_tn(uv, bk[h])) * p_all[:, sls[h]]


def _rwkv_scan(r, v, kk, lw, k, bb, bsz, nctx_c):
    n, hw = r.shape
    cs = RWKV_CHUNK
    nc = n // bsz // cs

    def blk(bi, d, c):
        rc = jnp.where(c < nctx_c, nctx_c - 1 - c, nc - 1 - (c - nctx_c))
        return bi * nc + jnp.where(d == 0, c, rc)

    s1 = pl.BlockSpec((cs, hw), lambda bi, d, c: (blk(bi, d, c), 0))
    s2 = pl.BlockSpec((1, cs, hw), lambda bi, d, c: (d, blk(bi, d, c), 0))
    return pl.pallas_call(
        _rwkv_scan_kernel,
        out_shape=jax.ShapeDtypeStruct((2, n, hw), F32),
        grid=(bsz, 2, nc),
        in_specs=[s1, s1, s1, s2, s2, s2],
        out_specs=s2,
        scratch_shapes=[pltpu.VMEM((RWKV_HEADS, LANES, LANES), F32)],
        compiler_params=_cparams(("parallel", "parallel", "arbitrary")),
        name="rwkv_scan",
    )(r, v, kk, lw, k, bb)


def _rwkv_out_kernel(o_ref, bonus_ref, g_ref, lng_ref, lnb_ref, y_ref):
    o = o_ref[0] + o_ref[1]
    lane = lax.broadcasted_iota(I32, (1, LANES), 1)
    real = lane < RWKV_HEAD
    lng = lng_ref[...]
    lnb = lnb_ref[...]
    parts = []
    for h in range(RWKV_HEADS):
        sl = slice(h * LANES, (h + 1) * LANES)
        x = o[:, sl]
        mu = jnp.sum(x, axis=-1, keepdims=True) * (1.0 / RWKV_HEAD)
        dlt = jnp.where(real, x - mu, 0.0)
        var = jnp.sum(dlt * dlt, axis=-1, keepdims=True) * (1.0 / RWKV_HEAD)
        parts.append(dlt * lax.rsqrt(var + RWKV_GN_EPS) * lng[:, sl] + lnb[:, sl])
    y = (jnp.concatenate(parts, axis=1) + bonus_ref[...]) * g_ref[...]
    y_ref[...] = y.astype(y_ref.dtype)


def _rwkv_out(o, bonus, g, lng, lnb, tm):
    _, n, hw = o.shape
    full = lambda a: pl.BlockSpec(a.shape, lambda i: (0,) * a.ndim)
    s1 = pl.BlockSpec((tm, hw), lambda i: (i, 0))
    return pl.pallas_call(
        _rwkv_out_kernel,
        out_shape=jax.ShapeDtypeStruct((n, hw), BF16),
        grid=(n // tm,),
        in_specs=[pl.BlockSpec((2, tm, hw), lambda i: (0, i, 0)), s1, s1, full(lng), full(lnb)],
        out_specs=s1,
        compiler_params=_cparams(("parallel",)),
        name="rwkv_out",
    )(o, bonus, g, lng, lnb)


def _merge_kernel(ya_ref, yb_ref, yc_ref, yd_ref, gate_ref, x_ref, gt_ref, wb_ref, wo_ref,
                  lng_ref, lnb_ref, o_ref, *, alpha, d):
    ys = (ya_ref, yb_ref, yc_ref, yd_ref)
    acc = None
    for i in range(N_BRANCH):
        gate = jax.nn.sigmoid(gate_ref[:, i * d:(i + 1) * d])
        term = gate * _dot(ys[i][...], wb_ref[i])
        acc = term if acc is None else acc + term
    mix = _dot(acc.astype(BF16), wo_ref[...])
    y = alpha * x_ref[...] + gt_ref[0] * mix
    o_ref[...] = _layer_norm(y, lng_ref[...], lnb_ref[...])


def _merge(ya, yb, yc, yd, gate, x, gt, wb, wo, lng, lnb, nb, nctx_b, tm, alpha):
    n, d = x.shape
    hw = ya.shape[1]
    full = lambda a: pl.BlockSpec(a.shape, lambda i: (0,) * a.ndim)
    sy = pl.BlockSpec((tm, hw), lambda i: (i, 0))
    kern = functools.partial(_merge_kernel, alpha=alpha, d=d)
    return pl.pallas_call(
        kern,
        out_shape=jax.ShapeDtypeStruct((n, d), F32),
        grid=(n // tm,),
        in_specs=[sy, sy, sy, sy,
                  pl.BlockSpec((tm, N_BRANCH * d), lambda i: (i, 0)),
                  pl.BlockSpec((tm, d), lambda i: (i, 0)),
                  pl.BlockSpec((1, 1, d), lambda i: (_group_index(i, nb, nctx_b), 0, 0)),
                  full(wb), full(wo), full(lng), full(lnb)],
        out_specs=pl.BlockSpec((tm, d), lambda i: (i, 0)),
        compiler_params=_cparams(("parallel",)),
        name="merge",
    )(ya, yb, yc, yd, gate, x, gt, wb, wo, lng, lnb)


def _extract_topk(srcs, n_rows, k, val_refs, pos_refs):
    rio = lax.broadcasted_iota(I32, srcs[0].shape, 0).astype(F32)
    js = range(len(srcs))
    for rnk in range(k):
        m = [jnp.max(s, axis=0, keepdims=True) for s in srcs]
        pos = [jnp.min(jnp.where(srcs[j] == m[j], rio, float(n_rows)), axis=0, keepdims=True)
               for j in js]
        for j in js:
            val_refs[j][rnk:rnk + 1, :] = m[j]
            pos_refs[j][rnk:rnk + 1, :] = pos[j]
        srcs = [jnp.where(rio == pos[j], -jnp.inf, srcs[j]) for j in js]


PEER_CAND_COUNTS = tuple(PEER_TOPK // (a + 1) for a in range(PEER_TOPK))
PEER_N_CAND = sum(PEER_CAND_COUNTS)
PEER_CAND_ROWS = -(-PEER_N_CAND // 8) * 8


def _peer_topk_kernel(x_ref, mod_ref, wq_ref, k1_ref, k2_ref, h_ref, idx_ref, wgt_ref, off_ref,
                      q_s, v1_s, i1_s, v2_s, i2_s, cand_s, cidx_s, best_s, pos_s):
    hd = pl.program_id(1)

    @pl.when(hd == 0)
    def _():
        m = mod_ref[0]
        hh = x_ref[...] * m[0:1, :] + m[1:2, :]
        h_ref[...] = hh
        q = _dot(hh.astype(BF16), wq_ref[...])
        for a in range(PEER_HEADS):
            q_s[a] = q[:, a * LANES:(a + 1) * LANES].astype(BF16)

    qh = q_s[hd]
    _extract_topk([_dot_nt(k1_ref[0], qh), _dot_nt(k2_ref[0], qh)],
                  PEER_N_KEYS, PEER_TOPK, [v1_s, v2_s], [i1_s, i2_s])
    row = 0
    for a, cnt in enumerate(PEER_CAND_COUNTS):
        cand_s[row:row + cnt, :] = v1_s[a:a + 1, :] + v2_s[0:cnt, :]
        cidx_s[row:row + cnt, :] = i1_s[a:a + 1, :] * float(PEER_N_KEYS) + i2_s[0:cnt, :]
        row += cnt
    pad = PEER_CAND_ROWS - PEER_N_CAND
    if pad:
        cand_s[PEER_N_CAND:, :] = jnp.full((pad, cand_s.shape[1]), -jnp.inf, F32)
        cidx_s[PEER_N_CAND:, :] = jnp.zeros((pad, cand_s.shape[1]), F32)
    _extract_topk([cand_s[...]], PEER_CAND_ROWS, PEER_TOPK, [best_s], [pos_s])
    cidx = cidx_s[...]
    rio = lax.broadcasted_iota(I32, cidx.shape, 0).astype(F32)
    lo = None
    for rnk in range(PEER_TOPK):
        sel = rio == pos_s[rnk:rnk + 1, :]
        e_id = jnp.max(jnp.where(sel, cidx, -1.0), axis=0, keepdims=True).astype(I32)
        idx_ref[0, rnk:rnk + 1, :] = e_id
        off = (e_id >> 1) * 8
        if rnk % 2 == 0:
            lo = off
        else:
            off_ref[0, rnk // 2:rnk // 2 + 1, :] = lo | (off << 16)
    best = best_s[...]
    e = jnp.exp(best - best[0:1, :])
    wgt_ref[0] = e / jnp.sum(e, axis=0, keepdims=True)


def _peer_topk(x, mod, wq, k1, k2, nb, nctx_b, tm):
    n, d = x.shape
    tk = PEER_TOPK
    full = lambda a: pl.BlockSpec(a.shape, lambda i, h: (0,) * a.ndim)
    return pl.pallas_call(
        _peer_topk_kernel,
        out_shape=(jax.ShapeDtypeStruct((n, d), F32),
                   jax.ShapeDtypeStruct((PEER_HEADS, tk, n), I32),
                   jax.ShapeDtypeStruct((PEER_HEADS, tk, n), F32),
                   jax.ShapeDtypeStruct((PEER_HEADS, tk // 2, n), I32)),
        grid=(n // tm, PEER_HEADS),
        in_specs=[pl.BlockSpec((tm, d), lambda i, h: (i, 0)),
                  pl.BlockSpec((1, 2, d), lambda i, h: (_group_index(i, nb, nctx_b), 0, 0)),
                  full(wq),
                  pl.BlockSpec((1, PEER_N_KEYS, LANES), lambda i, h: (h, 0, 0)),
                  pl.BlockSpec((1, PEER_N_KEYS, LANES), lambda i, h: (h, 0, 0))],
        out_specs=(pl.BlockSpec((tm, d), lambda i, h: (i, 0)),
                   pl.BlockSpec((1, tk, tm), lambda i, h: (h, 0, i)),
                   pl.BlockSpec((1, tk, tm), lambda i, h: (h, 0, i)),
                   pl.BlockSpec((1, tk // 2, tm), lambda i, h: (h, 0, i))),
        scratch_shapes=[pltpu.VMEM((PEER_HEADS, tm, LANES), BF16),
                        pltpu.VMEM((tk, tm), F32), pltpu.VMEM((tk, tm), F32),
                        pltpu.VMEM((tk, tm), F32), pltpu.VMEM((tk, tm), F32),
                        pltpu.VMEM((PEER_CAND_ROWS, tm), F32), pltpu.VMEM((PEER_CAND_ROWS, tm), F32),
                        pltpu.VMEM((tk, tm), F32), pltpu.VMEM((tk, tm), F32)],
        compiler_params=_cparams(("parallel", "arbitrary")),
        name="peer_topk",
    )(x, mod, wq, k1, k2)


PEER_NE = PEER_HEADS * PEER_TOPK
PEER_COLS = PEER_NE * 16
PEER_TOK_UNROLL = 4


def _table_spec(tab):
    return pl.BlockSpec(tab.shape, lambda i: (0, 0), pipeline_mode=pl.Buffered(1))


def _gather_view(off_ref, tab_v, t):
    tiles = []
    for j in range(PEER_NE // 2):
        w = off_ref[t * (PEER_NE // 2) + j]
        o0 = pl.multiple_of(w & 0xFFFF, 8)
        o1 = pl.multiple_of(lax.shift_right_logical(w, 16), 8)
        tiles.append(tab_v[pl.ds(o0, 8), :])
        tiles.append(tab_v[pl.ds(o1, 8), :])
    return pltpu.bitcast(jnp.concatenate(tiles, axis=0), BF16)


def _select_mask(px_row):
    shp = (8, PEER_COLS)
    row = lax.broadcasted_iota(I32, shp, 0)
    col = lax.broadcasted_iota(I32, shp, 1)
    sub = (col >> 1) & 7
    fixed = jnp.logical_and((col & 1) == (row >> 2), (sub & 3) == (row & 3))
    return jnp.logical_and(fixed, (sub >> 2).astype(F32) == px_row)


def _split2(x):
    x1 = x.astype(BF16)
    return x1, (x - x1.astype(F32)).astype(BF16)


def _peer_u_kernel(off_ref, h_ref, par_ref, wgt_ref, e16_ref, g16_ref, tab_v, c_ref,
                   px_s, d_s, *, tb):
    px_s[...] = _dot(par_ref[...].astype(BF16), e16_ref[...])

    def tokens(i, carry):
        for u in range(PEER_TOK_UNROLL):
            t = i * PEER_TOK_UNROLL + u
            view = _gather_view(off_ref, tab_v, t)
            h1, h2 = _split2(h_ref[t])
            dd = _dot_nt(jnp.concatenate([h1, h2], axis=0), view)
            dd = jnp.where(_select_mask(px_s[pl.ds(t, 1), :]), dd[0:8] + dd[8:16], 0.0)
            d_s[pl.ds(t, 1), :] = jnp.sum(dd, axis=0, keepdims=True)
        return carry

    lax.fori_loop(0, tb // PEER_TOK_UNROLL, tokens, 0)
    d1, d2 = _split2(d_s[...])
    g16 = g16_ref[...]
    act = _dot(d1, g16) + _dot(d2, g16)
    gelu = 0.5 * act * (1.0 + lax.erf(act * (2.0 ** -0.5)))
    c_ref[...] = gelu * wgt_ref[...]


def _peer_u(off_flat, h3, par, wgt, e16, g16, tab, tb):
    n = h3.shape[0]
    full = lambda a: pl.BlockSpec(a.shape, lambda i: (0,) * a.ndim)
    kern = functools.partial(_peer_u_kernel, tb=tb)
    tok = pl.BlockSpec((tb, PEER_NE), lambda i: (i, 0))
    return pl.pallas_call(
        kern,
        out_shape=jax.ShapeDtypeStruct((n, PEER_NE), F32),
        grid=(n // tb,),
        in_specs=[pl.BlockSpec((tb * PEER_NE // 2,), lambda i: (i,), memory_space=pltpu.SMEM),
                  pl.BlockSpec((tb, 8, LANES), lambda i: (i, 0, 0)),
                  tok, tok, full(e16), full(g16), _table_spec(tab)],
        out_specs=tok,
        scratch_shapes=[pltpu.VMEM((tb, PEER_COLS), F32), pltpu.VMEM((tb, PEER_COLS), F32)],
        compiler_params=_cparams(("arbitrary",)),
        name="peer_u",
    )(off_flat, h3, par, wgt, e16, g16, tab)


def _peer_v_kernel(off_ref, c_ref, par_ref, e16_ref, tab_v, o_ref, px_s, c1_s, c2_s, *, tb):
    e16 = e16_ref[...]
    px_s[...] = _dot(par_ref[...].astype(BF16), e16)
    c1, c2 = _split2(c_ref[...])
    c1_s[...] = _dot(c1, e16)
    c2_s[...] = _dot(c2, e16)

    def tokens(i, carry):
        for u in range(PEER_TOK_UNROLL):
            t = i * PEER_TOK_UNROLL + u
            view = _gather_view(off_ref, tab_v, t)
            sel = _select_mask(px_s[pl.ds(t, 1), :])
            lhs = jnp.concatenate([jnp.where(sel, c1_s[pl.ds(t, 1), :], 0.0),
                                   jnp.where(sel, c2_s[pl.ds(t, 1), :], 0.0)], axis=0)
            out = _dot(lhs.astype(BF16), view)
            o_ref[t] = out[0:8] + out[8:16]
        return carry

    lax.fori_loop(0, tb // PEER_TOK_UNROLL, tokens, 0)


def _peer_v(off_flat, cw, par, e16, tab, tb):
    n = cw.shape[0]
    full = lambda a: pl.BlockSpec(a.shape, lambda i: (0,) * a.ndim)
    kern = functools.partial(_peer_v_kernel, tb=tb)
    tok = pl.BlockSpec((tb, PEER_NE), lambda i: (i, 0))
    return pl.pallas_call(
        kern,
        out_shape=jax.ShapeDtypeStruct((n, 8, LANES), F32),
        grid=(n // tb,),
        in_specs=[pl.BlockSpec((tb * PEER_NE // 2,), lambda i: (i,), memory_space=pltpu.SMEM),
                  tok, tok, full(e16), _table_spec(tab)],
        out_specs=pl.BlockSpec((tb, 8, LANES), lambda i: (i, 0, 0)),
        scratch_shapes=[pltpu.VMEM((tb, PEER_COLS), F32), pltpu.VMEM((tb, PEER_COLS), F32),
                        pltpu.VMEM((tb, PEER_COLS), F32)],
        compiler_params=_cparams(("arbitrary",)),
        name="peer_v",
    )(off_flat, cw, par, e16, tab)


def _pack_table(tab):
    e, d = tab.shape
    bits = lax.bitcast_convert_type(tab.astype(BF16), jnp.uint16).astype(U32)
    words = bits[:, :d // 2] | (bits[:, d // 2:] << 16)
    return words.reshape(e * 4, LANES)


def _ln_res_kernel(x_ref, f_ref, gt_ref, lng_ref, lnb_ref, o_ref, *, alpha):
    y = alpha * x_ref[...] + gt_ref[0] * f_ref[...]
    o_ref[...] = _layer_norm(y, lng_ref[...], lnb_ref[...])


def _ln_res(x, f, gt, lng, lnb, nb, nctx_b, tm, alpha):
    n, d = x.shape
    full = lambda a: pl.BlockSpec(a.shape, lambda i: (0,) * a.ndim)
    s1 = pl.BlockSpec((tm, d), lambda i: (i, 0))
    return pl.pallas_call(
        functools.partial(_ln_res_kernel, alpha=alpha),
        out_shape=jax.ShapeDtypeStruct((n, d), F32),
        grid=(n // tm,),
        in_specs=[s1, s1,
                  pl.BlockSpec((1, 1, d), lambda i: (_group_index(i, nb, nctx_b), 0, 0)),
                  full(lng), full(lnb)],
        out_specs=s1,
        compiler_params=_cparams(("parallel",)),
        name="ln_res",
    )(x, f, gt, lng, lnb)


def _pad_heads(w, n_heads, hd):
    lead = w.shape[:-1]
    w = w.reshape(lead + (n_heads, hd))
    w = jnp.pad(w, [(0, 0)] * len(lead) + [(0, 0), (0, LANES - hd)])
    return w.reshape(lead + (n_heads * LANES,))


def _pad_head_rows(w, n_heads, hd):
    d = w.shape[-1]
    w = w.reshape(n_heads, hd, d)
    w = jnp.pad(w, [(0, 0), (0, LANES - hd), (0, 0)])
    return w.reshape(n_heads * LANES, d)


def _rope_tables(rows, rot_dim, lane_off, ctx_len):
    r_idx = jnp.repeat(jnp.arange(rows), GRID_W).astype(F32)
    c_idx = jnp.tile(jnp.arange(GRID_W), rows).astype(F32)
    n = rot_dim // 4
    inv = ROPE_THETA ** (-jnp.arange(n, dtype=F32) / n)
    ang = jnp.concatenate([r_idx[:, None] * inv, c_idx[:, None] * inv], axis=-1)
    cos = jnp.repeat(jnp.cos(ang), 2, axis=-1)
    sin = jnp.repeat(jnp.sin(ang), 2, axis=-1) * jnp.tile(jnp.array([-1.0, 1.0], F32), rot_dim // 2)
    s = ang.shape[0]
    cos_t = jnp.ones((ctx_len + s, LANES), F32).at[ctx_len:, lane_off:lane_off + rot_dim].set(cos)
    sin_t = jnp.zeros((ctx_len + s, LANES), F32).at[ctx_len:, lane_off:lane_off + rot_dim].set(sin)
    return cos_t, sin_t


def _split_cols(w, widths):
    out, start = [], 0
    for wd in widths:
        out.append(w[..., start:start + wd])
        start += wd
    return out


def _forward(x, c, ctx, c_ctx, ada_w, ada_b, w_in, mla_q_norm, mla_kv_norm, mla_w_uq, mla_w_ukv,
             rwkv_mu, rwkv_w0, rwkv_w2, rwkv_a0, rwkv_a2, rwkv_g2, rwkv_k_k, rwkv_k_a, rwkv_r_k,
             rwkv_ln_g, rwkv_ln_b, gqa_q_norm, gqa_k_norm, win_sink, w_branch, w_out, ln1_g, ln1_b,
             peer_wq, peer_keys, peer_u, peer_v, ln2_g, ln2_b):
    stages = []
    bsz, seq, d = x.shape
    ctx_len = ctx.shape[1]
    depth = ada_w.shape[0]
    alpha = (2 * depth) ** 0.25
    l_tot = ctx_len + seq
    n = bsz * l_tot
    tm = 256
    tm_feat = 128
    assert ctx_len % tm == 0 and seq % tm == 0 and seq % GRID_W == 0
    nb = l_tot // tm
    nctx_b = ctx_len // tm
    rows = seq // GRID_W
    ne = PEER_HEADS * PEER_TOPK

    xs = jnp.concatenate([ctx, x], axis=1).reshape(n, d)
    cos_m, sin_m = _rope_tables(rows, MLA_ROPE, MLA_NOPE, ctx_len)
    cos_h, sin_h = _rope_tables(rows, GQA_HEAD, 0, ctx_len)

    m_rows = 16
    cvec = jnp.zeros((m_rows, d), F32).at[:bsz].set(c).at[bsz].set(c_ctx)
    mla_in = MLA_Q_RANK + MLA_KV_RANK + MLA_ROPE
    rwkv_in = 3 * RWKV_W + 2 * RWKV_W_LORA + 2 * RWKV_A_LORA + RWKV_G_LORA
    gqa_in = (GQA_HEADS + 2 * GQA_KV_HEADS) * GQA_HEAD
    win_in = (WIN_HEADS + 2 * WIN_KV_HEADS) * WIN_HEAD
    rw_widths = (RWKV_W, RWKV_W, RWKV_W, RWKV_W_LORA, RWKV_W_LORA, RWKV_A_LORA, RWKV_A_LORA,
                 RWKV_G_LORA)

    col16 = jnp.arange(PEER_COLS)
    e16 = (col16[None, :] // 16 == jnp.arange(ne)[:, None]).astype(BF16)
    g16 = e16.T

    for lyr in range(depth):
        mod = _ada(cvec, ada_w[lyr], ada_b[lyr])
        chunks = [mod[:, i * d:(i + 1) * d] for i in range(6)]

        def table(ch):
            lat = ch[:bsz]
            cx = jnp.broadcast_to(ch[bsz][None], (bsz, d))
            return jnp.stack([cx, lat], axis=1).reshape(bsz * 2, d)

        sh1, sc1, gt1, sh2, sc2, gt2 = [table(ch) for ch in chunks]
        mod1 = jnp.stack([1.0 + sc1, sh1], axis=1)
        mod2 = jnp.stack([1.0 + sc2, sh2], axis=1)
        gt1 = gt1[:, None, :]
        gt2 = gt2[:, None, :]

        wi = w_in[lyr]
        w_mla, w_rw, w_gq, w_wn, w_gate = _split_cols(wi, (mla_in, rwkv_in, gqa_in, win_in, N_BRANCH * d))
        zc = lambda k: jnp.zeros((d, k), F32)
        w_mla_p = jnp.concatenate([w_mla[:, :MLA_Q_RANK + MLA_KV_RANK], zc(MLA_NOPE),
                                   w_mla[:, MLA_Q_RANK + MLA_KV_RANK:], zc(LANES - MLA_NOPE - MLA_ROPE)],
                                  axis=1)
        rr, rk_, rv, rwf, rwb, raf, rab, rgi = _split_cols(w_rw, rw_widths)
        hp = lambda w: _pad_heads(w, RWKV_HEADS, RWKV_HEAD)
        w_rw_p = jnp.concatenate([hp(rr), hp(rk_), hp(rv), rwf, rwb, raf, rab, rgi], axis=1)
        mu_parts = _split_cols(rwkv_mu[lyr], rw_widths)
        mu_p = jnp.concatenate([hp(mu_parts[0]), hp(mu_parts[1]), hp(mu_parts[2])] + mu_parts[3:], axis=1)

        def gqa_cols(w, nq, nkv, hd):
            q_, k_, v_ = _split_cols(w, (nq * hd, nkv * hd, nkv * hd))
            return jnp.concatenate([_pad_heads(q_, nq, hd), _pad_heads(k_, nkv, hd),
                                    _pad_heads(v_, nkv, hd)], axis=1)

        w_gq_p = gqa_cols(w_gq, GQA_HEADS, GQA_KV_HEADS, GQA_HEAD)
        w_wn_p = gqa_cols(w_wn, WIN_HEADS, WIN_KV_HEADS, WIN_HEAD)

        pm = _modmm(xs, mod1, w_mla_p.astype(BF16), nb, nctx_b, tm)
        pr = _modmm(xs, mod1, w_rw_p.astype(BF16), nb, nctx_b, tm)
        pg = _modmm(xs, mod1, w_gq_p.astype(BF16), nb, nctx_b, tm)
        pw = _modmm(xs, mod1, w_wn_p.astype(BF16), nb, nctx_b, tm)
        gate = _modmm(xs, mod1, w_gate.astype(BF16), nb, nctx_b, tm)

        uq = mla_w_uq[lyr].reshape(MLA_Q_RANK, MLA_HEADS, MLA_NOPE + MLA_ROPE)
        uq = jnp.pad(uq, [(0, 0), (0, 0), (0, LANES - MLA_NOPE - MLA_ROPE)]).reshape(MLA_Q_RANK, -1)
        ukv = mla_w_ukv[lyr].reshape(MLA_KV_RANK, MLA_HEADS, MLA_NOPE + MLA_V)
        uk = jnp.pad(ukv[:, :, :MLA_NOPE], [(0, 0), (0, 0), (0, LANES - MLA_NOPE)]).reshape(MLA_KV_RANK, -1)
        uv = jnp.pad(ukv[:, :, MLA_NOPE:], [(0, 0), (0, 0), (0, LANES - MLA_V)]).reshape(MLA_KV_RANK, -1)
        qa, ka, va = _mla_prep(pm, mla_q_norm[lyr][None], mla_kv_norm[lyr][None], uq.astype(BF16),
                               uk.astype(BF16), uv.astype(BF16), cos_m, sin_m, nb, tm)
        r3 = lambda a: a.reshape(bsz, l_tot, a.shape[-1])
        ya = _attention(qa, r3(ka), va, ctx_len, n_q=MLA_HEADS, n_kv=MLA_HEADS, tq=256)

        zl = jnp.zeros((RWKV_W_LORA, RWKV_HEADS * LANES), F32)
        w2c = jnp.concatenate([jnp.concatenate([hp(rwkv_w2[lyr, 0]), zl], axis=1),
                               jnp.concatenate([zl, hp(rwkv_w2[lyr, 1])], axis=1)], axis=0)
        a2c = jnp.concatenate([jnp.concatenate([hp(rwkv_a2[lyr, 0]), zl], axis=1),
                               jnp.concatenate([zl, hp(rwkv_a2[lyr, 1])], axis=1)], axis=0)
        w0c = jnp.concatenate([hp(rwkv_w0[lyr, 0]), hp(rwkv_w0[lyr, 1])])[None]
        a0c = jnp.concatenate([hp(rwkv_a0[lyr, 0]), hp(rwkv_a0[lyr, 1])])[None]
        feats = _rwkv_features(pr, mu_p, w0c, w2c.astype(BF16), a0c, a2c.astype(BF16),
                               hp(rwkv_g2[lyr]).astype(BF16), hp(rwkv_k_k[lyr])[None],
                               hp(rwkv_k_a[lyr])[None], hp(rwkv_r_k[lyr].reshape(-1))[None],
                               l_tot // tm_feat, ctx_len // tm_feat, tm_feat)
        f_r, f_v, f_kk, f_g, f_bonus, f_lw, f_k, f_b = feats
        o_scan = _rwkv_scan(f_r, f_v, f_kk, f_lw, f_k, f_b, bsz, ctx_len // RWKV_CHUNK)
        yb = _rwkv_out(o_scan, f_bonus, f_g, hp(rwkv_ln_g[lyr])[None], hp(rwkv_ln_b[lyr])[None], tm)

        pad_g = lambda g: jnp.pad(g, (0, LANES - g.shape[0]))[None]
        qc, kc, vc = _gqa_prep(pg, pad_g(gqa_q_norm[lyr]), pad_g(gqa_k_norm[lyr]), cos_h, sin_h,
                               nb, tm, n_q=GQA_HEADS, n_kv=GQA_KV_HEADS, hd=GQA_HEAD,
                               scale=GQA_SCALE, qk_norm=True, transposed=True)
        yc = _attention(qc, r3(kc), vc, ctx_len, n_q=GQA_HEADS, n_kv=GQA_KV_HEADS, tq=128)

        ones_g = jnp.ones((1, LANES), F32)
        qd, kd, vd = _gqa_prep(pw, ones_g, ones_g, cos_h, sin_h, nb, tm, n_q=WIN_HEADS,
                               n_kv=WIN_KV_HEADS, hd=WIN_HEAD, scale=WIN_SCALE, qk_norm=False,
                               transposed=False)
        yd = _window_attention(r3(qd), r3(kd), r3(vd), win_sink[lyr], ctx_len,
                               n_q=WIN_HEADS, n_kv=WIN_KV_HEADS)

        wb = jnp.stack([_pad_head_rows(w_branch[lyr, i], 8, 64) for i in range(N_BRANCH)]).astype(BF16)
        x_mid = _merge(ya.reshape(n, -1), yb, yc.reshape(n, -1), yd.reshape(n, -1), gate, xs, gt1,
                       wb, w_out[lyr].astype(BF16), ln1_g[lyr][None], ln1_b[lyr][None],
                       nb, nctx_b, tm, alpha)

        keys = peer_keys[lyr]
        half = PEER_DQ // 2
        k1 = jnp.pad(keys[:, 0], [(0, 0), (0, 0), (0, LANES - half)]).astype(BF16)
        k2 = jnp.pad(keys[:, 1], [(0, 0), (0, 0), (LANES - half, 0)]).astype(BF16)
        h_in, idx_t, wgt_t, off_t = _peer_topk(x_mid, mod2, peer_wq[lyr].astype(BF16), k1, k2, nb, nctx_b, tm)
        idx = idx_t.reshape(ne, n).T
        wgt = wgt_t.reshape(ne, n).T
        off_flat = off_t.reshape(ne // 2, n).T.reshape(-1)
        par = (idx & 1).astype(F32)
        tb = 64
        cw = _peer_u(off_flat, h_in.reshape(n, 8, LANES), par, wgt, e16, g16,
                     _pack_table(peer_u[lyr]), tb)
        ffn = _peer_v(off_flat, cw, par, e16, _pack_table(peer_v[lyr]), tb).reshape(n, d)
        xs = _ln_res(x_mid, ffn, gt2, ln2_g[lyr][None], ln2_b[lyr][None], nb, nctx_b, tm, alpha)
        stages.append(dict(ya=ya, yb=yb, yc=yc, yd=yd, x_mid=x_mid, idx=idx, wgt=wgt, ffn=ffn,
                           x_out=xs))

    return xs.reshape(bsz, l_tot, d)[:, ctx_len:, :], stages


def kernel(x, c, ctx, c_ctx, ada_w, ada_b, w_in, mla_q_norm, mla_kv_norm, mla_w_uq, mla_w_ukv,
           rwkv_mu, rwkv_w0, rwkv_w2, rwkv_a0, rwkv_a2, rwkv_g2, rwkv_k_k, rwkv_k_a, rwkv_r_k,
           rwkv_ln_g, rwkv_ln_b, gqa_q_norm, gqa_k_norm, win_sink, w_branch, w_out, ln1_g, ln1_b,
           peer_wq, peer_keys, peer_u, peer_v, ln2_g, ln2_b):
    out, _ = _forward(x, c, ctx, c_ctx, ada_w, ada_b, w_in, mla_q_norm, mla_kv_norm, mla_w_uq,
                      mla_w_ukv, rwkv_mu, rwkv_w0, rwkv_w2, rwkv_a0, rwkv_a2, rwkv_g2, rwkv_k_k,
                      rwkv_k_a, rwkv_r_k, rwkv_ln_g, rwkv_ln_b, gqa_q_norm, gqa_k_norm, win_sink,
                      w_branch, w_out, ln1_g, ln1_b, peer_wq, peer_keys, peer_u, peer_v, ln2_g,
                      ln2_b)
    return out
```

```python
import functools
import math

import jax
import jax.numpy as jnp
from jax import lax
from jax.experimental import pallas as pl
from jax.experimental.pallas import tpu as pltpu

F32 = jnp.float32
BF16 = jnp.bfloat16
I32 = jnp.int32
U32 = jnp.uint32

LANES = 128
GRID_W = 64
ROPE_THETA = 10000.0
NEG_INF = -1e30
LN_EPS = 1e-5
RMS_EPS = 1e-6

MLA_HEADS, MLA_Q_RANK, MLA_KV_RANK, MLA_NOPE, MLA_ROPE, MLA_V = 8, 256, 128, 64, 32, 64
MLA_SCALE = (MLA_NOPE + MLA_ROPE) ** -0.5
RWKV_HEADS, RWKV_HEAD = 8, 64
RWKV_W = RWKV_HEADS * RWKV_HEAD
RWKV_W_LORA, RWKV_A_LORA, RWKV_G_LORA = 64, 64, 128
RWKV_GN_EPS = 64e-5
RWKV_CHUNK = 128
GQA_HEADS, GQA_KV_HEADS, GQA_HEAD = 8, 2, 64
GQA_SCALE = GQA_HEAD ** -0.5
WIN_HEADS, WIN_KV_HEADS, WIN_HEAD, WINDOW = 8, 2, 64, 128
WIN_SCALE = WIN_HEAD ** -0.5
PEER_HEADS, PEER_N_KEYS, PEER_TOPK, PEER_DQ = 8, 128, 16, 128
N_BRANCH = 4

VMEM_LIMIT = 56 * 1024 * 1024


def _cparams(sem, vmem=None):
    return pltpu.CompilerParams(dimension_semantics=sem, vmem_limit_bytes=vmem or VMEM_LIMIT)


def _pick_tile(n, cap, mult=LANES):
    best = mult
    for t in range(mult, min(n, cap) + 1, mult):
        if n % t == 0:
            best = t
    return best


def _dot(a, b):
    return jnp.dot(a, b, preferred_element_type=F32)


def _dot_nt(a, b):
    return lax.dot_general(a, b, (((1,), (1,)), ((), ())), preferred_element_type=F32)


def _dot_tn(a, b):
    return lax.dot_general(a, b, (((0,), (0,)), ((), ())), preferred_element_type=F32)


ONES_LANE = 64


def _with_ones_lane(v):
    lane = lax.broadcasted_iota(I32, v.shape, v.ndim - 1)
    return jnp.where((lane & (LANES - 1)) == ONES_LANE, 1.0, v)


def _heads_t(x):
    return jnp.concatenate([x[:, h * LANES:(h + 1) * LANES].T for h in range(x.shape[1] // LANES)],
                           axis=0)


def _layer_norm(y, g, b):
    mu = jnp.mean(y, axis=-1, keepdims=True)
    d = y - mu
    var = jnp.mean(d * d, axis=-1, keepdims=True)
    return d * lax.rsqrt(var + LN_EPS) * g + b


def _swap_pairs(x):
    n = x.shape[-1]
    lane = lax.broadcasted_iota(I32, x.shape, x.ndim - 1)
    nxt = pltpu.roll(x, n - 1, x.ndim - 1)
    prv = pltpu.roll(x, 1, x.ndim - 1)
    return jnp.where((lane & 1) == 0, nxt, prv)


def _rope(x, cos, sin_signed):
    return x * cos + _swap_pairs(x) * sin_signed


def _ada_kernel(c_ref, w_ref, b_ref, o_ref):
    c = c_ref[...]
    s = c * jax.nn.sigmoid(c)
    o_ref[...] = _dot(s.astype(BF16), w_ref[...].astype(BF16)) + b_ref[...]


def _ada(cvec, w, b):
    m, k = cvec.shape
    nc = w.shape[1]
    tn = _pick_tile(nc, 1536)
    return pl.pallas_call(
        _ada_kernel,
        out_shape=jax.ShapeDtypeStruct((m, nc), F32),
        grid=(nc // tn,),
        in_specs=[pl.BlockSpec((m, k), lambda j: (0, 0)),
                  pl.BlockSpec((k, tn), lambda j: (0, j)),
                  pl.BlockSpec((1, tn), lambda j: (0, j))],
        out_specs=pl.BlockSpec((m, tn), lambda j: (0, j)),
        compiler_params=_cparams(("parallel",)),
        name="ada_mod",
    )(cvec, w, b.reshape(1, nc))


def _group_index(i, nb, nctx_b):
    return (i // nb) * 2 + (i % nb >= nctx_b).astype(I32)


def _mm_kernel(x_ref, mod_ref, w_ref, o_ref, *, sub, tm, nb, nctx_b):
    i = pl.program_id(1)
    parts = []
    for u in range(sub):
        m = mod_ref[_group_index(i * sub + u, nb, nctx_b)]
        xm = x_ref[u * tm:(u + 1) * tm, :] * m[0:1, :] + m[1:2, :]
        parts.append(xm.astype(BF16))
    o_ref[...] = _dot(jnp.concatenate(parts, axis=0), w_ref[...]).astype(o_ref.dtype)


def _modmm(x, mod, w, nb, nctx_b, tm):
    n, k = x.shape
    nc = w.shape[1]
    tn = _pick_tile(nc, 2048)
    sub = max(s for s in (4, 2, 1) if (n // tm) % s == 0)
    kern = functools.partial(_mm_kernel, sub=sub, tm=tm, nb=nb, nctx_b=nctx_b)
    return pl.pallas_call(
        kern,
        out_shape=jax.ShapeDtypeStruct((n, nc), F32),
        grid=(nc // tn, n // (tm * sub)),
        in_specs=[pl.BlockSpec((tm * sub, k), lambda j, i: (i, 0)),
                  pl.BlockSpec(mod.shape, lambda j, i: (0, 0, 0)),
                  pl.BlockSpec((k, tn), lambda j, i: (0, j))],
        out_specs=pl.BlockSpec((tm * sub, tn), lambda j, i: (i, j)),
        compiler_params=_cparams(("parallel", "parallel")),
        name="in_proj",
    )(x, mod, w)


def _mla_prep_kernel(p_ref, qn_ref, kvn_ref, wq_ref, wk_ref, wv_ref, cos_ref, sin_ref,
                     qt_ref, k_ref, vt_ref):
    p = p_ref[...]
    dq = p[:, 0:MLA_Q_RANK]
    dkv = p[:, MLA_Q_RANK:MLA_Q_RANK + MLA_KV_RANK]
    krp = p[:, MLA_Q_RANK + MLA_KV_RANK:]
    qn = dq * lax.rsqrt(jnp.mean(dq * dq, axis=-1, keepdims=True) + RMS_EPS) * qn_ref[...]
    kvn = dkv * lax.rsqrt(jnp.mean(dkv * dkv, axis=-1, keepdims=True) + RMS_EPS) * kvn_ref[...]
    kvn = kvn.astype(BF16)
    cos = cos_ref[...]
    sin = sin_ref[...]
    cos_h = jnp.concatenate([cos] * MLA_HEADS, axis=1)
    sin_h = jnp.concatenate([sin] * MLA_HEADS, axis=1)
    q = _rope(_dot(qn.astype(BF16), wq_ref[...]), cos_h, sin_h) * MLA_SCALE
    kr = _rope(krp, cos, sin)
    k = _dot(kvn, wk_ref[...]) + jnp.concatenate([kr] * MLA_HEADS, axis=1)
    qt_ref[0] = _heads_t(q).astype(BF16)
    k_ref[...] = k.astype(BF16)
    vt_ref[0, 0] = _heads_t(_with_ones_lane(_dot(kvn, wv_ref[...]))).astype(BF16)


def _mla_prep(pm, qn, kvn, wq, wk, wv, cos, sin, nb, tm):
    n = pm.shape[0]
    hw = MLA_HEADS * LANES
    bsz = n // (nb * tm)
    full = lambda a: pl.BlockSpec(a.shape, lambda i: (0,) * a.ndim)
    return pl.pallas_call(
        _mla_prep_kernel,
        out_shape=(jax.ShapeDtypeStruct((bsz, hw, nb * tm), BF16),
                   jax.ShapeDtypeStruct((n, hw), BF16),
                   jax.ShapeDtypeStruct((bsz, nb, hw, tm), BF16)),
        grid=(n // tm,),
        in_specs=[pl.BlockSpec((tm, pm.shape[1]), lambda i: (i, 0)),
                  full(qn), full(kvn), full(wq), full(wk), full(wv),
                  pl.BlockSpec((tm, LANES), lambda i: (i % nb, 0)),
                  pl.BlockSpec((tm, LANES), lambda i: (i % nb, 0))],
        out_specs=(pl.BlockSpec((1, hw, tm), lambda i: (i // nb, 0, i % nb)),
                   pl.BlockSpec((tm, hw), lambda i: (i, 0)),
                   pl.BlockSpec((1, 1, hw, tm), lambda i: (i // nb, i % nb, 0, 0))),
        compiler_params=_cparams(("parallel",)),
        name="mla_prep",
    )(pm, qn, kvn, wq, wk, wv, cos, sin)


def _gqa_prep_kernel(p_ref, qn_ref, kn_ref, cos_ref, sin_ref, q_ref, k_ref, v_ref,
                     *, n_q, n_kv, hd, scale, qk_norm, transposed):
    p = p_ref[...]
    cos = cos_ref[...]
    sin = sin_ref[...]

    def head(j, gain):
        x = p[:, j * LANES:(j + 1) * LANES]
        if qk_norm:
            ms = jnp.sum(x * x, axis=-1, keepdims=True) * (1.0 / hd)
            x = x * lax.rsqrt(ms + RMS_EPS) * gain
        return _rope(x, cos, sin)

    qg = qn_ref[...]
    kg = kn_ref[...]
    q = jnp.concatenate([head(j, qg) for j in range(n_q)], axis=1) * scale
    k = jnp.concatenate([head(n_q + j, kg) for j in range(n_kv)], axis=1)
    v = _with_ones_lane(p[:, (n_q + n_kv) * LANES:])
    k_ref[...] = k.astype(BF16)
    if transposed:
        q_ref[0] = _heads_t(q).astype(BF16)
        v_ref[0, 0] = _heads_t(v).astype(BF16)
    else:
        q_ref[...] = q.astype(BF16)
        v_ref[...] = v.astype(BF16)


def _gqa_prep(pg, qn, kn, cos, sin, nb, tm, *, n_q, n_kv, hd, scale, qk_norm, transposed):
    n = pg.shape[0]
    bsz = n // (nb * tm)
    full = lambda a: pl.BlockSpec(a.shape, lambda i: (0,) * a.ndim)
    kern = functools.partial(_gqa_prep_kernel, n_q=n_q, n_kv=n_kv, hd=hd, scale=scale,
                             qk_norm=qk_norm, transposed=transposed)
    if transposed:
        q_shape = jax.ShapeDtypeStruct((bsz, n_q * LANES, nb * tm), BF16)
        v_shape = jax.ShapeDtypeStruct((bsz, nb, n_kv * LANES, tm), BF16)
        q_spec = pl.BlockSpec((1, n_q * LANES, tm), lambda i: (i // nb, 0, i % nb))
        v_spec = pl.BlockSpec((1, 1, n_kv * LANES, tm), lambda i: (i // nb, i % nb, 0, 0))
    else:
        q_shape = jax.ShapeDtypeStruct((n, n_q * LANES), BF16)
        v_shape = jax.ShapeDtypeStruct((n, n_kv * LANES), BF16)
        q_spec = pl.BlockSpec((tm, n_q * LANES), lambda i: (i, 0))
        v_spec = pl.BlockSpec((tm, n_kv * LANES), lambda i: (i, 0))
    return pl.pallas_call(
        kern,
        out_shape=(q_shape, jax.ShapeDtypeStruct((n, n_kv * LANES), BF16), v_shape),
        grid=(n // tm,),
        in_specs=[pl.BlockSpec((tm, pg.shape[1]), lambda i: (i, 0)),
                  full(qn), full(kn),
                  pl.BlockSpec((tm, LANES), lambda i: (i % nb, 0)),
                  pl.BlockSpec((tm, LANES), lambda i: (i % nb, 0))],
        out_specs=(q_spec, pl.BlockSpec((tm, n_kv * LANES), lambda i: (i, 0)), v_spec),
        compiler_params=_cparams(("parallel",)),
        name="gqa_prep",
    )(pg, qn, kn, cos, sin)


ATTN_GROUPS_IN_FLIGHT = 8


def _attn_kernel(qt_ref, k_ref, vt_ref, o_ref, m_s, acc_s, *, n_q, n_kv, tq, tk, nctx_q,
                 nk_ctx, nk_all):
    qi = pl.program_id(1)
    nkb = jnp.where(qi < nctx_q, nk_ctx, nk_all)
    grp = n_q // n_kv
    gs = range(n_kv)
    m_s[...] = jnp.full(m_s.shape, NEG_INF, F32)
    acc_s[...] = jnp.zeros(acc_s.shape, F32)

    def body(kb, carry):
        ks = pl.multiple_of(kb * tk, tk)
        for g0 in range(0, n_kv, ATTN_GROUPS_IN_FLIGHT):
            gb = range(g0, min(g0 + ATTN_GROUPS_IN_FLIGHT, n_kv))
            s, m_old, m_new, p, a, pv = {}, {}, {}, {}, {}, {}
            for g in gb:
                qtg = jnp.concatenate(
                    [qt_ref[0, (g * grp + j) * LANES:(g * grp + j + 1) * LANES, :]
                     for j in range(grp)], axis=1)
                s[g] = _dot(k_ref[0, pl.ds(ks, tk), g * LANES:(g + 1) * LANES], qtg)
            for g in gb:
                m_old[g] = m_s[g]
                m_new[g] = jnp.maximum(m_old[g], jnp.max(s[g], axis=0, keepdims=True))
            for g in gb:
                p[g] = jnp.exp(s[g] - m_new[g]).astype(BF16)
                a[g] = jnp.exp(m_old[g] - m_new[g])
            for g in gb:
                pv[g] = _dot(vt_ref[0, kb, g * LANES:(g + 1) * LANES, :], p[g])
            for g in gb:
                m_s[g] = m_new[g]
                acc_s[g] = a[g] * acc_s[g] + pv[g]
        return carry

    lax.fori_loop(0, nkb, body, 0)
    row = lax.broadcasted_iota(I32, (LANES, 1), 0)
    for g in gs:
        acc = acc_s[g]
        ot = jnp.where(row < ONES_LANE, acc / acc[ONES_LANE:ONES_LANE + 1, :], 0.0)
        for a in range(grp):
            h = g * grp + a
            o_ref[0, :, h * LANES:(h + 1) * LANES] = ot[:, a * tq:(a + 1) * tq].T.astype(o_ref.dtype)


def _attention(qt, k, vt, ctx_len, *, n_q, n_kv, tq):
    b, l, _ = k.shape
    tk = vt.shape[-1]
    rows = (n_q // n_kv) * tq
    kern = functools.partial(_attn_kernel, n_q=n_q, n_kv=n_kv, tq=tq, tk=tk,
                             nctx_q=ctx_len // tq, nk_ctx=ctx_len // tk, nk_all=l // tk)
    return pl.pallas_call(
        kern,
        out_shape=jax.ShapeDtypeStruct((b, l, n_q * LANES), BF16),
        grid=(b, l // tq),
        in_specs=[pl.BlockSpec((1, n_q * LANES, tq), lambda bi, i: (bi, 0, i)),
                  pl.BlockSpec((1, l, n_kv * LANES), lambda bi, i: (bi, 0, 0)),
                  pl.BlockSpec((1, l // tk, n_kv * LANES, tk), lambda bi, i: (bi, 0, 0, 0))],
        out_specs=pl.BlockSpec((1, tq, n_q * LANES), lambda bi, i: (bi, i, 0)),
        scratch_shapes=[pltpu.VMEM((n_kv, 1, rows), F32), pltpu.VMEM((n_kv, LANES, rows), F32)],
        compiler_params=_cparams(("parallel", "arbitrary")),
        name="dense_attn",
    )(qt, k, vt)


def _win_kernel(sink_ref, q_ref, kc_ref, kp_ref, kcur_ref, kn_ref, vc_ref, vp_ref, vcur_ref,
                vn_ref, o_ref, *, n_q, n_kv, nctx_b, nb, ctx_len):
    qi = pl.program_id(1)
    w = WINDOW
    is_lat = (qi >= nctx_b).astype(I32)
    prev_ok = is_lat * (qi - 1 >= nctx_b).astype(I32)
    next_ok = is_lat * (qi + 1 < nb).astype(I32)
    nk = ctx_len + 3 * w
    r = lax.broadcasted_iota(I32, (w, nk), 0)
    c2 = lax.broadcasted_iota(I32, (w, nk), 1)
    c = c2 - ctx_len
    near = jnp.abs(r - (c - w)) <= w
    blk_ok = jnp.where(c < w, prev_ok, jnp.where(c < 2 * w, is_lat, next_ok)) > 0
    valid = jnp.logical_or(c2 < ctx_len, jnp.logical_and(near, blk_ok))
    bias = jnp.where(valid, 0.0, NEG_INF).astype(F32)
    grp = n_q // n_kv
    bias = jnp.concatenate([bias] * grp, axis=0)
    for g in range(n_kv):
        ls = slice(g * LANES, (g + 1) * LANES)
        qg = jnp.concatenate(
            [q_ref[0, :, (g * grp + a) * LANES:(g * grp + a + 1) * LANES] for a in range(grp)],
            axis=0)
        kcat = jnp.concatenate([kc_ref[0, :, ls], kp_ref[0, :, ls], kcur_ref[0, :, ls],
                                kn_ref[0, :, ls]], axis=0)
        vcat = jnp.concatenate([vc_ref[0, :, ls], vp_ref[0, :, ls], vcur_ref[0, :, ls],
                                vn_ref[0, :, ls]], axis=0)
        s = _dot_nt(qg, kcat) + bias
        sk = jnp.concatenate([jnp.full((w, 1), sink_ref[g * grp + a], F32) for a in range(grp)],
                             axis=0)
        m = jnp.maximum(jnp.max(s, axis=1, keepdims=True), sk)
        p = jnp.exp(s - m)
        den = jnp.sum(p, axis=1, keepdims=True) + jnp.exp(sk - m)
        o = _dot(p.astype(BF16), vcat) / den
        for a in range(grp):
            h = g * grp + a
            o_ref[0, :, h * LANES:(h + 1) * LANES] = o[a * w:(a + 1) * w].astype(o_ref.dtype)


def _window_attention(q, k, v, sink, ctx_len, *, n_q, n_kv):
    b, l, _ = q.shape
    w = WINDOW
    nb = l // w
    nctx_b = ctx_len // w
    kw = n_kv * LANES
    kern = functools.partial(_win_kernel, n_q=n_q, n_kv=n_kv, nctx_b=nctx_b, nb=nb,
                             ctx_len=ctx_len)
    ctx_spec = pl.BlockSpec((1, ctx_len, kw), lambda bi, i: (bi, 0, 0))
    prev_spec = pl.BlockSpec((1, w, kw), lambda bi, i: (bi, jnp.maximum(i - 1, 0), 0))
    cur_spec = pl.BlockSpec((1, w, kw), lambda bi, i: (bi, i, 0))
    next_spec = pl.BlockSpec((1, w, kw), lambda bi, i: (bi, jnp.minimum(i + 1, nb - 1), 0))
    return pl.pallas_call(
        kern,
        out_shape=jax.ShapeDtypeStruct((b, l, n_q * LANES), BF16),
        grid=(b, nb),
        in_specs=[pl.BlockSpec(memory_space=pltpu.SMEM),
                  pl.BlockSpec((1, w, n_q * LANES), lambda bi, i: (bi, i, 0)),
                  ctx_spec, prev_spec, cur_spec, next_spec,
                  ctx_spec, prev_spec, cur_spec, next_spec],
        out_specs=pl.BlockSpec((1, w, n_q * LANES), lambda bi, i: (bi, i, 0)),
        compiler_params=_cparams(("parallel", "arbitrary")),
        name="window_attn",
    )(sink, q, k, k, k, k, v, v, v, v)


def _rwkv_feat_kernel(p_ref, hp_ref, hn_ref, mu_ref, w0_ref, w2_ref, a0_ref, a2_ref, g2_ref,
                      kk_ref, ka_ref, rk_ref,
                      r_out, v_out, kk_out, g_out, bonus_out, lw_out, k_out, b_out,
                      *, tm, nb, nctx_b):
    i = pl.program_id(0) % nb
    seq_start = jnp.logical_or(i == 0, i == nctx_b)
    seq_end = jnp.logical_or(i == nctx_b - 1, i == nb - 1)
    p = p_ref[...]
    row = lax.broadcasted_iota(I32, p.shape, 0)
    first = jnp.where(seq_start, 0.0, hp_ref[7:8, :])
    last = jnp.where(seq_end, 0.0, hn_ref[0:1, :])
    prev = jnp.where(row == 0, first, pltpu.roll(p, 1, 0))
    nxt = jnp.where(row == tm - 1, last, pltpu.roll(p, tm - 1, 0))
    mu = mu_ref[...]
    ps = p + mu[0:1, :] * (prev - p) + mu[1:2, :] * (nxt - p)
    hw = RWKV_HEADS * LANES
    r = ps[:, 0:hw]
    k = ps[:, hw:2 * hw]
    v = ps[:, 2 * hw:3 * hw]
    wfb = ps[:, 3 * hw:3 * hw + LANES]
    afb = ps[:, 3 * hw + LANES:3 * hw + 2 * LANES]
    gi = ps[:, 3 * hw + 2 * LANES:]
    kkr = k * kk_ref[...]
    parts = []
    for h in range(RWKV_HEADS):
        x = kkr[:, h * LANES:(h + 1) * LANES]
        nrm = jnp.sqrt(jnp.sum(x * x, axis=-1, keepdims=True))
        parts.append(x / jnp.maximum(nrm, 1e-12))
    kk = jnp.concatenate(parts, axis=1)
    z = w0_ref[...] + _dot(jnp.tanh(wfb).astype(BF16), w2_ref[...])
    lw = -math.exp(-0.5) * jax.nn.sigmoid(z)
    a = jax.nn.sigmoid(a0_ref[...] + _dot(afb.astype(BF16), a2_ref[...]))
    ka = ka_ref[...]
    k0 = k * (1.0 + (a[:, 0:hw] - 1.0) * ka)
    k1 = k * (1.0 + (a[:, hw:] - 1.0) * ka)
    rkk = r * (k0 + k1) * rk_ref[...]
    bparts = []
    for h in range(RWKV_HEADS):
        sl = slice(h * LANES, (h + 1) * LANES)
        bparts.append(jnp.sum(rkk[:, sl], axis=-1, keepdims=True) * v[:, sl])
    r_out[...] = r
    v_out[...] = v
    kk_out[...] = kk
    g_out[...] = _dot(jax.nn.sigmoid(gi).astype(BF16), g2_ref[...])
    bonus_out[...] = jnp.concatenate(bparts, axis=1)
    lw_out[0] = lw[:, 0:hw]
    lw_out[1] = lw[:, hw:]
    k_out[0] = k0
    k_out[1] = k1
    b_out[0] = a[:, 0:hw] * kk
    b_out[1] = a[:, hw:] * kk


def _rwkv_features(pr, mu, w0, w2, a0, a2, g2, kk, ka, rk, nb, nctx_b, tm):
    n, wid = pr.shape
    hw = RWKV_HEADS * LANES
    full = lambda a: pl.BlockSpec(a.shape, lambda i: (0,) * a.ndim)
    kern = functools.partial(_rwkv_feat_kernel, tm=tm, nb=nb, nctx_b=nctx_b)
    one = jax.ShapeDtypeStruct((n, hw), F32)
    two = jax.ShapeDtypeStruct((2, n, hw), F32)
    s1 = pl.BlockSpec((tm, hw), lambda i: (i, 0))
    s2 = pl.BlockSpec((2, tm, hw), lambda i: (0, i, 0))
    t8 = tm // 8
    return pl.pallas_call(
        kern,
        out_shape=(one, one, one, one, one, two, two, two),
        grid=(n // tm,),
        in_specs=[pl.BlockSpec((tm, wid), lambda i: (i, 0)),
                  pl.BlockSpec((8, wid), lambda i: (jnp.maximum(i * t8 - 1, 0), 0)),
                  pl.BlockSpec((8, wid), lambda i: (jnp.minimum((i + 1) * t8, n // 8 - 1), 0)),
                  full(mu), full(w0), full(w2), full(a0), full(a2), full(g2),
                  full(kk), full(ka), full(rk)],
        out_specs=(s1, s1, s1, s1, s1, s2, s2, s2),
        compiler_params=_cparams(("parallel",)),
        name="rwkv_features",
    )(pr, pr, pr, mu, w0, w2, a0, a2, g2, kk, ka, rk)


def _split3_dot(mask_bf16, x):
    x1 = x.astype(BF16)
    r1 = x - x1.astype(F32)
    x2 = r1.astype(BF16)
    x3 = (r1 - x2.astype(F32)).astype(BF16)
    return _dot(mask_bf16, x1) + _dot(mask_bf16, x2) + _dot(mask_bf16, x3)


def _rwkv_scan_kernel(r_ref, v_ref, kk_ref, lw_ref, k_ref, b_ref, o_ref, s_ref):
    d = pl.program_id(1)
    c = pl.program_id(2)
    cs = RWKV_CHUNK

    @pl.when(c == 0)
    def _():
        s_ref[...] = jnp.zeros_like(s_ref)

    rev = d == 1
    t_i = lax.broadcasted_iota(I32, (cs, cs), 0)
    s_i = lax.broadcasted_iota(I32, (cs, cs), 1)
    order = jnp.where(rev, t_i - s_i, s_i - t_i)
    incl = order <= 0
    strict = order < 0
    eye = jnp.where(t_i == s_i, 1.0, 0.0).astype(F32)
    off_masks = []
    for lvl in range(int(math.log2(cs))):
        pair = (t_i >> (lvl + 1)) == (s_i >> (lvl + 1))
        half = (t_i >> lvl) != (s_i >> lvl)
        off = jnp.logical_and(jnp.logical_and(pair, half), strict)
        off_masks.append(jnp.where(off, 1.0, 0.0).astype(F32))
    lw = lw_ref[0]
    cum = _split3_dot(jnp.where(incl, 1.0, 0.0).astype(BF16), lw)
    p_in = jnp.exp(cum)
    p_inv = jnp.exp(-cum)
    p_ex = jnp.exp(cum - lw)
    tot = jnp.where(rev, cum[0:1, :], cum[cs - 1:cs, :])
    p_all = jnp.exp(tot)
    a_t = -kk_ref[...] * p_ex
    r_t = r_ref[...] * p_in
    k_t = k_ref[0] * p_inv
    b_t = b_ref[0] * p_inv
    v = v_ref[...]
    hs = range(RWKV_HEADS)
    sls = [slice(h * LANES, (h + 1) * LANES) for h in hs]
    vb = [v[:, sl].astype(BF16) for sl in sls]
    bk = [jnp.concatenate([b_t[:, sl], k_t[:, sl]], axis=0).astype(BF16) for sl in sls]
    gm = [_dot_nt(jnp.concatenate([a_t[:, sls[h]], r_t[:, sls[h]]], axis=0).astype(BF16), bk[h])
          for h in hs]
    m_ab = [jnp.where(strict, g[0:cs, 0:cs], 0.0) for g in gm]
    mkv = [_dot(jnp.where(strict, gm[h][0:cs, cs:], 0.0).astype(BF16), vb[h]) for h in hs]
    x = [eye + m * off_masks[0] for m in m_ab]
    for lvl in range(1, len(off_masks)):
        xb = [xx.astype(BF16) for xx in x]
        t1 = [_dot(xb[h], (m_ab[h] * off_masks[lvl]).astype(BF16)).astype(BF16) for h in hs]
        x = [x[h] + _dot(t1[h], xb[h]) for h in hs]
    z = [_dot(x[h].astype(BF16),
              jnp.concatenate([a_t[:, sls[h]], mkv[h]], axis=1).astype(BF16)) for h in hs]
    zb = [zz.astype(BF16) for zz in z]
    gy = [_dot(jnp.where(incl, gm[h][cs:, 0:cs], 0.0).astype(BF16), zb[h]) for h in hs]
    y0 = [_dot(jnp.where(incl, gm[h][cs:, cs:], 0.0).astype(BF16), vb[h]) + gy[h][:, LANES:]
          for h in hs]
    s0 = [s_ref[h] for h in hs]
    s0b = [s.astype(BF16) for s in s0]
    u = [_dot_nt(zb[h][:, 0:LANES], s0b[h]) + z[h][:, LANES:] for h in hs]
    for h in hs:
        g_mat = r_t[:, sls[h]] + gy[h][:, 0:LANES]
        o_ref[0, :, sls[h]] = _dot_nt(g_mat.astype(BF16), s0b[h]) + y0[h]
    for h in hs:
        uv = jnp.concatenate([u[h], v[:, sls[h]]], axis=0).astype(BF16)
        s_ref[h] = (s0[h] + _dot_tn(uv, bk[h])) * p_all[:, sls[h]]


def _rwkv_scan(r, v, kk, lw, k, bb, bsz, nctx_c):
    n, hw = r.shape
    cs = RWKV_CHUNK
    nc = n // bsz // cs

    def blk(bi, d, c):
        rc = jnp.where(c < nctx_c, nctx_c - 1 - c, nc - 1 - (c - nctx_c))
        return bi * nc + jnp.where(d == 0, c, rc)

    s1 = pl.BlockSpec((cs, hw), lambda bi, d, c: (blk(bi, d, c), 0))
    s2 = pl.BlockSpec((1, cs, hw), lambda bi, d, c: (d, blk(bi, d, c), 0))
    return pl.pallas_call(
        _rwkv_scan_kernel,
        out_shape=jax.ShapeDtypeStruct((2, n, hw), F32),
        grid=(bsz, 2, nc),
        in_specs=[s1, s1, s1, s2, s2, s2],
        out_specs=s2,
        scratch_shapes=[pltpu.VMEM((RWKV_HEADS, LANES, LANES), F32)],
        compiler_params=_cparams(("parallel", "parallel", "arbitrary")),
        name="rwkv_scan",
    )(r, v, kk, lw, k, bb)


def _rwkv_out_kernel(o_ref, bonus_ref, g_ref, lng_ref, lnb_ref, y_ref):
    o = o_ref[0] + o_ref[1]
    lane = lax.broadcasted_iota(I32, (1, LANES), 1)
    real = lane < RWKV_HEAD
    lng = lng_ref[...]
    lnb = lnb_ref[...]
    parts = []
    for h in range(RWKV_HEADS):
        sl = slice(h * LANES, (h + 1) * LANES)
        x = o[:, sl]
        mu = jnp.sum(x, axis=-1, keepdims=True) * (1.0 / RWKV_HEAD)
        dlt = jnp.where(real, x - mu, 0.0)
        var = jnp.sum(dlt * dlt, axis=-1, keepdims=True) * (1.0 / RWKV_HEAD)
        parts.append(dlt * lax.rsqrt(var + RWKV_GN_EPS) * lng[:, sl] + lnb[:, sl])
    y = (jnp.concatenate(parts, axis=1) + bonus_ref[...]) * g_ref[...]
    y_ref[...] = y.astype(y_ref.dtype)


def _rwkv_out(o, bonus, g, lng, lnb, tm):
    _, n, hw = o.shape
    full = lambda a: pl.BlockSpec(a.shape, lambda i: (0,) * a.ndim)
    s1 = pl.BlockSpec((tm, hw), lambda i: (i, 0))
    return pl.pallas_call(
        _rwkv_out_kernel,
        out_shape=jax.ShapeDtypeStruct((n, hw), BF16),
        grid=(n // tm,),
        in_specs=[pl.BlockSpec((2, tm, hw), lambda i: (0, i, 0)), s1, s1, full(lng), full(lnb)],
        out_specs=s1,
        compiler_params=_cparams(("parallel",)),
        name="rwkv_out",
    )(o, bonus, g, lng, lnb)


def _merge_kernel(ya_ref, yb_ref, yc_ref, yd_ref, gate_ref, x_ref, gt_ref, wb_ref, wo_ref,
                  lng_ref, lnb_ref, o_ref, *, alpha, d):
    ys = (ya_ref, yb_ref, yc_ref, yd_ref)
    acc = None
    for i in range(N_BRANCH):
        gate = jax.nn.sigmoid(gate_ref[:, i * d:(i + 1) * d])
        term = gate * _dot(ys[i][...], wb_ref[i])
        acc = term if acc is None else acc + term
    mix = _dot(acc.astype(BF16), wo_ref[...])
    y = alpha * x_ref[...] + gt_ref[0] * mix
    o_ref[...] = _layer_norm(y, lng_ref[...], lnb_ref[...])


def _merge(ya, yb, yc, yd, gate, x, gt, wb, wo, lng, lnb, nb, nctx_b, tm, alpha):
    n, d = x.shape
    hw = ya.shape[1]
    full = lambda a: pl.BlockSpec(a.shape, lambda i: (0,) * a.ndim)
    sy = pl.BlockSpec((tm, hw), lambda i: (i, 0))
    kern = functools.partial(_merge_kernel, alpha=alpha, d=d)
    return pl.pallas_call(
        kern,
        out_shape=jax.ShapeDtypeStruct((n, d), F32),
        grid=(n // tm,),
        in_specs=[sy, sy, sy, sy,
                  pl.BlockSpec((tm, N_BRANCH * d), lambda i: (i, 0)),
                  pl.BlockSpec((tm, d), lambda i: (i, 0)),
                  pl.BlockSpec((1, 1, d), lambda i: (_group_index(i, nb, nctx_b), 0, 0)),
                  full(wb), full(wo), full(lng), full(lnb)],
        out_specs=pl.BlockSpec((tm, d), lambda i: (i, 0)),
        compiler_params=_cparams(("parallel",)),
        name="merge",
    )(ya, yb, yc, yd, gate, x, gt, wb, wo, lng, lnb)


def _extract_topk(srcs, n_rows, k, val_refs, pos_refs):
    rio = lax.broadcasted_iota(I32, srcs[0].shape, 0).astype(F32)
    js = range(len(srcs))
    for rnk in range(k):
        m = [jnp.max(s, axis=0, keepdims=True) for s in srcs]
        pos = [jnp.min(jnp.where(srcs[j] == m[j], rio, float(n_rows)), axis=0, keepdims=True)
               for j in js]
        for j in js:
            val_refs[j][rnk:rnk + 1, :] = m[j]
            pos_refs[j][rnk:rnk + 1, :] = pos[j]
        srcs = [jnp.where(rio == pos[j], -jnp.inf, srcs[j]) for j in js]


PEER_CAND_COUNTS = tuple(PEER_TOPK // (a + 1) for a in range(PEER_TOPK))
PEER_N_CAND = sum(PEER_CAND_COUNTS)
PEER_CAND_ROWS = -(-PEER_N_CAND // 8) * 8


def _peer_topk_kernel(x_ref, mod_ref, wq_ref, k1_ref, k2_ref, h_ref, idx_ref, wgt_ref, off_ref,
                      q_s, v1_s, i1_s, v2_s, i2_s, cand_s, cidx_s, best_s, pos_s):
    hd = pl.program_id(1)

    @pl.when(hd == 0)
    def _():
        m = mod_ref[0]
        hh = x_ref[...] * m[0:1, :] + m[1:2, :]
        h_ref[...] = hh
        q = _dot(hh.astype(BF16), wq_ref[...])
        for a in range(PEER_HEADS):
            q_s[a] = q[:, a * LANES:(a + 1) * LANES].astype(BF16)

    qh = q_s[hd]
    _extract_topk([_dot_nt(k1_ref[0], qh), _dot_nt(k2_ref[0], qh)],
                  PEER_N_KEYS, PEER_TOPK, [v1_s, v2_s], [i1_s, i2_s])
    row = 0
    for a, cnt in enumerate(PEER_CAND_COUNTS):
        cand_s[row:row + cnt, :] = v1_s[a:a + 1, :] + v2_s[0:cnt, :]
        cidx_s[row:row + cnt, :] = i1_s[a:a + 1, :] * float(PEER_N_KEYS) + i2_s[0:cnt, :]
        row += cnt
    pad = PEER_CAND_ROWS - PEER_N_CAND
    if pad:
        cand_s[PEER_N_CAND:, :] = jnp.full((pad, cand_s.shape[1]), -jnp.inf, F32)
        cidx_s[PEER_N_CAND:, :] = jnp.zeros((pad, cand_s.shape[1]), F32)
    _extract_topk([cand_s[...]], PEER_CAND_ROWS, PEER_TOPK, [best_s], [pos_s])
    cidx = cidx_s[...]
    rio = lax.broadcasted_iota(I32, cidx.shape, 0).astype(F32)
    lo = None
    for rnk in range(PEER_TOPK):
        sel = rio == pos_s[rnk:rnk + 1, :]
        e_id = jnp.max(jnp.where(sel, cidx, -1.0), axis=0, keepdims=True).astype(I32)
        idx_ref[0, rnk:rnk + 1, :] = e_id
        off = (e_id >> 1) * 8
        if rnk % 2 == 0:
            lo = off
        else:
            off_ref[0, rnk // 2:rnk // 2 + 1, :] = lo | (off << 16)
    best = best_s[...]
    e = jnp.exp(best - best[0:1, :])
    wgt_ref[0] = e / jnp.sum(e, axis=0, keepdims=True)


def _peer_topk(x, mod, wq, k1, k2, nb, nctx_b, tm):
    n, d = x.shape
    tk = PEER_TOPK
    full = lambda a: pl.BlockSpec(a.shape, lambda i, h: (0,) * a.ndim)
    return pl.pallas_call(
        _peer_topk_kernel,
        out_shape=(jax.ShapeDtypeStruct((n, d), F32),
                   jax.ShapeDtypeStruct((PEER_HEADS, tk, n), I32),
                   jax.ShapeDtypeStruct((PEER_HEADS, tk, n), F32),
                   jax.ShapeDtypeStruct((PEER_HEADS, tk // 2, n), I32)),
        grid=(n // tm, PEER_HEADS),
        in_specs=[pl.BlockSpec((tm, d), lambda i, h: (i, 0)),
                  pl.BlockSpec((1, 2, d), lambda i, h: (_group_index(i, nb, nctx_b), 0, 0)),
                  full(wq),
                  pl.BlockSpec((1, PEER_N_KEYS, LANES), lambda i, h: (h, 0, 0)),
                  pl.BlockSpec((1, PEER_N_KEYS, LANES), lambda i, h: (h, 0, 0))],
        out_specs=(pl.BlockSpec((tm, d), lambda i, h: (i, 0)),
                   pl.BlockSpec((1, tk, tm), lambda i, h: (h, 0, i)),
                   pl.BlockSpec((1, tk, tm), lambda i, h: (h, 0, i)),
                   pl.BlockSpec((1, tk // 2, tm), lambda i, h: (h, 0, i))),
        scratch_shapes=[pltpu.VMEM((PEER_HEADS, tm, LANES), BF16),
                        pltpu.VMEM((tk, tm), F32), pltpu.VMEM((tk, tm), F32),
                        pltpu.VMEM((tk, tm), F32), pltpu.VMEM((tk, tm), F32),
                        pltpu.VMEM((PEER_CAND_ROWS, tm), F32), pltpu.VMEM((PEER_CAND_ROWS, tm), F32),
                        pltpu.VMEM((tk, tm), F32), pltpu.VMEM((tk, tm), F32)],
        compiler_params=_cparams(("parallel", "arbitrary")),
        name="peer_topk",
    )(x, mod, wq, k1, k2)


PEER_NE = PEER_HEADS * PEER_TOPK
PEER_COLS = PEER_NE * 16
PEER_TOK_UNROLL = 8


def _table_spec(tab):
    return pl.BlockSpec(tab.shape, lambda i: (0, 0), pipeline_mode=pl.Buffered(1))


def _gather_view(off_ref, tab_v, t):
    tiles = []
    tok_ref = off_ref.at[pl.ds(t * (PEER_NE // 2), PEER_NE // 2)]
    for j in range(PEER_NE // 2):
        w = tok_ref[j]
        o0 = pl.multiple_of(w & 0xFFFF, 8)
        o1 = pl.multiple_of(lax.shift_right_logical(w, 16), 8)
        tiles.append(tab_v[pl.ds(o0, 8), :])
        tiles.append(tab_v[pl.ds(o1, 8), :])
    return pltpu.bitcast(jnp.concatenate(tiles, axis=0), BF16)


def _select_mask(px_row):
    shp = (8, PEER_COLS)
    row = lax.broadcasted_iota(I32, shp, 0)
    col = lax.broadcasted_iota(I32, shp, 1)
    sub = (col >> 1) & 7
    fixed = jnp.logical_and((col & 1) == (row >> 2), (sub & 3) == (row & 3))
    return jnp.logical_and(fixed, (sub >> 2).astype(F32) == px_row)


def _split2(x):
    x1 = x.astype(BF16)
    return x1, (x - x1.astype(F32)).astype(BF16)


def _peer_u_kernel(off_ref, h_ref, par_ref, wgt_ref, e16_ref, g16_ref, tab_v, c_ref,
                   px_s, d_s, *, tb):
    px_s[...] = _dot(par_ref[...].astype(BF16), e16_ref[...])

    def tokens(i, carry):
        for u in range(PEER_TOK_UNROLL):
            t = i * PEER_TOK_UNROLL + u
            view = _gather_view(off_ref, tab_v, t)
            h1, h2 = _split2(h_ref[t])
            dd = _dot_nt(jnp.concatenate([h1, h2], axis=0), view)
            dd = jnp.where(_select_mask(px_s[pl.ds(t, 1), :]), dd[0:8] + dd[8:16], 0.0)
            d_s[pl.ds(t, 1), :] = jnp.sum(dd, axis=0, keepdims=True)
        return carry

    lax.fori_loop(0, tb // PEER_TOK_UNROLL, tokens, 0)
    d1, d2 = _split2(d_s[...])
    g16 = g16_ref[...]
    act = _dot(d1, g16) + _dot(d2, g16)
    gelu = 0.5 * act * (1.0 + lax.erf(act * (2.0 ** -0.5)))
    c_ref[...] = gelu * wgt_ref[...]


def _peer_u(off_flat, h3, par, wgt, e16, g16, tab, tb):
    n = h3.shape[0]
    full = lambda a: pl.BlockSpec(a.shape, lambda i: (0,) * a.ndim)
    kern = functools.partial(_peer_u_kernel, tb=tb)
    tok = pl.BlockSpec((tb, PEER_NE), lambda i: (i, 0))
    return pl.pallas_call(
        kern,
        out_shape=jax.ShapeDtypeStruct((n, PEER_NE), F32),
        grid=(n // tb,),
        in_specs=[pl.BlockSpec((tb * PEER_NE // 2,), lambda i: (i,), memory_space=pltpu.SMEM),
                  pl.BlockSpec((tb, 8, LANES), lambda i: (i, 0, 0)),
                  tok, tok, full(e16), full(g16), _table_spec(tab)],
        out_specs=tok,
        scratch_shapes=[pltpu.VMEM((tb, PEER_COLS), F32), pltpu.VMEM((tb, PEER_COLS), F32)],
        compiler_params=_cparams(("arbitrary",)),
        name="peer_u",
    )(off_flat, h3, par, wgt, e16, g16, tab)


def _peer_v_kernel(off_ref, c_ref, par_ref, e16_ref, tab_v, o_ref, px_s, c1_s, c2_s, *, tb):
    e16 = e16_ref[...]
    px_s[...] = _dot(par_ref[...].astype(BF16), e16)
    c1, c2 = _split2(c_ref[...])
    c1_s[...] = _dot(c1, e16)
    c2_s[...] = _dot(c2, e16)

    def tokens(i, carry):
        for u in range(PEER_TOK_UNROLL):
            t = i * PEER_TOK_UNROLL + u
            view = _gather_view(off_ref, tab_v, t)
            sel = _select_mask(px_s[pl.ds(t, 1), :])
            lhs = jnp.concatenate([jnp.where(sel, c1_s[pl.ds(t, 1), :], 0.0),
                                   jnp.where(sel, c2_s[pl.ds(t, 1), :], 0.0)], axis=0)
            out = _dot(lhs.astype(BF16), view)
            o_ref[t] = out[0:8] + out[8:16]
        return carry

    lax.fori_loop(0, tb // PEER_TOK_UNROLL, tokens, 0)


def _peer_v(off_flat, cw, par, e16, tab, tb):
    n = cw.shape[0]
    full = lambda a: pl.BlockSpec(a.shape, lambda i: (0,) * a.ndim)
    kern = functools.partial(_peer_v_kernel, tb=tb)
    tok = pl.BlockSpec((tb, PEER_NE), lambda i: (i, 0))
    return pl.pallas_call(
        kern,
        out_shape=jax.ShapeDtypeStruct((n, 8, LANES), F32),
        grid=(n // tb,),
        in_specs=[pl.BlockSpec((tb * PEER_NE // 2,), lambda i: (i,), memory_space=pltpu.SMEM),
                  tok, tok, full(e16), _table_spec(tab)],
        out_specs=pl.BlockSpec((tb, 8, LANES), lambda i: (i, 0, 0)),
        scratch_shapes=[pltpu.VMEM((tb, PEER_COLS), F32), pltpu.VMEM((tb, PEER_COLS), F32),
                        pltpu.VMEM((tb, PEER_COLS), F32)],
        compiler_params=_cparams(("arbitrary",)),
        name="peer_v",
    )(off_flat, cw, par, e16, tab)


def _pack_table(tab):
    e, d = tab.shape
    bits = lax.bitcast_convert_type(tab.astype(BF16), jnp.uint16).astype(U32)
    words = bits[:, :d // 2] | (bits[:, d // 2:] << 16)
    return words.reshape(e * 4, LANES)


def _ln_res_kernel(x_ref, f_ref, gt_ref, lng_ref, lnb_ref, o_ref, *, alpha):
    y = alpha * x_ref[...] + gt_ref[0] * f_ref[...]
    o_ref[...] = _layer_norm(y, lng_ref[...], lnb_ref[...])


def _ln_res(x, f, gt, lng, lnb, nb, nctx_b, tm, alpha):
    n, d = x.shape
    full = lambda a: pl.BlockSpec(a.shape, lambda i: (0,) * a.ndim)
    s1 = pl.BlockSpec((tm, d), lambda i: (i, 0))
    return pl.pallas_call(
        functools.partial(_ln_res_kernel, alpha=alpha),
        out_shape=jax.ShapeDtypeStruct((n, d), F32),
        grid=(n // tm,),
        in_specs=[s1, s1,
                  pl.BlockSpec((1, 1, d), lambda i: (_group_index(i, nb, nctx_b), 0, 0)),
                  full(lng), full(lnb)],
        out_specs=s1,
        compiler_params=_cparams(("parallel",)),
        name="ln_res",
    )(x, f, gt, lng, lnb)


def _pad_heads(w, n_heads, hd):
    lead = w.shape[:-1]
    w = w.reshape(lead + (n_heads, hd))
    w = jnp.pad(w, [(0, 0)] * len(lead) + [(0, 0), (0, LANES - hd)])
    return w.reshape(lead + (n_heads * LANES,))


def _pad_head_rows(w, n_heads, hd):
    d = w.shape[-1]
    w = w.reshape(n_heads, hd, d)
    w = jnp.pad(w, [(0, 0), (0, LANES - hd), (0, 0)])
    return w.reshape(n_heads * LANES, d)


def _rope_tables(rows, rot_dim, lane_off, ctx_len):
    r_idx = jnp.repeat(jnp.arange(rows), GRID_W).astype(F32)
    c_idx = jnp.tile(jnp.arange(GRID_W), rows).astype(F32)
    n = rot_dim // 4
    inv = ROPE_THETA ** (-jnp.arange(n, dtype=F32) / n)
    ang = jnp.concatenate([r_idx[:, None] * inv, c_idx[:, None] * inv], axis=-1)
    cos = jnp.repeat(jnp.cos(ang), 2, axis=-1)
    sin = jnp.repeat(jnp.sin(ang), 2, axis=-1) * jnp.tile(jnp.array([-1.0, 1.0], F32), rot_dim // 2)
    s = ang.shape[0]
    cos_t = jnp.ones((ctx_len + s, LANES), F32).at[ctx_len:, lane_off:lane_off + rot_dim].set(cos)
    sin_t = jnp.zeros((ctx_len + s, LANES), F32).at[ctx_len:, lane_off:lane_off + rot_dim].set(sin)
    return cos_t, sin_t


def _split_cols(w, widths):
    out, start = [], 0
    for wd in widths:
        out.append(w[..., start:start + wd])
        start += wd
    return out


def _forward(x, c, ctx, c_ctx, ada_w, ada_b, w_in, mla_q_norm, mla_kv_norm, mla_w_uq, mla_w_ukv,
             rwkv_mu, rwkv_w0, rwkv_w2, rwkv_a0, rwkv_a2, rwkv_g2, rwkv_k_k, rwkv_k_a, rwkv_r_k,
             rwkv_ln_g, rwkv_ln_b, gqa_q_norm, gqa_k_norm, win_sink, w_branch, w_out, ln1_g, ln1_b,
             peer_wq, peer_keys, peer_u, peer_v, ln2_g, ln2_b):
    stages = []
    bsz, seq, d = x.shape
    ctx_len = ctx.shape[1]
    depth = ada_w.shape[0]
    alpha = (2 * depth) ** 0.25
    l_tot = ctx_len + seq
    n = bsz * l_tot
    tm = 256
    tm_feat = 128
    assert ctx_len % tm == 0 and seq % tm == 0 and seq % GRID_W == 0
    nb = l_tot // tm
    nctx_b = ctx_len // tm
    rows = seq // GRID_W
    ne = PEER_HEADS * PEER_TOPK

    xs = jnp.concatenate([ctx, x], axis=1).reshape(n, d)
    cos_m, sin_m = _rope_tables(rows, MLA_ROPE, MLA_NOPE, ctx_len)
    cos_h, sin_h = _rope_tables(rows, GQA_HEAD, 0, ctx_len)

    m_rows = 16
    cvec = jnp.zeros((m_rows, d), F32).at[:bsz].set(c).at[bsz].set(c_ctx)
    mla_in = MLA_Q_RANK + MLA_KV_RANK + MLA_ROPE
    rwkv_in = 3 * RWKV_W + 2 * RWKV_W_LORA + 2 * RWKV_A_LORA + RWKV_G_LORA
    gqa_in = (GQA_HEADS + 2 * GQA_KV_HEADS) * GQA_HEAD
    win_in = (WIN_HEADS + 2 * WIN_KV_HEADS) * WIN_HEAD
    rw_widths = (RWKV_W, RWKV_W, RWKV_W, RWKV_W_LORA, RWKV_W_LORA, RWKV_A_LORA, RWKV_A_LORA,
                 RWKV_G_LORA)

    col16 = jnp.arange(PEER_COLS)
    e16 = (col16[None, :] // 16 == jnp.arange(ne)[:, None]).astype(BF16)
    g16 = e16.T

    for lyr in range(depth):
        mod = _ada(cvec, ada_w[lyr], ada_b[lyr])
        chunks = [mod[:, i * d:(i + 1) * d] for i in range(6)]

        def table(ch):
            lat = ch[:bsz]
            cx = jnp.broadcast_to(ch[bsz][None], (bsz, d))
            return jnp.stack([cx, lat], axis=1).reshape(bsz * 2, d)

        sh1, sc1, gt1, sh2, sc2, gt2 = [table(ch) for ch in chunks]
        mod1 = jnp.stack([1.0 + sc1, sh1], axis=1)
        mod2 = jnp.stack([1.0 + sc2, sh2], axis=1)
        gt1 = gt1[:, None, :]
        gt2 = gt2[:, None, :]

        wi = w_in[lyr]
        w_mla, w_rw, w_gq, w_wn, w_gate = _split_cols(wi, (mla_in, rwkv_in, gqa_in, win_in, N_BRANCH * d))
        zc = lambda k: jnp.zeros((d, k), F32)
        w_mla_p = jnp.concatenate([w_mla[:, :MLA_Q_RANK + MLA_KV_RANK], zc(MLA_NOPE),
                                   w_mla[:, MLA_Q_RANK + MLA_KV_RANK:], zc(LANES - MLA_NOPE - MLA_ROPE)],
                                  axis=1)
        rr, rk_, rv, rwf, rwb, raf, rab, rgi = _split_cols(w_rw, rw_widths)
        hp = lambda w: _pad_heads(w, RWKV_HEADS, RWKV_HEAD)
        w_rw_p = jnp.concatenate([hp(rr), hp(rk_), hp(rv), rwf, rwb, raf, rab, rgi], axis=1)
        mu_parts = _split_cols(rwkv_mu[lyr], rw_widths)
        mu_p = jnp.concatenate([hp(mu_parts[0]), hp(mu_parts[1]), hp(mu_parts[2])] + mu_parts[3:], axis=1)

        def gqa_cols(w, nq, nkv, hd):
            q_, k_, v_ = _split_cols(w, (nq * hd, nkv * hd, nkv * hd))
            return jnp.concatenate([_pad_heads(q_, nq, hd), _pad_heads(k_, nkv, hd),
                                    _pad_heads(v_, nkv, hd)], axis=1)

        w_gq_p = gqa_cols(w_gq, GQA_HEADS, GQA_KV_HEADS, GQA_HEAD)
        w_wn_p = gqa_cols(w_wn, WIN_HEADS, WIN_KV_HEADS, WIN_HEAD)

        pm = _modmm(xs, mod1, w_mla_p.astype(BF16), nb, nctx_b, tm)
        pr = _modmm(xs, mod1, w_rw_p.astype(BF16), nb, nctx_b, tm)
        pg = _modmm(xs, mod1, w_gq_p.astype(BF16), nb, nctx_b, tm)
        pw = _modmm(xs, mod1, w_wn_p.astype(BF16), nb, nctx_b, tm)
        gate = _modmm(xs, mod1, w_gate.astype(BF16), nb, nctx_b, tm)

        uq = mla_w_uq[lyr].reshape(MLA_Q_RANK, MLA_HEADS, MLA_NOPE + MLA_ROPE)
        uq = jnp.pad(uq, [(0, 0), (0, 0), (0, LANES - MLA_NOPE - MLA_ROPE)]).reshape(MLA_Q_RANK, -1)
        ukv = mla_w_ukv[lyr].reshape(MLA_KV_RANK, MLA_HEADS, MLA_NOPE + MLA_V)
        uk = jnp.pad(ukv[:, :, :MLA_NOPE], [(0, 0), (0, 0), (0, LANES - MLA_NOPE)]).reshape(MLA_KV_RANK, -1)
        uv = jnp.pad(ukv[:, :, MLA_NOPE:], [(0, 0), (0, 0), (0, LANES - MLA_V)]).reshape(MLA_KV_RANK, -1)
        qa, ka, va = _mla_prep(pm, mla_q_norm[lyr][None], mla_kv_norm[lyr][None], uq.astype(BF16),
                               uk.astype(BF16), uv.astype(BF16), cos_m, sin_m, nb, tm)
        r3 = lambda a: a.reshape(bsz, l_tot, a.shape[-1])
        ya = _attention(qa, r3(ka), va, ctx_len, n_q=MLA_HEADS, n_kv=MLA_HEADS, tq=256)

        zl = jnp.zeros((RWKV_W_LORA, RWKV_HEADS * LANES), F32)
        w2c = jnp.concatenate([jnp.concatenate([hp(rwkv_w2[lyr, 0]), zl], axis=1),
                               jnp.concatenate([zl, hp(rwkv_w2[lyr, 1])], axis=1)], axis=0)
        a2c = jnp.concatenate([jnp.concatenate([hp(rwkv_a2[lyr, 0]), zl], axis=1),
                               jnp.concatenate([zl, hp(rwkv_a2[lyr, 1])], axis=1)], axis=0)
        w0c = jnp.concatenate([hp(rwkv_w0[lyr, 0]), hp(rwkv_w0[lyr, 1])])[None]
        a0c = jnp.concatenate([hp(rwkv_a0[lyr, 0]), hp(rwkv_a0[lyr, 1])])[None]
        feats = _rwkv_features(pr, mu_p, w0c, w2c.astype(BF16), a0c, a2c.astype(BF16),
                               hp(rwkv_g2[lyr]).astype(BF16), hp(rwkv_k_k[lyr])[None],
                               hp(rwkv_k_a[lyr])[None], hp(rwkv_r_k[lyr].reshape(-1))[None],
                               l_tot // tm_feat, ctx_len // tm_feat, tm_feat)
        f_r, f_v, f_kk, f_g, f_bonus, f_lw, f_k, f_b = feats
        o_scan = _rwkv_scan(f_r, f_v, f_kk, f_lw, f_k, f_b, bsz, ctx_len // RWKV_CHUNK)
        yb = _rwkv_out(o_scan, f_bonus, f_g, hp(rwkv_ln_g[lyr])[None], hp(rwkv_ln_b[lyr])[None], tm)

        pad_g = lambda g: jnp.pad(g, (0, LANES - g.shape[0]))[None]
        qc, kc, vc = _gqa_prep(pg, pad_g(gqa_q_norm[lyr]), pad_g(gqa_k_norm[lyr]), cos_h, sin_h,
                               nb, tm, n_q=GQA_HEADS, n_kv=GQA_KV_HEADS, hd=GQA_HEAD,
                               scale=GQA_SCALE, qk_norm=True, transposed=True)
        yc = _attention(qc, r3(kc), vc, ctx_len, n_q=GQA_HEADS, n_kv=GQA_KV_HEADS, tq=256)

        ones_g = jnp.ones((1, LANES), F32)
        qd, kd, vd = _gqa_prep(pw, ones_g, ones_g, cos_h, sin_h, nb, tm, n_q=WIN_HEADS,
                               n_kv=WIN_KV_HEADS, hd=WIN_HEAD, scale=WIN_SCALE, qk_norm=False,
                               transposed=False)
        yd = _window_attention(r3(qd), r3(kd), r3(vd), win_sink[lyr], ctx_len,
                               n_q=WIN_HEADS, n_kv=WIN_KV_HEADS)

        wb = jnp.stack([_pad_head_rows(w_branch[lyr, i], 8, 64) for i in range(N_BRANCH)]).astype(BF16)
        x_mid = _merge(ya.reshape(n, -1), yb, yc.reshape(n, -1), yd.reshape(n, -1), gate, xs, gt1,
                       wb, w_out[lyr].astype(BF16), ln1_g[lyr][None], ln1_b[lyr][None],
                       nb, nctx_b, tm, alpha)

        keys = peer_keys[lyr]
        half = PEER_DQ // 2
        k1 = jnp.pad(keys[:, 0], [(0, 0), (0, 0), (0, LANES - half)]).astype(BF16)
        k2 = jnp.pad(keys[:, 1], [(0, 0), (0, 0), (LANES - half, 0)]).astype(BF16)
        h_in, idx_t, wgt_t, off_t = _peer_topk(x_mid, mod2, peer_wq[lyr].astype(BF16), k1, k2, nb, nctx_b, tm)
        idx = idx_t.reshape(ne, n).T
        wgt = wgt_t.reshape(ne, n).T
        off_flat = off_t.reshape(ne // 2, n).T.reshape(-1)
        par = (idx & 1).astype(F32)
        tb = 64
        cw = _peer_u(off_flat, h_in.reshape(n, 8, LANES), par, wgt, e16, g16,
                     _pack_table(peer_u[lyr]), tb)
        ffn = _peer_v(off_flat, cw, par, e16, _pack_table(peer_v[lyr]), tb).reshape(n, d)
        xs = _ln_res(x_mid, ffn, gt2, ln2_g[lyr][None], ln2_b[lyr][None], nb, nctx_b, tm, alpha)
        stages.append(dict(ya=ya, yb=yb, yc=yc, yd=yd, x_mid=x_mid, idx=idx, wgt=wgt, ffn=ffn,
                           x_out=xs))

    return xs.reshape(bsz, l_tot, d)[:, ctx_len:, :], stages


def kernel(x, c, ctx, c_ctx, ada_w, ada_b, w_in, mla_q_norm, mla_kv_norm, mla_w_uq, mla_w_ukv,
           rwkv_mu, rwkv_w0, rwkv_w2, rwkv_a0, rwkv_a2, rwkv_g2, rwkv_k_k, rwkv_k_a, rwkv_r_k,
           rwkv_ln_g, rwkv_ln_b, gqa_q_norm, gqa_k_norm, win_sink, w_branch, w_out, ln1_g, ln1_b,
           peer_wq, peer_keys, peer_u, peer_v, ln2_g, ln2_b):
    out, _ = _forward(x, c, ctx, c_ctx, ada_w, ada_b, w_in, mla_q_norm, mla_kv_norm, mla_w_uq,
                      mla_w_ukv, rwkv_mu, rwkv_w0, rwkv_w2, rwkv_a0, rwkv_a2, rwkv_g2, rwkv_k_k,
                      rwkv_k_a, rwkv_r_k, rwkv_ln_g, rwkv_ln_b, gqa_q_norm, gqa_k_norm, win_sink,
                      w_branch, w_out, ln1_g, ln1_b, peer_wq, peer_keys, peer_u, peer_v, ln2_g,
                      ln2_b)
    return out
```

```python
import functools
import math

import jax
import jax.numpy as jnp
from jax import lax
from jax.experimental import pallas as pl
from jax.experimental.pallas import tpu as pltpu

F32 = jnp.float32
BF16 = jnp.bfloat16
I32 = jnp.int32
U32 = jnp.uint32

LANES = 128
GRID_W = 64
ROPE_THETA = 10000.0
NEG_INF = -1e30
LN_EPS = 1e-5
RMS_EPS = 1e-6

MLA_HEADS, MLA_Q_RANK, MLA_KV_RANK, MLA_NOPE, MLA_ROPE, MLA_V = 8, 256, 128, 64, 32, 64
MLA_SCALE = (MLA_NOPE + MLA_ROPE) ** -0.5
RWKV_HEADS, RWKV_HEAD = 8, 64
RWKV_W = RWKV_HEADS * RWKV_HEAD
RWKV_W_LORA, RWKV_A_LORA, RWKV_G_LORA = 64, 64, 128
RWKV_GN_EPS = 64e-5
RWKV_CHUNK = 128
GQA_HEADS, GQA_KV_HEADS, GQA_HEAD = 8, 2, 64
GQA_SCALE = GQA_HEAD ** -0.5
WIN_HEADS, WIN_KV_HEADS, WIN_HEAD, WINDOW = 8, 2, 64, 128
WIN_SCALE = WIN_HEAD ** -0.5
PEER_HEADS, PEER_N_KEYS, PEER_TOPK, PEER_DQ = 8, 128, 16, 128
N_BRANCH = 4

VMEM_LIMIT = 56 * 1024 * 1024


def _cparams(sem, vmem=None):
    return pltpu.CompilerParams(dimension_semantics=sem, vmem_limit_bytes=vmem or VMEM_LIMIT)


def _pick_tile(n, cap, mult=LANES):
    best = mult
    for t in range(mult, min(n, cap) + 1, mult):
        if n % t == 0:
            best = t
    return best


def _dot(a, b):
    return jnp.dot(a, b, preferred_element_type=F32)


def _dot_nt(a, b):
    return lax.dot_general(a, b, (((1,), (1,)), ((), ())), preferred_element_type=F32)


def _dot_tn(a, b):
    return lax.dot_general(a, b, (((0,), (0,)), ((), ())), preferred_element_type=F32)


ONES_LANE = 64


def _with_ones_lane(v):
    lane = lax.broadcasted_iota(I32, v.shape, v.ndim - 1)
    return jnp.where((lane & (LANES - 1)) == ONES_LANE, 1.0, v)


def _heads_t(x):
    return jnp.concatenate([x[:, h * LANES:(h + 1) * LANES].T for h in range(x.shape[1] // LANES)],
                           axis=0)


def _layer_norm(y, g, b):
    mu = jnp.mean(y, axis=-1, keepdims=True)
    d = y - mu
    var = jnp.mean(d * d, axis=-1, keepdims=True)
    return d * lax.rsqrt(var + LN_EPS) * g + b


def _swap_pairs(x):
    n = x.shape[-1]
    lane = lax.broadcasted_iota(I32, x.shape, x.ndim - 1)
    nxt = pltpu.roll(x, n - 1, x.ndim - 1)
    prv = pltpu.roll(x, 1, x.ndim - 1)
    return jnp.where((lane & 1) == 0, nxt, prv)


def _rope(x, cos, sin_signed):
    return x * cos + _swap_pairs(x) * sin_signed


def _ada_kernel(c_ref, w_ref, b_ref, o_ref):
    c = c_ref[...]
    s = c * jax.nn.sigmoid(c)
    o_ref[...] = _dot(s.astype(BF16), w_ref[...].astype(BF16)) + b_ref[...]


def _ada(cvec, w, b):
    m, k = cvec.shape
    nc = w.shape[1]
    tn = _pick_tile(nc, 1536)
    return pl.pallas_call(
        _ada_kernel,
        out_shape=jax.ShapeDtypeStruct((m, nc), F32),
        grid=(nc // tn,),
        in_specs=[pl.BlockSpec((m, k), lambda j: (0, 0)),
                  pl.BlockSpec((k, tn), lambda j: (0, j)),
                  pl.BlockSpec((1, tn), lambda j: (0, j))],
        out_specs=pl.BlockSpec((m, tn), lambda j: (0, j)),
        compiler_params=_cparams(("parallel",)),
        name="ada_mod",
    )(cvec, w, b.reshape(1, nc))


def _group_index(i, nb, nctx_b):
    return (i // nb) * 2 + (i % nb >= nctx_b).astype(I32)


def _mm_kernel(x_ref, mod_ref, w_ref, o_ref, *, sub, tm, nb, nctx_b):
    i = pl.program_id(1)
    parts = []
    for u in range(sub):
        m = mod_ref[_group_index(i * sub + u, nb, nctx_b)]
        xm = x_ref[u * tm:(u + 1) * tm, :] * m[0:1, :] + m[1:2, :]
        parts.append(xm.astype(BF16))
    o_ref[...] = _dot(jnp.concatenate(parts, axis=0), w_ref[...]).astype(o_ref.dtype)


def _modmm(x, mod, w, nb, nctx_b, tm, out_dtype=F32):
    n, k = x.shape
    nc = w.shape[1]
    tn = _pick_tile(nc, 2048)
    sub = max(s for s in (4, 2, 1) if (n // tm) % s == 0)
    kern = functools.partial(_mm_kernel, sub=sub, tm=tm, nb=nb, nctx_b=nctx_b)
    return pl.pallas_call(
        kern,
        out_shape=jax.ShapeDtypeStruct((n, nc), out_dtype),
        grid=(nc // tn, n // (tm * sub)),
        in_specs=[pl.BlockSpec((tm * sub, k), lambda j, i: (i, 0)),
                  pl.BlockSpec(mod.shape, lambda j, i: (0, 0, 0)),
                  pl.BlockSpec((k, tn), lambda j, i: (0, j))],
        out_specs=pl.BlockSpec((tm * sub, tn), lambda j, i: (i, j)),
        compiler_params=_cparams(("parallel", "parallel")),
        name="in_proj",
    )(x, mod, w)


def _mla_prep_kernel(p_ref, qn_ref, kvn_ref, wq_ref, wk_ref, wv_ref, cos_ref, sin_ref,
                     qt_ref, k_ref, vt_ref):
    p = p_ref[...]
    dq = p[:, 0:MLA_Q_RANK]
    dkv = p[:, MLA_Q_RANK:MLA_Q_RANK + MLA_KV_RANK]
    krp = p[:, MLA_Q_RANK + MLA_KV_RANK:]
    qn = dq * lax.rsqrt(jnp.mean(dq * dq, axis=-1, keepdims=True) + RMS_EPS) * qn_ref[...]
    kvn = dkv * lax.rsqrt(jnp.mean(dkv * dkv, axis=-1, keepdims=True) + RMS_EPS) * kvn_ref[...]
    kvn = kvn.astype(BF16)
    cos = cos_ref[...]
    sin = sin_ref[...]
    cos_h = jnp.concatenate([cos] * MLA_HEADS, axis=1)
    sin_h = jnp.concatenate([sin] * MLA_HEADS, axis=1)
    q = _rope(_dot(qn.astype(BF16), wq_ref[...]), cos_h, sin_h) * MLA_SCALE
    kr = _rope(krp, cos, sin)
    k = _dot(kvn, wk_ref[...]) + jnp.concatenate([kr] * MLA_HEADS, axis=1)
    qt_ref[0] = _heads_t(q).astype(BF16)
    k_ref[...] = k.astype(BF16)
    vt_ref[0, 0] = _heads_t(_with_ones_lane(_dot(kvn, wv_ref[...]))).astype(BF16)


def _mla_prep(pm, qn, kvn, wq, wk, wv, cos, sin, nb, tm):
    n = pm.shape[0]
    hw = MLA_HEADS * LANES
    bsz = n // (nb * tm)
    full = lambda a: pl.BlockSpec(a.shape, lambda i: (0,) * a.ndim)
    return pl.pallas_call(
        _mla_prep_kernel,
        out_shape=(jax.ShapeDtypeStruct((bsz, hw, nb * tm), BF16),
                   jax.ShapeDtypeStruct((n, hw), BF16),
                   jax.ShapeDtypeStruct((bsz, nb, hw, tm), BF16)),
        grid=(n // tm,),
        in_specs=[pl.BlockSpec((tm, pm.shape[1]), lambda i: (i, 0)),
                  full(qn), full(kvn), full(wq), full(wk), full(wv),
                  pl.BlockSpec((tm, LANES), lambda i: (i % nb, 0)),
                  pl.BlockSpec((tm, LANES), lambda i: (i % nb, 0))],
        out_specs=(pl.BlockSpec((1, hw, tm), lambda i: (i // nb, 0, i % nb)),
                   pl.BlockSpec((tm, hw), lambda i: (i, 0)),
                   pl.BlockSpec((1, 1, hw, tm), lambda i: (i // nb, i % nb, 0, 0))),
        compiler_params=_cparams(("parallel",)),
        name="mla_prep",
    )(pm, qn, kvn, wq, wk, wv, cos, sin)


def _gqa_prep_kernel(p_ref, qn_ref, kn_ref, cos_ref, sin_ref, q_ref, k_ref, v_ref,
                     *, n_q, n_kv, hd, scale, qk_norm, transposed):
    p = p_ref[...].astype(F32)
    cos = cos_ref[...]
    sin = sin_ref[...]

    def head(j, gain):
        x = p[:, j * LANES:(j + 1) * LANES]
        if qk_norm:
            ms = jnp.sum(x * x, axis=-1, keepdims=True) * (1.0 / hd)
            x = x * lax.rsqrt(ms + RMS_EPS) * gain
        return _rope(x, cos, sin)

    qg = qn_ref[...]
    kg = kn_ref[...]
    q = jnp.concatenate([head(j, qg) for j in range(n_q)], axis=1) * scale
    k = jnp.concatenate([head(n_q + j, kg) for j in range(n_kv)], axis=1)
    v = _with_ones_lane(p[:, (n_q + n_kv) * LANES:])
    k_ref[...] = k.astype(BF16)
    if transposed:
        q_ref[0] = _heads_t(q).astype(BF16)
        v_ref[0, 0] = _heads_t(v).astype(BF16)
    else:
        q_ref[...] = q.astype(BF16)
        v_ref[...] = v.astype(BF16)


def _gqa_prep(pg, qn, kn, cos, sin, nb, tm, *, n_q, n_kv, hd, scale, qk_norm, transposed):
    n = pg.shape[0]
    bsz = n // (nb * tm)
    full = lambda a: pl.BlockSpec(a.shape, lambda i: (0,) * a.ndim)
    kern = functools.partial(_gqa_prep_kernel, n_q=n_q, n_kv=n_kv, hd=hd, scale=scale,
                             qk_norm=qk_norm, transposed=transposed)
    if transposed:
        q_shape = jax.ShapeDtypeStruct((bsz, n_q * LANES, nb * tm), BF16)
        v_shape = jax.ShapeDtypeStruct((bsz, nb, n_kv * LANES, tm), BF16)
        q_spec = pl.BlockSpec((1, n_q * LANES, tm), lambda i: (i // nb, 0, i % nb))
        v_spec = pl.BlockSpec((1, 1, n_kv * LANES, tm), lambda i: (i // nb, i % nb, 0, 0))
    else:
        q_shape = jax.ShapeDtypeStruct((n, n_q * LANES), BF16)
        v_shape = jax.ShapeDtypeStruct((n, n_kv * LANES), BF16)
        q_spec = pl.BlockSpec((tm, n_q * LANES), lambda i: (i, 0))
        v_spec = pl.BlockSpec((tm, n_kv * LANES), lambda i: (i, 0))
    return pl.pallas_call(
        kern,
        out_shape=(q_shape, jax.ShapeDtypeStruct((n, n_kv * LANES), BF16), v_shape),
        grid=(n // tm,),
        in_specs=[pl.BlockSpec((tm, pg.shape[1]), lambda i: (i, 0)),
                  full(qn), full(kn),
                  pl.BlockSpec((tm, LANES), lambda i: (i % nb, 0)),
                  pl.BlockSpec((tm, LANES), lambda i: (i % nb, 0))],
        out_specs=(q_spec, pl.BlockSpec((tm, n_kv * LANES), lambda i: (i, 0)), v_spec),
        compiler_params=_cparams(("parallel",)),
        name="gqa_prep",
    )(pg, qn, kn, cos, sin)


ATTN_GROUPS_IN_FLIGHT = 8


def _attn_kernel(qt_ref, k_ref, vt_ref, o_ref, m_s, acc_s, *, n_q, n_kv, tq, tk, nctx_q,
                 nk_ctx, nk_all):
    qi = pl.program_id(1)
    nkb = jnp.where(qi < nctx_q, nk_ctx, nk_all)
    grp = n_q // n_kv
    gs = range(n_kv)
    m_s[...] = jnp.full(m_s.shape, NEG_INF, F32)
    acc_s[...] = jnp.zeros(acc_s.shape, F32)

    def body(kb, carry):
        ks = pl.multiple_of(kb * tk, tk)
        for g0 in range(0, n_kv, ATTN_GROUPS_IN_FLIGHT):
            gb = range(g0, min(g0 + ATTN_GROUPS_IN_FLIGHT, n_kv))
            s, m_old, m_new, p, a, pv = {}, {}, {}, {}, {}, {}
            for g in gb:
                qtg = jnp.concatenate(
                    [qt_ref[0, (g * grp + j) * LANES:(g * grp + j + 1) * LANES, :]
                     for j in range(grp)], axis=1)
                s[g] = _dot(k_ref[0, pl.ds(ks, tk), g * LANES:(g + 1) * LANES], qtg)
            for g in gb:
                m_old[g] = m_s[g]
                m_new[g] = jnp.maximum(m_old[g], jnp.max(s[g], axis=0, keepdims=True))
            for g in gb:
                p[g] = jnp.exp(s[g] - m_new[g]).astype(BF16)
                a[g] = jnp.exp(m_old[g] - m_new[g])
            for g in gb:
                pv[g] = _dot(vt_ref[0, kb, g * LANES:(g + 1) * LANES, :], p[g])
            for g in gb:
                m_s[g] = m_new[g]
                acc_s[g] = a[g] * acc_s[g] + pv[g]
        return carry

    lax.fori_loop(0, nkb, body, 0)
    row = lax.broadcasted_iota(I32, (LANES, 1), 0)
    for g in gs:
        acc = acc_s[g]
        ot = jnp.where(row < ONES_LANE, acc / acc[ONES_LANE:ONES_LANE + 1, :], 0.0)
        for a in range(grp):
            h = g * grp + a
            o_ref[0, :, h * LANES:(h + 1) * LANES] = ot[:, a * tq:(a + 1) * tq].T.astype(o_ref.dtype)


def _attention(qt, k, vt, ctx_len, *, n_q, n_kv, tq):
    b, l, _ = k.shape
    tk = vt.shape[-1]
    rows = (n_q // n_kv) * tq
    kern = functools.partial(_attn_kernel, n_q=n_q, n_kv=n_kv, tq=tq, tk=tk,
                             nctx_q=ctx_len // tq, nk_ctx=ctx_len // tk, nk_all=l // tk)
    return pl.pallas_call(
        kern,
        out_shape=jax.ShapeDtypeStruct((b, l, n_q * LANES), BF16),
        grid=(b, l // tq),
        in_specs=[pl.BlockSpec((1, n_q * LANES, tq), lambda bi, i: (bi, 0, i)),
                  pl.BlockSpec((1, l, n_kv * LANES), lambda bi, i: (bi, 0, 0)),
                  pl.BlockSpec((1, l // tk, n_kv * LANES, tk), lambda bi, i: (bi, 0, 0, 0))],
        out_specs=pl.BlockSpec((1, tq, n_q * LANES), lambda bi, i: (bi, i, 0)),
        scratch_shapes=[pltpu.VMEM((n_kv, 1, rows), F32), pltpu.VMEM((n_kv, LANES, rows), F32)],
        compiler_params=_cparams(("parallel", "arbitrary")),
        name="dense_attn",
    )(qt, k, vt)


def _win_kernel(sink_ref, q_ref, kc_ref, kp_ref, kcur_ref, kn_ref, vc_ref, vp_ref, vcur_ref,
                vn_ref, o_ref, *, n_q, n_kv, nctx_b, nb, ctx_len):
    qi = pl.program_id(1)
    w = WINDOW
    is_lat = (qi >= nctx_b).astype(I32)
    prev_ok = is_lat * (qi - 1 >= nctx_b).astype(I32)
    next_ok = is_lat * (qi + 1 < nb).astype(I32)
    nk = ctx_len + 3 * w
    r = lax.broadcasted_iota(I32, (w, nk), 0)
    c2 = lax.broadcasted_iota(I32, (w, nk), 1)
    c = c2 - ctx_len
    near = jnp.abs(r - (c - w)) <= w
    blk_ok = jnp.where(c < w, prev_ok, jnp.where(c < 2 * w, is_lat, next_ok)) > 0
    valid = jnp.logical_or(c2 < ctx_len, jnp.logical_and(near, blk_ok))
    bias = jnp.where(valid, 0.0, NEG_INF).astype(F32)
    grp = n_q // n_kv
    bias = jnp.concatenate([bias] * grp, axis=0)
    for g in range(n_kv):
        ls = slice(g * LANES, (g + 1) * LANES)
        qg = jnp.concatenate(
            [q_ref[0, :, (g * grp + a) * LANES:(g * grp + a + 1) * LANES] for a in range(grp)],
            axis=0)
        kcat = jnp.concatenate([kc_ref[0, :, ls], kp_ref[0, :, ls], kcur_ref[0, :, ls],
                                kn_ref[0, :, ls]], axis=0)
        vcat = jnp.concatenate([vc_ref[0, :, ls], vp_ref[0, :, ls], vcur_ref[0, :, ls],
                                vn_ref[0, :, ls]], axis=0)
        s = _dot_nt(qg, kcat) + bias
        sk = jnp.concatenate([jnp.full((w, 1), sink_ref[g * grp + a], F32) for a in range(grp)],
                             axis=0)
        m = jnp.maximum(jnp.max(s, axis=1, keepdims=True), sk)
        p = jnp.exp(s - m)
        den = jnp.sum(p, axis=1, keepdims=True) + jnp.exp(sk - m)
        o = _dot(p.astype(BF16), vcat) / den
        for a in range(grp):
            h = g * grp + a
            o_ref[0, :, h * LANES:(h + 1) * LANES] = o[a * w:(a + 1) * w].astype(o_ref.dtype)


def _window_attention(q, k, v, sink, ctx_len, *, n_q, n_kv):
    b, l, _ = q.shape
    w = WINDOW
    nb = l // w
    nctx_b = ctx_len // w
    kw = n_kv * LANES
    kern = functools.partial(_win_kernel, n_q=n_q, n_kv=n_kv, nctx_b=nctx_b, nb=nb,
                             ctx_len=ctx_len)
    ctx_spec = pl.BlockSpec((1, ctx_len, kw), lambda bi, i: (bi, 0, 0))
    prev_spec = pl.BlockSpec((1, w, kw), lambda bi, i: (bi, jnp.maximum(i - 1, 0), 0))
    cur_spec = pl.BlockSpec((1, w, kw), lambda bi, i: (bi, i, 0))
    next_spec = pl.BlockSpec((1, w, kw), lambda bi, i: (bi, jnp.minimum(i + 1, nb - 1), 0))
    return pl.pallas_call(
        kern,
        out_shape=jax.ShapeDtypeStruct((b, l, n_q * LANES), BF16),
        grid=(b, nb),
        in_specs=[pl.BlockSpec(memory_space=pltpu.SMEM),
                  pl.BlockSpec((1, w, n_q * LANES), lambda bi, i: (bi, i, 0)),
                  ctx_spec, prev_spec, cur_spec, next_spec,
                  ctx_spec, prev_spec, cur_spec, next_spec],
        out_specs=pl.BlockSpec((1, w, n_q * LANES), lambda bi, i: (bi, i, 0)),
        compiler_params=_cparams(("parallel", "arbitrary")),
        name="window_attn",
    )(sink, q, k, k, k, k, v, v, v, v)


def _rwkv_feat_kernel(p_ref, hp_ref, hn_ref, mu_ref, w0_ref, w2_ref, a0_ref, a2_ref, g2_ref,
                      kk_ref, ka_ref, rk_ref,
                      r_out, v_out, kk_out, g_out, bonus_out, lw_out, k_out, b_out,
                      *, tm, nb, nctx_b):
    i = pl.program_id(0) % nb
    seq_start = jnp.logical_or(i == 0, i == nctx_b)
    seq_end = jnp.logical_or(i == nctx_b - 1, i == nb - 1)
    p = p_ref[...]
    row = lax.broadcasted_iota(I32, p.shape, 0)
    first = jnp.where(seq_start, 0.0, hp_ref[7:8, :])
    last = jnp.where(seq_end, 0.0, hn_ref[0:1, :])
    prev = jnp.where(row == 0, first, pltpu.roll(p, 1, 0))
    nxt = jnp.where(row == tm - 1, last, pltpu.roll(p, tm - 1, 0))
    mu = mu_ref[...]
    ps = p + mu[0:1, :] * (prev - p) + mu[1:2, :] * (nxt - p)
    hw = RWKV_HEADS * LANES
    r = ps[:, 0:hw]
    k = ps[:, hw:2 * hw]
    v = ps[:, 2 * hw:3 * hw]
    wfb = ps[:, 3 * hw:3 * hw + LANES]
    afb = ps[:, 3 * hw + LANES:3 * hw + 2 * LANES]
    gi = ps[:, 3 * hw + 2 * LANES:]
    kkr = k * kk_ref[...]
    parts = []
    for h in range(RWKV_HEADS):
        x = kkr[:, h * LANES:(h + 1) * LANES]
        nrm = jnp.sqrt(jnp.sum(x * x, axis=-1, keepdims=True))
        parts.append(x / jnp.maximum(nrm, 1e-12))
    kk = jnp.concatenate(parts, axis=1)
    z = w0_ref[...] + _dot(jnp.tanh(wfb).astype(BF16), w2_ref[...])
    lw = -math.exp(-0.5) * jax.nn.sigmoid(z)
    a = jax.nn.sigmoid(a0_ref[...] + _dot(afb.astype(BF16), a2_ref[...]))
    ka = ka_ref[...]
    k0 = k * (1.0 + (a[:, 0:hw] - 1.0) * ka)
    k1 = k * (1.0 + (a[:, hw:] - 1.0) * ka)
    rkk = r * (k0 + k1) * rk_ref[...]
    bparts = []
    for h in range(RWKV_HEADS):
        sl = slice(h * LANES, (h + 1) * LANES)
        bparts.append(jnp.sum(rkk[:, sl], axis=-1, keepdims=True) * v[:, sl])
    r_out[...] = r.astype(r_out.dtype)
    v_out[...] = v.astype(v_out.dtype)
    kk_out[...] = kk.astype(kk_out.dtype)
    g_out[...] = _dot(jax.nn.sigmoid(gi).astype(BF16), g2_ref[...]).astype(g_out.dtype)
    bonus_out[...] = jnp.concatenate(bparts, axis=1).astype(bonus_out.dtype)
    lw_out[0] = lw[:, 0:hw]
    lw_out[1] = lw[:, hw:]
    k_out[0] = k0.astype(k_out.dtype)
    k_out[1] = k1.astype(k_out.dtype)
    b_out[0] = (a[:, 0:hw] * kk).astype(b_out.dtype)
    b_out[1] = (a[:, hw:] * kk).astype(b_out.dtype)


def _rwkv_features(pr, mu, w0, w2, a0, a2, g2, kk, ka, rk, nb, nctx_b, tm):
    n, wid = pr.shape
    hw = RWKV_HEADS * LANES
    full = lambda a: pl.BlockSpec(a.shape, lambda i: (0,) * a.ndim)
    kern = functools.partial(_rwkv_feat_kernel, tm=tm, nb=nb, nctx_b=nctx_b)
    one = jax.ShapeDtypeStruct((n, hw), BF16)
    two = jax.ShapeDtypeStruct((2, n, hw), BF16)
    two_f32 = jax.ShapeDtypeStruct((2, n, hw), F32)
    s1 = pl.BlockSpec((tm, hw), lambda i: (i, 0))
    s2 = pl.BlockSpec((2, tm, hw), lambda i: (0, i, 0))
    t8 = tm // 8
    return pl.pallas_call(
        kern,
        out_shape=(one, one, one, one, one, two_f32, two, two),
        grid=(n // tm,),
        in_specs=[pl.BlockSpec((tm, wid), lambda i: (i, 0)),
                  pl.BlockSpec((8, wid), lambda i: (jnp.maximum(i * t8 - 1, 0), 0)),
                  pl.BlockSpec((8, wid), lambda i: (jnp.minimum((i + 1) * t8, n // 8 - 1), 0)),
                  full(mu), full(w0), full(w2), full(a0), full(a2), full(g2),
                  full(kk), full(ka), full(rk)],
        out_specs=(s1, s1, s1, s1, s1, s2, s2, s2),
        compiler_params=_cparams(("parallel",)),
        name="rwkv_features",
    )(pr, pr, pr, mu, w0, w2, a0, a2, g2, kk, ka, rk)


def _split3_dot(mask_bf16, x):
    x1 = x.astype(BF16)
    r1 = x - x1.astype(F32)
    x2 = r1.astype(BF16)
    x3 = (r1 - x2.astype(F32)).astype(BF16)
    return _dot(mask_bf16, x1) + _dot(mask_bf16, x2) + _dot(mask_bf16, x3)


def _rwkv_scan_kernel(r_ref, v_ref, kk_ref, lw_ref, k_ref, b_ref, o_ref, s_ref):
    d = pl.program_id(1)
    c = pl.program_id(2)
    cs = RWKV_CHUNK

    @pl.when(c == 0)
    def _():
        s_ref[...] = jnp.zeros_like(s_ref)

    rev = d == 1
    t_i = lax.broadcasted_iota(I32, (cs, cs), 0)
    s_i = lax.broadcasted_iota(I32, (cs, cs), 1)
    order = jnp.where(rev, t_i - s_i, s_i - t_i)
    incl = order <= 0
    strict = order < 0
    eye = jnp.where(t_i == s_i, 1.0, 0.0).astype(F32)
    off_masks = []
    for lvl in range(int(math.log2(cs))):
        pair = (t_i >> (lvl + 1)) == (s_i >> (lvl + 1))
        half = (t_i >> lvl) != (s_i >> lvl)
        off = jnp.logical_and(jnp.logical_and(pair, half), strict)
        off_masks.append(jnp.where(off, 1.0, 0.0).astype(F32))
    lw = lw_ref[0]
    cum = _split3_dot(jnp.where(incl, 1.0, 0.0).astype(BF16), lw)
    p_in = jnp.exp(cum)
    p_inv = jnp.exp(-cum)
    p_ex = jnp.exp(cum - lw)
    tot = jnp.where(rev, cum[0:1, :], cum[cs - 1:cs, :])
    p_all = jnp.exp(tot)
    a_t = -kk_ref[...].astype(F32) * p_ex
    r_t = r_ref[...].astype(F32) * p_in
    k_t = k_ref[0].astype(F32) * p_inv
    b_t = b_ref[0].astype(F32) * p_inv
    v = v_ref[...].astype(F32)
    hs = range(RWKV_HEADS)
    sls = [slice(h * LANES, (h + 1) * LANES) for h in hs]
    vb = [v[:, sl].astype(BF16) for sl in sls]
    bk = [jnp.concatenate([b_t[:, sl], k_t[:, sl]], axis=0).astype(BF16) for sl in sls]
    gm = [_dot_nt(jnp.concatenate([a_t[:, sls[h]], r_t[:, sls[h]]], axis=0).astype(BF16), bk[h])
          for h in hs]
    m_ab = [jnp.where(strict, g[0:cs, 0:cs], 0.0) for g in gm]
    mkv = [_dot(jnp.where(strict, gm[h][0:cs, cs:], 0.0).astype(BF16), vb[h]) for h in hs]
    x = [eye + m * off_masks[0] for m in m_ab]
    for lvl in range(1, len(off_masks)):
        xb = [xx.astype(BF16) for xx in x]
        t1 = [_dot(xb[h], (m_ab[h] * off_masks[lvl]).astype(BF16)).astype(BF16) for h in hs]
        x = [x[h] + _dot(t1[h], xb[h]) for h in hs]
    z = [_dot(x[h].astype(BF16),
              jnp.concatenate([a_t[:, sls[h]], mkv[h]], axis=1).astype(BF16)) for h in hs]
    zb = [zz.astype(BF16) for zz in z]
    gy = [_dot(jnp.where(incl, gm[h][cs:, 0:cs], 0.0).astype(BF16), zb[h]) for h in hs]
    y0 = [_dot(jnp.where(incl, gm[h][cs:, cs:], 0.0).astype(BF16), vb[h]) + gy[h][:, LANES:]
          for h in hs]
    s0 = [s_ref[h] for h in hs]
    s0b = [s.astype(BF16) for s in s0]
    u = [_dot_nt(zb[h][:, 0:LANES], s0b[h]) + z[h][:, LANES:] for h in hs]
    for h in hs:
        g_mat = r_t[:, sls[h]] + gy[h][:, 0:LANES]
        o_ref[0, :, sls[h]] = _dot_nt(g_mat.astype(BF16), s0b[h]) + y0[h]
    for h in hs:
        uv = jnp.concatenate([u[h], v[:, sls[h]]], axis=0).astype(BF16)
        s_ref[h] = (s0[h] + _dot_tn(uv, bk[h])) * p_all[:, sls[h]]


def _rwkv_scan(r, v, kk, lw, k, bb, bsz, nctx_c):
    n, hw = r.shape
    cs = RWKV_CHUNK
    nc = n // bsz // cs

    def blk(bi, d, c):
        rc = jnp.where(c < nctx_c, nctx_c - 1 - c, nc - 1 - (c - nctx_c))
        return bi * nc + jnp.where(d == 0, c, rc)

    s1 = pl.BlockSpec((cs, hw), lambda bi, d, c: (blk(bi, d, c), 0))
    s2 = pl.BlockSpec((1, cs, hw), lambda bi, d, c: (d, blk(bi, d, c), 0))
    return pl.pallas_call(
        _rwkv_scan_kernel,
        out_shape=jax.ShapeDtypeStruct((2, n, hw), F32),
        grid=(bsz, 2, nc),
        in_specs=[s1, s1, s1, s2, s2, s2],
        out_specs=s2,
        scratch_shapes=[pltpu.VMEM((RWKV_HEADS, LANES, LANES), F32)],
        compiler_params=_cparams(("parallel", "parallel", "arbitrary")),
        name="rwkv_scan",
    )(r, v, kk, lw, k, bb)


def _rwkv_out_kernel(o_ref, bonus_ref, g_ref, lng_ref, lnb_ref, y_ref):
    o = o_ref[0] + o_ref[1]
    lane = lax.broadcasted_iota(I32, (1, LANES), 1)
    real = lane < RWKV_HEAD
    lng = lng_ref[...]
    lnb = lnb_ref[...]
    parts = []
    for h in range(RWKV_HEADS):
        sl = slice(h * LANES, (h + 1) * LANES)
        x = o[:, sl]
        mu = jnp.sum(x, axis=-1, keepdims=True) * (1.0 / RWKV_HEAD)
        dlt = jnp.where(real, x - mu, 0.0)
        var = jnp.sum(dlt * dlt, axis=-1, keepdims=True) * (1.0 / RWKV_HEAD)
        parts.append(dlt * lax.rsqrt(var + RWKV_GN_EPS) * lng[:, sl] + lnb[:, sl])
    y = (jnp.concatenate(parts, axis=1) + bonus_ref[...].astype(F32)) * g_ref[...].astype(F32)
    y_ref[...] = y.astype(y_ref.dtype)


def _rwkv_out(o, bonus, g, lng, lnb, tm):
    _, n, hw = o.shape
    full = lambda a: pl.BlockSpec(a.shape, lambda i: (0,) * a.ndim)
    s1 = pl.BlockSpec((tm, hw), lambda i: (i, 0))
    return pl.pallas_call(
        _rwkv_out_kernel,
        out_shape=jax.ShapeDtypeStruct((n, hw), BF16),
        grid=(n // tm,),
        in_specs=[pl.BlockSpec((2, tm, hw), lambda i: (0, i, 0)), s1, s1, full(lng), full(lnb)],
        out_specs=s1,
        compiler_params=_cparams(("parallel",)),
        name="rwkv_out",
    )(o, bonus, g, lng, lnb)


def _merge_kernel(ya_ref, yb_ref, yc_ref, yd_ref, gate_ref, x_ref, gt_ref, wb_ref, wo_ref,
                  lng_ref, lnb_ref, o_ref, *, alpha, d):
    ys = (ya_ref, yb_ref, yc_ref, yd_ref)
    acc = None
    for i in range(N_BRANCH):
        gate = jax.nn.sigmoid(gate_ref[:, i * d:(i + 1) * d].astype(F32))
        term = gate * _dot(ys[i][...], wb_ref[i])
        acc = term if acc is None else acc + term
    mix = _dot(acc.astype(BF16), wo_ref[...])
    y = alpha * x_ref[...] + gt_ref[0] * mix
    o_ref[...] = _layer_norm(y, lng_ref[...], lnb_ref[...])


def _merge(ya, yb, yc, yd, gate, x, gt, wb, wo, lng, lnb, nb, nctx_b, tm, alpha):
    n, d = x.shape
    hw = ya.shape[1]
    full = lambda a: pl.BlockSpec(a.shape, lambda i: (0,) * a.ndim)
    sy = pl.BlockSpec((tm, hw), lambda i: (i, 0))
    kern = functools.partial(_merge_kernel, alpha=alpha, d=d)
    return pl.pallas_call(
        kern,
        out_shape=jax.ShapeDtypeStruct((n, d), F32),
        grid=(n // tm,),
        in_specs=[sy, sy, sy, sy,
                  pl.BlockSpec((tm, N_BRANCH * d), lambda i: (i, 0)),
                  pl.BlockSpec((tm, d), lambda i: (i, 0)),
                  pl.BlockSpec((1, 1, d), lambda i: (_group_index(i, nb, nctx_b), 0, 0)),
                  full(wb), full(wo), full(lng), full(lnb)],
        out_specs=pl.BlockSpec((tm, d), lambda i: (i, 0)),
        compiler_params=_cparams(("parallel",)),
        name="merge",
    )(ya, yb, yc, yd, gate, x, gt, wb, wo, lng, lnb)


def _extract_topk(srcs, n_rows, k, val_refs, pos_refs):
    rio = lax.broadcasted_iota(I32, srcs[0].shape, 0).astype(F32)
    js = range(len(srcs))
    for rnk in range(k):
        m = [jnp.max(s, axis=0, keepdims=True) for s in srcs]
        pos = [jnp.min(jnp.where(srcs[j] == m[j], rio, float(n_rows)), axis=0, keepdims=True)
               for j in js]
        for j in js:
            val_refs[j][rnk:rnk + 1, :] = m[j]
            pos_refs[j][rnk:rnk + 1, :] = pos[j]
        srcs = [jnp.where(rio == pos[j], -jnp.inf, srcs[j]) for j in js]


PEER_CAND_COUNTS = tuple(PEER_TOPK // (a + 1) for a in range(PEER_TOPK))
PEER_N_CAND = sum(PEER_CAND_COUNTS)
PEER_CAND_ROWS = -(-PEER_N_CAND // 8) * 8


def _peer_topk_kernel(x_ref, mod_ref, wq_ref, k1_ref, k2_ref, h_ref, idx_ref, wgt_ref, off_ref,
                      q_s, v1_s, i1_s, v2_s, i2_s, cand_s, cidx_s, best_s, pos_s):
    hd = pl.program_id(1)

    @pl.when(hd == 0)
    def _():
        m = mod_ref[0]
        hh = x_ref[...] * m[0:1, :] + m[1:2, :]
        h_ref[...] = hh
        q = _dot(hh.astype(BF16), wq_ref[...])
        for a in range(PEER_HEADS):
            q_s[a] = q[:, a * LANES:(a + 1) * LANES].astype(BF16)

    qh = q_s[hd]
    _extract_topk([_dot_nt(k1_ref[0], qh), _dot_nt(k2_ref[0], qh)],
                  PEER_N_KEYS, PEER_TOPK, [v1_s, v2_s], [i1_s, i2_s])
    row = 0
    for a, cnt in enumerate(PEER_CAND_COUNTS):
        cand_s[row:row + cnt, :] = v1_s[a:a + 1, :] + v2_s[0:cnt, :]
        cidx_s[row:row + cnt, :] = i1_s[a:a + 1, :] * float(PEER_N_KEYS) + i2_s[0:cnt, :]
        row += cnt
    pad = PEER_CAND_ROWS - PEER_N_CAND
    if pad:
        cand_s[PEER_N_CAND:, :] = jnp.full((pad, cand_s.shape[1]), -jnp.inf, F32)
        cidx_s[PEER_N_CAND:, :] = jnp.zeros((pad, cand_s.shape[1]), F32)
    _extract_topk([cand_s[...]], PEER_CAND_ROWS, PEER_TOPK, [best_s], [pos_s])
    cidx = cidx_s[...]
    rio = lax.broadcasted_iota(I32, cidx.shape, 0).astype(F32)
    lo = None
    for rnk in range(PEER_TOPK):
        sel = rio == pos_s[rnk:rnk + 1, :]
        e_id = jnp.max(jnp.where(sel, cidx, -1.0), axis=0, keepdims=True).astype(I32)
        idx_ref[0, rnk:rnk + 1, :] = e_id
        off = (e_id >> 1) * 8
        if rnk % 2 == 0:
            lo = off
        else:
            off_ref[0, rnk // 2:rnk // 2 + 1, :] = lo | (off << 16)
    best = best_s[...]
    e = jnp.exp(best - best[0:1, :])
    wgt_ref[0] = e / jnp.sum(e, axis=0, keepdims=True)


def _peer_topk(x, mod, wq, k1, k2, nb, nctx_b, tm):
    n, d = x.shape
    tk = PEER_TOPK
    full = lambda a: pl.BlockSpec(a.shape, lambda i, h: (0,) * a.ndim)
    return pl.pallas_call(
        _peer_topk_kernel,
        out_shape=(jax.ShapeDtypeStruct((n, d), F32),
                   jax.ShapeDtypeStruct((PEER_HEADS, tk, n), I32),
                   jax.ShapeDtypeStruct((PEER_HEADS, tk, n), F32),
                   jax.ShapeDtypeStruct((PEER_HEADS, tk // 2, n), I32)),
        grid=(n // tm, PEER_HEADS),
        in_specs=[pl.BlockSpec((tm, d), lambda i, h: (i, 0)),
                  pl.BlockSpec((1, 2, d), lambda i, h: (_group_index(i, nb, nctx_b), 0, 0)),
                  full(wq),
                  pl.BlockSpec((1, PEER_N_KEYS, LANES), lambda i, h: (h, 0, 0)),
                  pl.BlockSpec((1, PEER_N_KEYS, LANES), lambda i, h: (h, 0, 0))],
        out_specs=(pl.BlockSpec((tm, d), lambda i, h: (i, 0)),
                   pl.BlockSpec((1, tk, tm), lambda i, h: (h, 0, i)),
                   pl.BlockSpec((1, tk, tm), lambda i, h: (h, 0, i)),
                   pl.BlockSpec((1, tk // 2, tm), lambda i, h: (h, 0, i))),
        scratch_shapes=[pltpu.VMEM((PEER_HEADS, tm, LANES), BF16),
                        pltpu.VMEM((tk, tm), F32), pltpu.VMEM((tk, tm), F32),
                        pltpu.VMEM((tk, tm), F32), pltpu.VMEM((tk, tm), F32),
                        pltpu.VMEM((PEER_CAND_ROWS, tm), F32), pltpu.VMEM((PEER_CAND_ROWS, tm), F32),
                        pltpu.VMEM((tk, tm), F32), pltpu.VMEM((tk, tm), F32)],
        compiler_params=_cparams(("parallel", "arbitrary")),
        name="peer_topk",
    )(x, mod, wq, k1, k2)


PEER_NE = PEER_HEADS * PEER_TOPK
PEER_COLS = PEER_NE * 16
PEER_TOK_UNROLL = 16


def _table_spec(tab):
    return pl.BlockSpec(tab.shape, lambda i: (0, 0), pipeline_mode=pl.Buffered(1))


def _gather_view(off_ref, tab_v, t):
    tiles = []
    tok_ref = off_ref.at[pl.ds(t * (PEER_NE // 2), PEER_NE // 2)]
    for j in range(PEER_NE // 2):
        w = tok_ref[j]
        o0 = pl.multiple_of(w & 0xFFFF, 8)
        o1 = pl.multiple_of(lax.shift_right_logical(w, 16), 8)
        tiles.append(tab_v[pl.ds(o0, 8), :])
        tiles.append(tab_v[pl.ds(o1, 8), :])
    return pltpu.bitcast(jnp.concatenate(tiles, axis=0), BF16)


def _select_mask(px_row):
    shp = (8, PEER_COLS)
    row = lax.broadcasted_iota(I32, shp, 0)
    col = lax.broadcasted_iota(I32, shp, 1)
    sub = (col >> 1) & 7
    fixed = jnp.logical_and((col & 1) == (row >> 2), (sub & 3) == (row & 3))
    return jnp.logical_and(fixed, (sub >> 2).astype(F32) == px_row)


def _split2(x):
    x1 = x.astype(BF16)
    return x1, (x - x1.astype(F32)).astype(BF16)


def _peer_u_kernel(off_ref, h_ref, par_ref, wgt_ref, e16_ref, g16_ref, tab_v, c_ref,
                   px_s, d_s, *, tb):
    px_s[...] = _dot(par_ref[...].astype(BF16), e16_ref[...])

    def tokens(i, carry):
        for u in range(PEER_TOK_UNROLL):
            t = i * PEER_TOK_UNROLL + u
            view = _gather_view(off_ref, tab_v, t)
            h1, h2 = _split2(h_ref[t])
            dd = _dot_nt(jnp.concatenate([h1, h2], axis=0), view)
            dd = jnp.where(_select_mask(px_s[pl.ds(t, 1), :]), dd[0:8] + dd[8:16], 0.0)
            d_s[pl.ds(t, 1), :] = jnp.sum(dd, axis=0, keepdims=True)
        return carry

    lax.fori_loop(0, tb // PEER_TOK_UNROLL, tokens, 0)
    d1, d2 = _split2(d_s[...])
    g16 = g16_ref[...]
    act = _dot(d1, g16) + _dot(d2, g16)
    gelu = 0.5 * act * (1.0 + lax.erf(act * (2.0 ** -0.5)))
    c_ref[...] = gelu * wgt_ref[...]


def _peer_u(off_flat, h3, par, wgt, e16, g16, tab, tb):
    n = h3.shape[0]
    full = lambda a: pl.BlockSpec(a.shape, lambda i: (0,) * a.ndim)
    kern = functools.partial(_peer_u_kernel, tb=tb)
    tok = pl.BlockSpec((tb, PEER_NE), lambda i: (i, 0))
    return pl.pallas_call(
        kern,
        out_shape=jax.ShapeDtypeStruct((n, PEER_NE), F32),
        grid=(n // tb,),
        in_specs=[pl.BlockSpec((tb * PEER_NE // 2,), lambda i: (i,), memory_space=pltpu.SMEM),
                  pl.BlockSpec((tb, 8, LANES), lambda i: (i, 0, 0)),
                  tok, tok, full(e16), full(g16), _table_spec(tab)],
        out_specs=tok,
        scratch_shapes=[pltpu.VMEM((tb, PEER_COLS), F32), pltpu.VMEM((tb, PEER_COLS), F32)],
        compiler_params=_cparams(("arbitrary",)),
        name="peer_u",
    )(off_flat, h3, par, wgt, e16, g16, tab)


def _peer_v_kernel(off_ref, c_ref, par_ref, e16_ref, tab_v, o_ref, px_s, c1_s, c2_s, *, tb):
    e16 = e16_ref[...]
    px_s[...] = _dot(par_ref[...].astype(BF16), e16)
    c1, c2 = _split2(c_ref[...])
    c1_s[...] = _dot(c1, e16)
    c2_s[...] = _dot(c2, e16)

    def tokens(i, carry):
        for u in range(PEER_TOK_UNROLL):
            t = i * PEER_TOK_UNROLL + u
            view = _gather_view(off_ref, tab_v, t)
            sel = _select_mask(px_s[pl.ds(t, 1), :])
            lhs = jnp.concatenate([jnp.where(sel, c1_s[pl.ds(t, 1), :], 0.0),
                                   jnp.where(sel, c2_s[pl.ds(t, 1), :], 0.0)], axis=0)
            out = _dot(lhs.astype(BF16), view)
            o_ref[t] = out[0:8] + out[8:16]
        return carry

    lax.fori_loop(0, tb // PEER_TOK_UNROLL, tokens, 0)


def _peer_v(off_flat, cw, par, e16, tab, tb):
    n = cw.shape[0]
    full = lambda a: pl.BlockSpec(a.shape, lambda i: (0,) * a.ndim)
    kern = functools.partial(_peer_v_kernel, tb=tb)
    tok = pl.BlockSpec((tb, PEER_NE), lambda i: (i, 0))
    return pl.pallas_call(
        kern,
        out_shape=jax.ShapeDtypeStruct((n, 8, LANES), F32),
        grid=(n // tb,),
        in_specs=[pl.BlockSpec((tb * PEER_NE // 2,), lambda i: (i,), memory_space=pltpu.SMEM),
                  tok, tok, full(e16), _table_spec(tab)],
        out_specs=pl.BlockSpec((tb, 8, LANES), lambda i: (i, 0, 0)),
        scratch_shapes=[pltpu.VMEM((tb, PEER_COLS), F32), pltpu.VMEM((tb, PEER_COLS), F32),
                        pltpu.VMEM((tb, PEER_COLS), F32)],
        compiler_params=_cparams(("arbitrary",)),
        name="peer_v",
    )(off_flat, cw, par, e16, tab)


def _pack_table(tab):
    e, d = tab.shape
    bits = lax.bitcast_convert_type(tab.astype(BF16), jnp.uint16).astype(U32)
    words = bits[:, :d // 2] | (bits[:, d // 2:] << 16)
    return words.reshape(e * 4, LANES)


def _ln_res_kernel(x_ref, f_ref, gt_ref, lng_ref, lnb_ref, o_ref, *, alpha):
    y = alpha * x_ref[...] + gt_ref[0] * f_ref[...]
    o_ref[...] = _layer_norm(y, lng_ref[...], lnb_ref[...])


def _ln_res(x, f, gt, lng, lnb, nb, nctx_b, tm, alpha):
    n, d = x.shape
    full = lambda a: pl.BlockSpec(a.shape, lambda i: (0,) * a.ndim)
    s1 = pl.BlockSpec((tm, d), lambda i: (i, 0))
    return pl.pallas_call(
        functools.partial(_ln_res_kernel, alpha=alpha),
        out_shape=jax.ShapeDtypeStruct((n, d), F32),
        grid=(n // tm,),
        in_specs=[s1, s1,
                  pl.BlockSpec((1, 1, d), lambda i: (_group_index(i, nb, nctx_b), 0, 0)),
                  full(lng), full(lnb)],
        out_specs=s1,
        compiler_params=_cparams(("parallel",)),
        name="ln_res",
    )(x, f, gt, lng, lnb)


def _pad_heads(w, n_heads, hd):
    lead = w.shape[:-1]
    w = w.reshape(lead + (n_heads, hd))
    w = jnp.pad(w, [(0, 0)] * len(lead) + [(0, 0), (0, LANES - hd)])
    return w.reshape(lead + (n_heads * LANES,))


def _pad_head_rows(w, n_heads, hd):
    d = w.shape[-1]
    w = w.reshape(n_heads, hd, d)
    w = jnp.pad(w, [(0, 0), (0, LANES - hd), (0, 0)])
    return w.reshape(n_heads * LANES, d)


def _rope_tables(rows, rot_dim, lane_off, ctx_len):
    r_idx = jnp.repeat(jnp.arange(rows), GRID_W).astype(F32)
    c_idx = jnp.tile(jnp.arange(GRID_W), rows).astype(F32)
    n = rot_dim // 4
    inv = ROPE_THETA ** (-jnp.arange(n, dtype=F32) / n)
    ang = jnp.concatenate([r_idx[:, None] * inv, c_idx[:, None] * inv], axis=-1)
    cos = jnp.repeat(jnp.cos(ang), 2, axis=-1)
    sin = jnp.repeat(jnp.sin(ang), 2, axis=-1) * jnp.tile(jnp.array([-1.0, 1.0], F32), rot_dim // 2)
    s = ang.shape[0]
    cos_t = jnp.ones((ctx_len + s, LANES), F32).at[ctx_len:, lane_off:lane_off + rot_dim].set(cos)
    sin_t = jnp.zeros((ctx_len + s, LANES), F32).at[ctx_len:, lane_off:lane_off + rot_dim].set(sin)
    return cos_t, sin_t


def _split_cols(w, widths):
    out, start = [], 0
    for wd in widths:
        out.append(w[..., start:start + wd])
        start += wd
    return out


def _forward(x, c, ctx, c_ctx, ada_w, ada_b, w_in, mla_q_norm, mla_kv_norm, mla_w_uq, mla_w_ukv,
             rwkv_mu, rwkv_w0, rwkv_w2, rwkv_a0, rwkv_a2, rwkv_g2, rwkv_k_k, rwkv_k_a, rwkv_r_k,
             rwkv_ln_g, rwkv_ln_b, gqa_q_norm, gqa_k_norm, win_sink, w_branch, w_out, ln1_g, ln1_b,
             peer_wq, peer_keys, peer_u, peer_v, ln2_g, ln2_b):
    stages = []
    bsz, seq, d = x.shape
    ctx_len = ctx.shape[1]
    depth = ada_w.shape[0]
    alpha = (2 * depth) ** 0.25
    l_tot = ctx_len + seq
    n = bsz * l_tot
    tm = 256
    tm_feat = 128
    assert ctx_len % tm == 0 and seq % tm == 0 and seq % GRID_W == 0
    nb = l_tot // tm
    nctx_b = ctx_len // tm
    rows = seq // GRID_W
    ne = PEER_HEADS * PEER_TOPK

    xs = jnp.concatenate([ctx, x], axis=1).reshape(n, d)
    cos_m, sin_m = _rope_tables(rows, MLA_ROPE, MLA_NOPE, ctx_len)
    cos_h, sin_h = _rope_tables(rows, GQA_HEAD, 0, ctx_len)

    m_rows = 16
    cvec = jnp.zeros((m_rows, d), F32).at[:bsz].set(c).at[bsz].set(c_ctx)
    mla_in = MLA_Q_RANK + MLA_KV_RANK + MLA_ROPE
    rwkv_in = 3 * RWKV_W + 2 * RWKV_W_LORA + 2 * RWKV_A_LORA + RWKV_G_LORA
    gqa_in = (GQA_HEADS + 2 * GQA_KV_HEADS) * GQA_HEAD
    win_in = (WIN_HEADS + 2 * WIN_KV_HEADS) * WIN_HEAD
    rw_widths = (RWKV_W, RWKV_W, RWKV_W, RWKV_W_LORA, RWKV_W_LORA, RWKV_A_LORA, RWKV_A_LORA,
                 RWKV_G_LORA)

    col16 = jnp.arange(PEER_COLS)
    e16 = (col16[None, :] // 16 == jnp.arange(ne)[:, None]).astype(BF16)
    g16 = e16.T

    for lyr in range(depth):
        mod = _ada(cvec, ada_w[lyr], ada_b[lyr])
        chunks = [mod[:, i * d:(i + 1) * d] for i in range(6)]

        def table(ch):
            lat = ch[:bsz]
            cx = jnp.broadcast_to(ch[bsz][None], (bsz, d))
            return jnp.stack([cx, lat], axis=1).reshape(bsz * 2, d)

        sh1, sc1, gt1, sh2, sc2, gt2 = [table(ch) for ch in chunks]
        mod1 = jnp.stack([1.0 + sc1, sh1], axis=1)
        mod2 = jnp.stack([1.0 + sc2, sh2], axis=1)
        gt1 = gt1[:, None, :]
        gt2 = gt2[:, None, :]

        wi = w_in[lyr]
        w_mla, w_rw, w_gq, w_wn, w_gate = _split_cols(wi, (mla_in, rwkv_in, gqa_in, win_in, N_BRANCH * d))
        zc = lambda k: jnp.zeros((d, k), F32)
        w_mla_p = jnp.concatenate([w_mla[:, :MLA_Q_RANK + MLA_KV_RANK], zc(MLA_NOPE),
                                   w_mla[:, MLA_Q_RANK + MLA_KV_RANK:], zc(LANES - MLA_NOPE - MLA_ROPE)],
                                  axis=1)
        rr, rk_, rv, rwf, rwb, raf, rab, rgi = _split_cols(w_rw, rw_widths)
        hp = lambda w: _pad_heads(w, RWKV_HEADS, RWKV_HEAD)
        w_rw_p = jnp.concatenate([hp(rr), hp(rk_), hp(rv), rwf, rwb, raf, rab, rgi], axis=1)
        mu_parts = _split_cols(rwkv_mu[lyr], rw_widths)
        mu_p = jnp.concatenate([hp(mu_parts[0]), hp(mu_parts[1]), hp(mu_parts[2])] + mu_parts[3:], axis=1)

        def gqa_cols(w, nq, nkv, hd):
            q_, k_, v_ = _split_cols(w, (nq * hd, nkv * hd, nkv * hd))
            return jnp.concatenate([_pad_heads(q_, nq, hd), _pad_heads(k_, nkv, hd),
                                    _pad_heads(v_, nkv, hd)], axis=1)

        w_gq_p = gqa_cols(w_gq, GQA_HEADS, GQA_KV_HEADS, GQA_HEAD)
        w_wn_p = gqa_cols(w_wn, WIN_HEADS, WIN_KV_HEADS, WIN_HEAD)

        pm = _modmm(xs, mod1, w_mla_p.astype(BF16), nb, nctx_b, tm)
        pr = _modmm(xs, mod1, w_rw_p.astype(BF16), nb, nctx_b, tm)
        pg = _modmm(xs, mod1, w_gq_p.astype(BF16), nb, nctx_b, tm, BF16)
        pw = _modmm(xs, mod1, w_wn_p.astype(BF16), nb, nctx_b, tm, BF16)
        gate = _modmm(xs, mod1, w_gate.astype(BF16), nb, nctx_b, tm, BF16)

        uq = mla_w_uq[lyr].reshape(MLA_Q_RANK, MLA_HEADS, MLA_NOPE + MLA_ROPE)
        uq = jnp.pad(uq, [(0, 0), (0, 0), (0, LANES - MLA_NOPE - MLA_ROPE)]).reshape(MLA_Q_RANK, -1)
        ukv = mla_w_ukv[lyr].reshape(MLA_KV_RANK, MLA_HEADS, MLA_NOPE + MLA_V)
        uk = jnp.pad(ukv[:, :, :MLA_NOPE], [(0, 0), (0, 0), (0, LANES - MLA_NOPE)]).reshape(MLA_KV_RANK, -1)
        uv = jnp.pad(ukv[:, :, MLA_NOPE:], [(0, 0), (0, 0), (0, LANES - MLA_V)]).reshape(MLA_KV_RANK, -1)
        qa, ka, va = _mla_prep(pm, mla_q_norm[lyr][None], mla_kv_norm[lyr][None], uq.astype(BF16),
                               uk.astype(BF16), uv.astype(BF16), cos_m, sin_m, nb, tm)
        r3 = lambda a: a.reshape(bsz, l_tot, a.shape[-1])
        ya = _attention(qa, r3(ka), va, ctx_len, n_q=MLA_HEADS, n_kv=MLA_HEADS, tq=256)

        zl = jnp.zeros((RWKV_W_LORA, RWKV_HEADS * LANES), F32)
        w2c = jnp.concatenate([jnp.concatenate([hp(rwkv_w2[lyr, 0]), zl], axis=1),
                               jnp.concatenate([zl, hp(rwkv_w2[lyr, 1])], axis=1)], axis=0)
        a2c = jnp.concatenate([jnp.concatenate([hp(rwkv_a2[lyr, 0]), zl], axis=1),
                               jnp.concatenate([zl, hp(rwkv_a2[lyr, 1])], axis=1)], axis=0)
        w0c = jnp.concatenate([hp(rwkv_w0[lyr, 0]), hp(rwkv_w0[lyr, 1])])[None]
        a0c = jnp.concatenate([hp(rwkv_a0[lyr, 0]), hp(rwkv_a0[lyr, 1])])[None]
        feats = _rwkv_features(pr, mu_p, w0c, w2c.astype(BF16), a0c, a2c.astype(BF16),
                               hp(rwkv_g2[lyr]).astype(BF16), hp(rwkv_k_k[lyr])[None],
                               hp(rwkv_k_a[lyr])[None], hp(rwkv_r_k[lyr].reshape(-1))[None],
                               l_tot // tm_feat, ctx_len // tm_feat, tm_feat)
        f_r, f_v, f_kk, f_g, f_bonus, f_lw, f_k, f_b = feats
        o_scan = _rwkv_scan(f_r, f_v, f_kk, f_lw, f_k, f_b, bsz, ctx_len // RWKV_CHUNK)
        yb = _rwkv_out(o_scan, f_bonus, f_g, hp(rwkv_ln_g[lyr])[None], hp(rwkv_ln_b[lyr])[None], tm)

        pad_g = lambda g: jnp.pad(g, (0, LANES - g.shape[0]))[None]
        qc, kc, vc = _gqa_prep(pg, pad_g(gqa_q_norm[lyr]), pad_g(gqa_k_norm[lyr]), cos_h, sin_h,
                               nb, tm, n_q=GQA_HEADS, n_kv=GQA_KV_HEADS, hd=GQA_HEAD,
                               scale=GQA_SCALE, qk_norm=True, transposed=True)
        yc = _attention(qc, r3(kc), vc, ctx_len, n_q=GQA_HEADS, n_kv=GQA_KV_HEADS, tq=256)

        ones_g = jnp.ones((1, LANES), F32)
        qd, kd, vd = _gqa_prep(pw, ones_g, ones_g, cos_h, sin_h, nb, tm, n_q=WIN_HEADS,
                               n_kv=WIN_KV_HEADS, hd=WIN_HEAD, scale=WIN_SCALE, qk_norm=False,
                               transposed=False)
        yd = _window_attention(r3(qd), r3(kd), r3(vd), win_sink[lyr], ctx_len,
                               n_q=WIN_HEADS, n_kv=WIN_KV_HEADS)

        wb = jnp.stack([_pad_head_rows(w_branch[lyr, i], 8, 64) for i in range(N_BRANCH)]).astype(BF16)
        x_mid = _merge(ya.reshape(n, -1), yb, yc.reshape(n, -1), yd.reshape(n, -1), gate, xs, gt1,
                       wb, w_out[lyr].astype(BF16), ln1_g[lyr][None], ln1_b[lyr][None],
                       nb, nctx_b, tm, alpha)

        keys = peer_keys[lyr]
        half = PEER_DQ // 2
        k1 = jnp.pad(keys[:, 0], [(0, 0), (0, 0), (0, LANES - half)]).astype(BF16)
        k2 = jnp.pad(keys[:, 1], [(0, 0), (0, 0), (LANES - half, 0)]).astype(BF16)
        h_in, idx_t, wgt_t, off_t = _peer_topk(x_mid, mod2, peer_wq[lyr].astype(BF16), k1, k2, nb, nctx_b, tm)
        idx = idx_t.reshape(ne, n).T
        wgt = wgt_t.reshape(ne, n).T
        off_flat = off_t.reshape(ne // 2, n).T.reshape(-1)
        par = (idx & 1).astype(F32)
        tb = 64
        cw = _peer_u(off_flat, h_in.reshape(n, 8, LANES), par, wgt, e16, g16,
                     _pack_table(peer_u[lyr]), tb)
        ffn = _peer_v(off_flat, cw, par, e16, _pack_table(peer_v[lyr]), tb).reshape(n, d)
        xs = _ln_res(x_mid, ffn, gt2, ln2_g[lyr][None], ln2_b[lyr][None], nb, nctx_b, tm, alpha)
        stages.append(dict(ya=ya, yb=yb, yc=yc, yd=yd, x_mid=x_mid, idx=idx, wgt=wgt, ffn=ffn,
                           x_out=xs))

    return xs.reshape(bsz, l_tot, d)[:, ctx_len:, :], stages


def kernel(x, c, ctx, c_ctx, ada_w, ada_b, w_in, mla_q_norm, mla_kv_norm, mla_w_uq, mla_w_ukv,
           rwkv_mu, rwkv_w0, rwkv_w2, rwkv_a0, rwkv_a2, rwkv_g2, rwkv_k_k, rwkv_k_a, rwkv_r_k,
           rwkv_ln_g, rwkv_ln_b, gqa_q_norm, gqa_k_norm, win_sink, w_branch, w_out, ln1_g, ln1_b,
           peer_wq, peer_keys, peer_u, peer_v, ln2_g, ln2_b):
    out, _ = _forward(x, c, ctx, c_ctx, ada_w, ada_b, w_in, mla_q_norm, mla_kv_norm, mla_w_uq,
                      mla_w_ukv, rwkv_mu, rwkv_w0, rwkv_w2, rwkv_a0, rwkv_a2, rwkv_g2, rwkv_k_k,
                      rwkv_k_a, rwkv_r_k, rwkv_ln_g, rwkv_ln_b, gqa_q_norm, gqa_k_norm, win_sink,
                      w_branch, w_out, ln1_g, ln1_b, peer_wq, peer_keys, peer_u, peer_v, ln2_g,
                      ln2_b)
    return out
```

```python
import functools
import math

import jax
import jax.numpy as jnp
from jax import lax
from jax.experimental import pallas as pl
from jax.experimental.pallas import tpu as pltpu

F32 = jnp.float32
BF16 = jnp.bfloat16
I32 = jnp.int32
U32 = jnp.uint32

LANES = 128
GRID_W = 64
ROPE_THETA = 10000.0
NEG_INF = -1e30
LN_EPS = 1e-5
RMS_EPS = 1e-6

MLA_HEADS, MLA_Q_RANK, MLA_KV_RANK, MLA_NOPE, MLA_ROPE, MLA_V = 8, 256, 128, 64, 32, 64
MLA_SCALE = (MLA_NOPE + MLA_ROPE) ** -0.5
RWKV_HEADS, RWKV_HEAD = 8, 64
RWKV_W = RWKV_HEADS * RWKV_HEAD
RWKV_W_LORA, RWKV_A_LORA, RWKV_G_LORA = 64, 64, 128
RWKV_GN_EPS = 64e-5
RWKV_CHUNK = 128
GQA_HEADS, GQA_KV_HEADS, GQA_HEAD = 8, 2, 64
GQA_SCALE = GQA_HEAD ** -0.5
WIN_HEADS, WIN_KV_HEADS, WIN_HEAD, WINDOW = 8, 2, 64, 128
WIN_SCALE = WIN_HEAD ** -0.5
PEER_HEADS, PEER_N_KEYS, PEER_TOPK, PEER_DQ = 8, 128, 16, 128
N_BRANCH = 4

VMEM_LIMIT = 56 * 1024 * 1024


def _cparams(sem, vmem=None):
    return pltpu.CompilerParams(dimension_semantics=sem, vmem_limit_bytes=vmem or VMEM_LIMIT)


def _pick_tile(n, cap, mult=LANES):
    best = mult
    for t in range(mult, min(n, cap) + 1, mult):
        if n % t == 0:
            best = t
    return best


def _dot(a, b):
    return jnp.dot(a, b, preferred_element_type=F32)


def _dot_nt(a, b):
    return lax.dot_general(a, b, (((1,), (1,)), ((), ())), preferred_element_type=F32)


def _dot_tn(a, b):
    return lax.dot_general(a, b, (((0,), (0,)), ((), ())), preferred_element_type=F32)


ONES_LANE = 64


def _with_ones_lane(v):
    lane = lax.broadcasted_iota(I32, v.shape, v.ndim - 1)
    return jnp.where((lane & (LANES - 1)) == ONES_LANE, 1.0, v)


def _heads_t(x):
    return jnp.concatenate([x[:, h * LANES:(h + 1) * LANES].T for h in range(x.shape[1] // LANES)],
                           axis=0)


def _layer_norm(y, g, b):
    mu = jnp.mean(y, axis=-1, keepdims=True)
    d = y - mu
    var = jnp.mean(d * d, axis=-1, keepdims=True)
    return d * lax.rsqrt(var + LN_EPS) * g + b


def _swap_pairs(x):
    n = x.shape[-1]
    lane = lax.broadcasted_iota(I32, x.shape, x.ndim - 1)
    nxt = pltpu.roll(x, n - 1, x.ndim - 1)
    prv = pltpu.roll(x, 1, x.ndim - 1)
    return jnp.where((lane & 1) == 0, nxt, prv)


def _rope(x, cos, sin_signed):
    return x * cos + _swap_pairs(x) * sin_signed


def _ada_kernel(c_ref, w_ref, b_ref, o_ref):
    c = c_ref[...]
    s = c * jax.nn.sigmoid(c)
    o_ref[...] = _dot(s.astype(BF16), w_ref[...].astype(BF16)) + b_ref[...]


def _ada(cvec, w, b):
    m, k = cvec.shape
    nc = w.shape[1]
    tn = _pick_tile(nc, 1536)
    return pl.pallas_call(
        _ada_kernel,
        out_shape=jax.ShapeDtypeStruct((m, nc), F32),
        grid=(nc // tn,),
        in_specs=[pl.BlockSpec((m, k), lambda j: (0, 0)),
                  pl.BlockSpec((k, tn), lambda j: (0, j)),
                  pl.BlockSpec((1, tn), lambda j: (0, j))],
        out_specs=pl.BlockSpec((m, tn), lambda j: (0, j)),
        compiler_params=_cparams(("parallel",)),
        name="ada_mod",
    )(cvec, w, b.reshape(1, nc))


def _group_index(i, nb, nctx_b):
    return (i // nb) * 2 + (i % nb >= nctx_b).astype(I32)


def _mm_kernel(x_ref, mod_ref, w_ref, o_ref, *, sub, tm, nb, nctx_b):
    i = pl.program_id(1)
    parts = []
    for u in range(sub):
        m = mod_ref[_group_index(i * sub + u, nb, nctx_b)]
        xm = x_ref[u * tm:(u + 1) * tm, :] * m[0:1, :] + m[1:2, :]
        parts.append(xm.astype(BF16))
    o_ref[...] = _dot(jnp.concatenate(parts, axis=0), w_ref[...]).astype(o_ref.dtype)


def _modmm(x, mod, w, nb, nctx_b, tm, out_dtype=F32):
    n, k = x.shape
    nc = w.shape[1]
    tn = _pick_tile(nc, 2048)
    sub = max(s for s in (4, 2, 1) if (n // tm) % s == 0)
    kern = functools.partial(_mm_kernel, sub=sub, tm=tm, nb=nb, nctx_b=nctx_b)
    return pl.pallas_call(
        kern,
        out_shape=jax.ShapeDtypeStruct((n, nc), out_dtype),
        grid=(nc // tn, n // (tm * sub)),
        in_specs=[pl.BlockSpec((tm * sub, k), lambda j, i: (i, 0)),
                  pl.BlockSpec(mod.shape, lambda j, i: (0, 0, 0)),
                  pl.BlockSpec((k, tn), lambda j, i: (0, j))],
        out_specs=pl.BlockSpec((tm * sub, tn), lambda j, i: (i, j)),
        compiler_params=_cparams(("parallel", "parallel")),
        name="in_proj",
    )(x, mod, w)


def _mla_prep_kernel(p_ref, qn_ref, kvn_ref, wq_ref, wk_ref, wv_ref, cos_ref, sin_ref,
                     qt_ref, k_ref, vt_ref):
    p = p_ref[...]
    dq = p[:, 0:MLA_Q_RANK]
    dkv = p[:, MLA_Q_RANK:MLA_Q_RANK + MLA_KV_RANK]
    krp = p[:, MLA_Q_RANK + MLA_KV_RANK:]
    qn = dq * lax.rsqrt(jnp.mean(dq * dq, axis=-1, keepdims=True) + RMS_EPS) * qn_ref[...]
    kvn = dkv * lax.rsqrt(jnp.mean(dkv * dkv, axis=-1, keepdims=True) + RMS_EPS) * kvn_ref[...]
    kvn = kvn.astype(BF16)
    cos = cos_ref[...]
    sin = sin_ref[...]
    cos_h = jnp.concatenate([cos] * MLA_HEADS, axis=1)
    sin_h = jnp.concatenate([sin] * MLA_HEADS, axis=1)
    q = _rope(_dot(qn.astype(BF16), wq_ref[...]), cos_h, sin_h) * MLA_SCALE
    kr = _rope(krp, cos, sin)
    k = _dot(kvn, wk_ref[...]) + jnp.concatenate([kr] * MLA_HEADS, axis=1)
    qt_ref[0] = _heads_t(q).astype(BF16)
    k_ref[...] = k.astype(BF16)
    vt_ref[0, 0] = _heads_t(_with_ones_lane(_dot(kvn, wv_ref[...]))).astype(BF16)


def _mla_prep(pm, qn, kvn, wq, wk, wv, cos, sin, nb, tm):
    n = pm.shape[0]
    hw = MLA_HEADS * LANES
    bsz = n // (nb * tm)
    full = lambda a: pl.BlockSpec(a.shape, lambda i: (0,) * a.ndim)
    return pl.pallas_call(
        _mla_prep_kernel,
        out_shape=(jax.ShapeDtypeStruct((bsz, hw, nb * tm), BF16),
                   jax.ShapeDtypeStruct((n, hw), BF16),
                   jax.ShapeDtypeStruct((bsz, nb, hw, tm), BF16)),
        grid=(n // tm,),
        in_specs=[pl.BlockSpec((tm, pm.shape[1]), lambda i: (i, 0)),
                  full(qn), full(kvn), full(wq), full(wk), full(wv),
                  pl.BlockSpec((tm, LANES), lambda i: (i % nb, 0)),
                  pl.BlockSpec((tm, LANES), lambda i: (i % nb, 0))],
        out_specs=(pl.BlockSpec((1, hw, tm), lambda i: (i // nb, 0, i % nb)),
                   pl.BlockSpec((tm, hw), lambda i: (i, 0)),
                   pl.BlockSpec((1, 1, hw, tm), lambda i: (i // nb, i % nb, 0, 0))),
        compiler_params=_cparams(("parallel",)),
        name="mla_prep",
    )(pm, qn, kvn, wq, wk, wv, cos, sin)


def _gqa_prep_kernel(p_ref, qn_ref, kn_ref, cos_ref, sin_ref, q_ref, k_ref, v_ref,
                     *, n_q, n_kv, hd, scale, qk_norm, transposed):
    p = p_ref[...].astype(F32)
    cos = cos_ref[...]
    sin = sin_ref[...]

    def head(j, gain):
        x = p[:, j * LANES:(j + 1) * LANES]
        if qk_norm:
            ms = jnp.sum(x * x, axis=-1, keepdims=True) * (1.0 / hd)
            x = x * lax.rsqrt(ms + RMS_EPS) * gain
        return _rope(x, cos, sin)

    qg = qn_ref[...]
    kg = kn_ref[...]
    q = jnp.concatenate([head(j, qg) for j in range(n_q)], axis=1) * scale
    k = jnp.concatenate([head(n_q + j, kg) for j in range(n_kv)], axis=1)
    v = _with_ones_lane(p[:, (n_q + n_kv) * LANES:])
    k_ref[...] = k.astype(BF16)
    if transposed:
        q_ref[0] = _heads_t(q).astype(BF16)
        v_ref[0, 0] = _heads_t(v).astype(BF16)
    else:
        q_ref[...] = q.astype(BF16)
        v_ref[...] = v.astype(BF16)


def _gqa_prep(pg, qn, kn, cos, sin, nb, tm, *, n_q, n_kv, hd, scale, qk_norm, transposed):
    n = pg.shape[0]
    bsz = n // (nb * tm)
    full = lambda a: pl.BlockSpec(a.shape, lambda i: (0,) * a.ndim)
    kern = functools.partial(_gqa_prep_kernel, n_q=n_q, n_kv=n_kv, hd=hd, scale=scale,
                             qk_norm=qk_norm, transposed=transposed)
    if transposed:
        q_shape = jax.ShapeDtypeStruct((bsz, n_q * LANES, nb * tm), BF16)
        v_shape = jax.ShapeDtypeStruct((bsz, nb, n_kv * LANES, tm), BF16)
        q_spec = pl.BlockSpec((1, n_q * LANES, tm), lambda i: (i // nb, 0, i % nb))
        v_spec = pl.BlockSpec((1, 1, n_kv * LANES, tm), lambda i: (i // nb, i % nb, 0, 0))
    else:
        q_shape = jax.ShapeDtypeStruct((n, n_q * LANES), BF16)
        v_shape = jax.ShapeDtypeStruct((n, n_kv * LANES), BF16)
        q_spec = pl.BlockSpec((tm, n_q * LANES), lambda i: (i, 0))
        v_spec = pl.BlockSpec((tm, n_kv * LANES), lambda i: (i, 0))
    return pl.pallas_call(
        kern,
        out_shape=(q_shape, jax.ShapeDtypeStruct((n, n_kv * LANES), BF16), v_shape),
        grid=(n // tm,),
        in_specs=[pl.BlockSpec((tm, pg.shape[1]), lambda i: (i, 0)),
                  full(qn), full(kn),
                  pl.BlockSpec((tm, LANES), lambda i: (i % nb, 0)),
                  pl.BlockSpec((tm, LANES), lambda i: (i % nb, 0))],
        out_specs=(q_spec, pl.BlockSpec((tm, n_kv * LANES), lambda i: (i, 0)), v_spec),
        compiler_params=_cparams(("parallel",)),
        name="gqa_prep",
    )(pg, qn, kn, cos, sin)


ATTN_GROUPS_IN_FLIGHT = 8


def _attn_kernel(qt_ref, k_ref, vt_ref, o_ref, m_s, acc_s, *, n_q, n_kv, tq, tk, nctx_q,
                 nk_ctx, nk_all):
    qi = pl.program_id(1)
    nkb = jnp.where(qi < nctx_q, nk_ctx, nk_all)
    grp = n_q // n_kv
    gs = range(n_kv)
    m_s[...] = jnp.full(m_s.shape, NEG_INF, F32)
    acc_s[...] = jnp.zeros(acc_s.shape, F32)

    def body(kb, carry):
        ks = pl.multiple_of(kb * tk, tk)
        for g0 in range(0, n_kv, ATTN_GROUPS_IN_FLIGHT):
            gb = range(g0, min(g0 + ATTN_GROUPS_IN_FLIGHT, n_kv))
            s, m_old, m_new, p, a, pv = {}, {}, {}, {}, {}, {}
            for g in gb:
                qtg = jnp.concatenate(
                    [qt_ref[0, (g * grp + j) * LANES:(g * grp + j + 1) * LANES, :]
                     for j in range(grp)], axis=1)
                s[g] = _dot(k_ref[0, pl.ds(ks, tk), g * LANES:(g + 1) * LANES], qtg)
            for g in gb:
                m_old[g] = m_s[g]
                m_new[g] = jnp.maximum(m_old[g], jnp.max(s[g], axis=0, keepdims=True))
            for g in gb:
                p[g] = jnp.exp(s[g] - m_new[g]).astype(BF16)
                a[g] = jnp.exp(m_old[g] - m_new[g])
            for g in gb:
                pv[g] = _dot(vt_ref[0, kb, g * LANES:(g + 1) * LANES, :], p[g])
            for g in gb:
                m_s[g] = m_new[g]
                acc_s[g] = a[g] * acc_s[g] + pv[g]
        return carry

    lax.fori_loop(0, nkb, body, 0)
    row = lax.broadcasted_iota(I32, (LANES, 1), 0)
    for g in gs:
        acc = acc_s[g]
        ot = jnp.where(row < ONES_LANE, acc / acc[ONES_LANE:ONES_LANE + 1, :], 0.0)
        for a in range(grp):
            h = g * grp + a
            o_ref[0, :, h * LANES:(h + 1) * LANES] = ot[:, a * tq:(a + 1) * tq].T.astype(o_ref.dtype)


def _attention(qt, k, vt, ctx_len, *, n_q, n_kv, tq):
    b, l, _ = k.shape
    tk = vt.shape[-1]
    rows = (n_q // n_kv) * tq
    kern = functools.partial(_attn_kernel, n_q=n_q, n_kv=n_kv, tq=tq, tk=tk,
                             nctx_q=ctx_len // tq, nk_ctx=ctx_len // tk, nk_all=l // tk)
    return pl.pallas_call(
        kern,
        out_shape=jax.ShapeDtypeStruct((b, l, n_q * LANES), BF16),
        grid=(b, l // tq),
        in_specs=[pl.BlockSpec((1, n_q * LANES, tq), lambda bi, i: (bi, 0, i)),
                  pl.BlockSpec((1, l, n_kv * LANES), lambda bi, i: (bi, 0, 0)),
                  pl.BlockSpec((1, l // tk, n_kv * LANES, tk), lambda bi, i: (bi, 0, 0, 0))],
        out_specs=pl.BlockSpec((1, tq, n_q * LANES), lambda bi, i: (bi, i, 0)),
        scratch_shapes=[pltpu.VMEM((n_kv, 1, rows), F32), pltpu.VMEM((n_kv, LANES, rows), F32)],
        compiler_params=_cparams(("parallel", "arbitrary")),
        name="dense_attn",
    )(qt, k, vt)


def _win_kernel(sink_ref, q_ref, kc_ref, kp_ref, kcur_ref, kn_ref, vc_ref, vp_ref, vcur_ref,
                vn_ref, o_ref, *, n_q, n_kv, nctx_b, nb, ctx_len):
    qi = pl.program_id(1)
    w = WINDOW
    is_lat = (qi >= nctx_b).astype(I32)
    prev_ok = is_lat * (qi - 1 >= nctx_b).astype(I32)
    next_ok = is_lat * (qi + 1 < nb).astype(I32)
    nk = ctx_len + 3 * w
    r = lax.broadcasted_iota(I32, (w, nk), 0)
    c2 = lax.broadcasted_iota(I32, (w, nk), 1)
    c = c2 - ctx_len
    near = jnp.abs(r - (c - w)) <= w
    blk_ok = jnp.where(c < w, prev_ok, jnp.where(c < 2 * w, is_lat, next_ok)) > 0
    valid = jnp.logical_or(c2 < ctx_len, jnp.logical_and(near, blk_ok))
    bias = jnp.where(valid, 0.0, NEG_INF).astype(F32)
    grp = n_q // n_kv
    bias = jnp.concatenate([bias] * grp, axis=0)
    for g in range(n_kv):
        ls = slice(g * LANES, (g + 1) * LANES)
        qg = jnp.concatenate(
            [q_ref[0, :, (g * grp + a) * LANES:(g * grp + a + 1) * LANES] for a in range(grp)],
            axis=0)
        kcat = jnp.concatenate([kc_ref[0, :, ls], kp_ref[0, :, ls], kcur_ref[0, :, ls],
                                kn_ref[0, :, ls]], axis=0)
        vcat = jnp.concatenate([vc_ref[0, :, ls], vp_ref[0, :, ls], vcur_ref[0, :, ls],
                                vn_ref[0, :, ls]], axis=0)
        s = _dot_nt(qg, kcat) + bias
        sk = jnp.concatenate([jnp.full((w, 1), sink_ref[g * grp + a], F32) for a in range(grp)],
                             axis=0)
        m = jnp.maximum(jnp.max(s, axis=1, keepdims=True), sk)
        p = jnp.exp(s - m)
        den = jnp.sum(p, axis=1, keepdims=True) + jnp.exp(sk - m)
        o = _dot(p.astype(BF16), vcat) / den
        for a in range(grp):
            h = g * grp + a
            o_ref[0, :, h * LANES:(h + 1) * LANES] = o[a * w:(a + 1) * w].astype(o_ref.dtype)


def _window_attention(q, k, v, sink, ctx_len, *, n_q, n_kv):
    b, l, _ = q.shape
    w = WINDOW
    nb = l // w
    nctx_b = ctx_len // w
    kw = n_kv * LANES
    kern = functools.partial(_win_kernel, n_q=n_q, n_kv=n_kv, nctx_b=nctx_b, nb=nb,
                             ctx_len=ctx_len)
    ctx_spec = pl.BlockSpec((1, ctx_len, kw), lambda bi, i: (bi, 0, 0))
    prev_spec = pl.BlockSpec((1, w, kw), lambda bi, i: (bi, jnp.maximum(i - 1, 0), 0))
    cur_spec = pl.BlockSpec((1, w, kw), lambda bi, i: (bi, i, 0))
    next_spec = pl.BlockSpec((1, w, kw), lambda bi, i: (bi, jnp.minimum(i + 1, nb - 1), 0))
    return pl.pallas_call(
        kern,
        out_shape=jax.ShapeDtypeStruct((b, l, n_q * LANES), BF16),
        grid=(b, nb),
        in_specs=[pl.BlockSpec(memory_space=pltpu.SMEM),
                  pl.BlockSpec((1, w, n_q * LANES), lambda bi, i: (bi, i, 0)),
                  ctx_spec, prev_spec, cur_spec, next_spec,
                  ctx_spec, prev_spec, cur_spec, next_spec],
        out_specs=pl.BlockSpec((1, w, n_q * LANES), lambda bi, i: (bi, i, 0)),
        compiler_params=_cparams(("parallel", "arbitrary")),
        name="window_attn",
    )(sink, q, k, k, k, k, v, v, v, v)


def _rwkv_feat_kernel(p_ref, hp_ref, hn_ref, mu_ref, w0_ref, w2_ref, a0_ref, a2_ref, g2_ref,
                      kk_ref, ka_ref, rk_ref,
                      r_out, v_out, kk_out, g_out, bonus_out, lw_out, k_out, b_out,
                      *, tm, nb, nctx_b):
    i = pl.program_id(0) % nb
    seq_start = jnp.logical_or(i == 0, i == nctx_b)
    seq_end = jnp.logical_or(i == nctx_b - 1, i == nb - 1)
    p = p_ref[...]
    row = lax.broadcasted_iota(I32, p.shape, 0)
    first = jnp.where(seq_start, 0.0, hp_ref[7:8, :])
    last = jnp.where(seq_end, 0.0, hn_ref[0:1, :])
    prev = jnp.where(row == 0, first, pltpu.roll(p, 1, 0))
    nxt = jnp.where(row == tm - 1, last, pltpu.roll(p, tm - 1, 0))
    mu = mu_ref[...]
    ps = p + mu[0:1, :] * (prev - p) + mu[1:2, :] * (nxt - p)
    hw = RWKV_HEADS * LANES
    r = ps[:, 0:hw]
    k = ps[:, hw:2 * hw]
    v = ps[:, 2 * hw:3 * hw]
    wfb = ps[:, 3 * hw:3 * hw + LANES]
    afb = ps[:, 3 * hw + LANES:3 * hw + 2 * LANES]
    gi = ps[:, 3 * hw + 2 * LANES:]
    kkr = k * kk_ref[...]
    parts = []
    for h in range(RWKV_HEADS):
        x = kkr[:, h * LANES:(h + 1) * LANES]
        nrm = jnp.sqrt(jnp.sum(x * x, axis=-1, keepdims=True))
        parts.append(x / jnp.maximum(nrm, 1e-12))
    kk = jnp.concatenate(parts, axis=1)
    z = w0_ref[...] + _dot(jnp.tanh(wfb).astype(BF16), w2_ref[...])
    lw = -math.exp(-0.5) * jax.nn.sigmoid(z)
    a = jax.nn.sigmoid(a0_ref[...] + _dot(afb.astype(BF16), a2_ref[...]))
    ka = ka_ref[...]
    k0 = k * (1.0 + (a[:, 0:hw] - 1.0) * ka)
    k1 = k * (1.0 + (a[:, hw:] - 1.0) * ka)
    rkk = r * (k0 + k1) * rk_ref[...]
    bparts = []
    for h in range(RWKV_HEADS):
        sl = slice(h * LANES, (h + 1) * LANES)
        bparts.append(jnp.sum(rkk[:, sl], axis=-1, keepdims=True) * v[:, sl])
    r_out[...] = r.astype(r_out.dtype)
    v_out[...] = v.astype(v_out.dtype)
    kk_out[...] = kk.astype(kk_out.dtype)
    g_out[...] = _dot(jax.nn.sigmoid(gi).astype(BF16), g2_ref[...]).astype(g_out.dtype)
    bonus_out[...] = jnp.concatenate(bparts, axis=1).astype(bonus_out.dtype)
    lw_out[0] = lw[:, 0:hw]
    lw_out[1] = lw[:, hw:]
    k_out[0] = k0.astype(k_out.dtype)
    k_out[1] = k1.astype(k_out.dtype)
    b_out[0] = (a[:, 0:hw] * kk).astype(b_out.dtype)
    b_out[1] = (a[:, hw:] * kk).astype(b_out.dtype)


def _rwkv_features(pr, mu, w0, w2, a0, a2, g2, kk, ka, rk, nb, nctx_b, tm):
    n, wid = pr.shape
    hw = RWKV_HEADS * LANES
    full = lambda a: pl.BlockSpec(a.shape, lambda i: (0,) * a.ndim)
    kern = functools.partial(_rwkv_feat_kernel, tm=tm, nb=nb, nctx_b=nctx_b)
    one = jax.ShapeDtypeStruct((n, hw), BF16)
    two = jax.ShapeDtypeStruct((2, n, hw), BF16)
    two_f32 = jax.ShapeDtypeStruct((2, n, hw), F32)
    s1 = pl.BlockSpec((tm, hw), lambda i: (i, 0))
    s2 = pl.BlockSpec((2, tm, hw), lambda i: (0, i, 0))
    t8 = tm // 8
    return pl.pallas_call(
        kern,
        out_shape=(one, one, one, one, one, two_f32, two, two),
        grid=(n // tm,),
        in_specs=[pl.BlockSpec((tm, wid), lambda i: (i, 0)),
                  pl.BlockSpec((8, wid), lambda i: (jnp.maximum(i * t8 - 1, 0), 0)),
                  pl.BlockSpec((8, wid), lambda i: (jnp.minimum((i + 1) * t8, n // 8 - 1), 0)),
                  full(mu), full(w0), full(w2), full(a0), full(a2), full(g2),
                  full(kk), full(ka), full(rk)],
        out_specs=(s1, s1, s1, s1, s1, s2, s2, s2),
        compiler_params=_cparams(("parallel",)),
        name="rwkv_features",
    )(pr, pr, pr, mu, w0, w2, a0, a2, g2, kk, ka, rk)


def _split3_dot(mask_bf16, x):
    x1 = x.astype(BF16)
    r1 = x - x1.astype(F32)
    x2 = r1.astype(BF16)
    x3 = (r1 - x2.astype(F32)).astype(BF16)
    return _dot(mask_bf16, x1) + _dot(mask_bf16, x2) + _dot(mask_bf16, x3)


def _rwkv_scan_kernel(r_ref, v_ref, kk_ref, lw_ref, k_ref, b_ref, o_ref, s_ref):
    d = pl.program_id(1)
    c = pl.program_id(2)
    cs = RWKV_CHUNK

    @pl.when(c == 0)
    def _():
        s_ref[...] = jnp.zeros_like(s_ref)

    rev = d == 1
    t_i = lax.broadcasted_iota(I32, (cs, cs), 0)
    s_i = lax.broadcasted_iota(I32, (cs, cs), 1)
    order = jnp.where(rev, t_i - s_i, s_i - t_i)
    incl = order <= 0
    strict = order < 0
    eye = jnp.where(t_i == s_i, 1.0, 0.0).astype(F32)
    off_masks = []
    for lvl in range(int(math.log2(cs))):
        pair = (t_i >> (lvl + 1)) == (s_i >> (lvl + 1))
        half = (t_i >> lvl) != (s_i >> lvl)
        off = jnp.logical_and(jnp.logical_and(pair, half), strict)
        off_masks.append(jnp.where(off, 1.0, 0.0).astype(F32))
    lw = lw_ref[0]
    cum = _split3_dot(jnp.where(incl, 1.0, 0.0).astype(BF16), lw)
    p_in = jnp.exp(cum)
    p_inv = jnp.exp(-cum)
    p_ex = jnp.exp(cum - lw)
    tot = jnp.where(rev, cum[0:1, :], cum[cs - 1:cs, :])
    p_all = jnp.exp(tot)
    a_t = -kk_ref[...].astype(F32) * p_ex
    r_t = r_ref[...].astype(F32) * p_in
    k_t = k_ref[0].astype(F32) * p_inv
    b_t = b_ref[0].astype(F32) * p_inv
    v = v_ref[...].astype(F32)
    hs = range(RWKV_HEADS)
    sls = [slice(h * LANES, (h + 1) * LANES) for h in hs]
    vb = [v[:, sl].astype(BF16) for sl in sls]
    bk = [jnp.concatenate([b_t[:, sl], k_t[:, sl]], axis=0).astype(BF16) for sl in sls]
    gm = [_dot_nt(jnp.concatenate([a_t[:, sls[h]], r_t[:, sls[h]]], axis=0).astype(BF16), bk[h])
          for h in hs]
    m_ab = [jnp.where(strict, g[0:cs, 0:cs], 0.0) for g in gm]
    mkv = [_dot(jnp.where(strict, gm[h][0:cs, cs:], 0.0).astype(BF16), vb[h]) for h in hs]
    x = [eye + m * off_masks[0] for m in m_ab]
    for lvl in range(1, len(off_masks)):
        xb = [xx.astype(BF16) for xx in x]
        t1 = [_dot(xb[h], (m_ab[h] * off_masks[lvl]).astype(BF16)).astype(BF16) for h in hs]
        x = [x[h] + _dot(t1[h], xb[h]) for h in hs]
    z = [_dot(x[h].astype(BF16),
              jnp.concatenate([a_t[:, sls[h]], mkv[h]], axis=1).astype(BF16)) for h in hs]
    zb = [zz.astype(BF16) for zz in z]
    gy = [_dot(jnp.where(incl, gm[h][cs:, 0:cs], 0.0).astype(BF16), zb[h]) for h in hs]
    y0 = [_dot(jnp.where(incl, gm[h][cs:, cs:], 0.0).astype(BF16), vb[h]) + gy[h][:, LANES:]
          for h in hs]
    s0 = [s_ref[h] for h in hs]
    s0b = [s.astype(BF16) for s in s0]
    u = [_dot_nt(zb[h][:, 0:LANES], s0b[h]) + z[h][:, LANES:] for h in hs]
    for h in hs:
        g_mat = r_t[:, sls[h]] + gy[h][:, 0:LANES]
        o_ref[0, :, sls[h]] = _dot_nt(g_mat.astype(BF16), s0b[h]) + y0[h]
    for h in hs:
        uv = jnp.concatenate([u[h], v[:, sls[h]]], axis=0).astype(BF16)
        s_ref[h] = (s0[h] + _dot_tn(uv, bk[h])) * p_all[:, sls[h]]


def _rwkv_scan(r, v, kk, lw, k, bb, bsz, nctx_c):
    n, hw = r.shape
    cs = RWKV_CHUNK
    nc = n // bsz // cs

    def blk(bi, d, c):
        rc = jnp.where(c < nctx_c, nctx_c - 1 - c, nc - 1 - (c - nctx_c))
        return bi * nc + jnp.where(d == 0, c, rc)

    s1 = pl.BlockSpec((cs, hw), lambda bi, d, c: (blk(bi, d, c), 0))
    s2 = pl.BlockSpec((1, cs, hw), lambda bi, d, c: (d, blk(bi, d, c), 0))
    return pl.pallas_call(
        _rwkv_scan_kernel,
        out_shape=jax.ShapeDtypeStruct((2, n, hw), F32),
        grid=(bsz, 2, nc),
        in_specs=[s1, s1, s1, s2, s2, s2],
        out_specs=s2,
        scratch_shapes=[pltpu.VMEM((RWKV_HEADS, LANES, LANES), F32)],
        compiler_params=_cparams(("parallel", "parallel", "arbitrary")),
        name="rwkv_scan",
    )(r, v, kk, lw, k, bb)


def _rwkv_out_kernel(o_ref, bonus_ref, g_ref, lng_ref, lnb_ref, y_ref):
    o = o_ref[0] + o_ref[1]
    lane = lax.broadcasted_iota(I32, (1, LANES), 1)
    real = lane < RWKV_HEAD
    lng = lng_ref[...]
    lnb = lnb_ref[...]
    parts = []
    for h in range(RWKV_HEADS):
        sl = slice(h * LANES, (h + 1) * LANES)
        x = o[:, sl]
        mu = jnp.sum(x, axis=-1, keepdims=True) * (1.0 / RWKV_HEAD)
        dlt = jnp.where(real, x - mu, 0.0)
        var = jnp.sum(dlt * dlt, axis=-1, keepdims=True) * (1.0 / RWKV_HEAD)
        parts.append(dlt * lax.rsqrt(var + RWKV_GN_EPS) * lng[:, sl] + lnb[:, sl])
    y = (jnp.concatenate(parts, axis=1) + bonus_ref[...].astype(F32)) * g_ref[...].astype(F32)
    y_ref[...] = y.astype(y_ref.dtype)


def _rwkv_out(o, bonus, g, lng, lnb, tm):
    _, n, hw = o.shape
    full = lambda a: pl.BlockSpec(a.shape, lambda i: (0,) * a.ndim)
    s1 = pl.BlockSpec((tm, hw), lambda i: (i, 0))
    return pl.pallas_call(
        _rwkv_out_kernel,
        out_shape=jax.ShapeDtypeStruct((n, hw), BF16),
        grid=(n // tm,),
        in_specs=[pl.BlockSpec((2, tm, hw), lambda i: (0, i, 0)), s1, s1, full(lng), full(lnb)],
        out_specs=s1,
        compiler_params=_cparams(("parallel",)),
        name="rwkv_out",
    )(o, bonus, g, lng, lnb)


def _merge_kernel(ya_ref, yb_ref, yc_ref, yd_ref, gate_ref, x_ref, gt_ref, wb_ref, wo_ref,
                  lng_ref, lnb_ref, o_ref, *, alpha, d):
    ys = (ya_ref, yb_ref, yc_ref, yd_ref)
    acc = None
    for i in range(N_BRANCH):
        gate = jax.nn.sigmoid(gate_ref[:, i * d:(i + 1) * d].astype(F32))
        term = gate * _dot(ys[i][...], wb_ref[i])
        acc = term if acc is None else acc + term
    mix = _dot(acc.astype(BF16), wo_ref[...])
    y = alpha * x_ref[...] + gt_ref[0] * mix
    o_ref[...] = _layer_norm(y, lng_ref[...], lnb_ref[...])


def _merge(ya, yb, yc, yd, gate, x, gt, wb, wo, lng, lnb, nb, nctx_b, tm, alpha, latent_only):
    n, d = x.shape
    hw = ya.shape[1]
    full = lambda a: pl.BlockSpec(a.shape, lambda i: (0,) * a.ndim)
    if latent_only:
        nbl = nb - nctx_b
        n_out = n // nb * nbl
        src = lambda i: (i // nbl) * nb + nctx_b + i % nbl
    else:
        n_out = n
        src = lambda i: i
    sy = pl.BlockSpec((tm, hw), lambda i: (src(i), 0))
    kern = functools.partial(_merge_kernel, alpha=alpha, d=d)
    return pl.pallas_call(
        kern,
        out_shape=jax.ShapeDtypeStruct((n_out, d), F32),
        grid=(n_out // tm,),
        in_specs=[sy, sy, sy, sy,
                  pl.BlockSpec((tm, N_BRANCH * d), lambda i: (src(i), 0)),
                  pl.BlockSpec((tm, d), lambda i: (src(i), 0)),
                  pl.BlockSpec((1, 1, d), lambda i: (_group_index(src(i), nb, nctx_b), 0, 0)),
                  full(wb), full(wo), full(lng), full(lnb)],
        out_specs=pl.BlockSpec((tm, d), lambda i: (i, 0)),
        compiler_params=_cparams(("parallel",)),
        name="merge",
    )(ya, yb, yc, yd, gate, x, gt, wb, wo, lng, lnb)


def _extract_topk(srcs, n_rows, k, val_refs, pos_refs):
    rio = lax.broadcasted_iota(I32, srcs[0].shape, 0).astype(F32)
    js = range(len(srcs))
    for rnk in range(k):
        m = [jnp.max(s, axis=0, keepdims=True) for s in srcs]
        pos = [jnp.min(jnp.where(srcs[j] == m[j], rio, float(n_rows)), axis=0, keepdims=True)
               for j in js]
        for j in js:
            val_refs[j][rnk:rnk + 1, :] = m[j]
            pos_refs[j][rnk:rnk + 1, :] = pos[j]
        srcs = [jnp.where(rio == pos[j], -jnp.inf, srcs[j]) for j in js]


PEER_CAND_COUNTS = tuple(PEER_TOPK // (a + 1) for a in range(PEER_TOPK))
PEER_N_CAND = sum(PEER_CAND_COUNTS)
PEER_CAND_ROWS = -(-PEER_N_CAND // 8) * 8


PEER_HEADS_PER_STEP = 2


def _peer_topk_kernel(x_ref, mod_ref, wq_ref, k1_ref, k2_ref, h_ref, idx_ref, wgt_ref, off_ref,
                      q_s, v1_s, i1_s, v2_s, i2_s, cand_s, cidx_s, best_s, pos_s):
    step = pl.program_id(1)
    hps = range(PEER_HEADS_PER_STEP)

    @pl.when(step == 0)
    def _():
        m = mod_ref[0]
        hh = x_ref[...] * m[0:1, :] + m[1:2, :]
        h_ref[...] = hh
        q = _dot(hh.astype(BF16), wq_ref[...])
        for a in range(PEER_HEADS):
            q_s[a] = q[:, a * LANES:(a + 1) * LANES].astype(BF16)

    qh = [q_s[step * PEER_HEADS_PER_STEP + j] for j in hps]
    scores = [_dot_nt(k1_ref[j], qh[j]) for j in hps] + [_dot_nt(k2_ref[j], qh[j]) for j in hps]
    _extract_topk(scores, PEER_N_KEYS, PEER_TOPK,
                  [v1_s.at[j] for j in hps] + [v2_s.at[j] for j in hps],
                  [i1_s.at[j] for j in hps] + [i2_s.at[j] for j in hps])
    for j in hps:
        row = 0
        for a, cnt in enumerate(PEER_CAND_COUNTS):
            cand_s[j, row:row + cnt, :] = v1_s[j, a:a + 1, :] + v2_s[j, 0:cnt, :]
            cidx_s[j, row:row + cnt, :] = (i1_s[j, a:a + 1, :] * float(PEER_N_KEYS)
                                           + i2_s[j, 0:cnt, :])
            row += cnt
        pad = PEER_CAND_ROWS - PEER_N_CAND
        if pad:
            cand_s[j, PEER_N_CAND:, :] = jnp.full((pad, cand_s.shape[2]), -jnp.inf, F32)
            cidx_s[j, PEER_N_CAND:, :] = jnp.zeros((pad, cand_s.shape[2]), F32)
    _extract_topk([cand_s[j] for j in hps], PEER_CAND_ROWS, PEER_TOPK,
                  [best_s.at[j] for j in hps], [pos_s.at[j] for j in hps])
    for j in hps:
        cidx = cidx_s[j]
        rio = lax.broadcasted_iota(I32, cidx.shape, 0).astype(F32)
        lo = None
        for rnk in range(PEER_TOPK):
            sel = rio == pos_s[j, rnk:rnk + 1, :]
            e_id = jnp.max(jnp.where(sel, cidx, -1.0), axis=0, keepdims=True).astype(I32)
            idx_ref[j, rnk:rnk + 1, :] = e_id
            off = (e_id >> 1) * 8
            if rnk % 2 == 0:
                lo = off
            else:
                off_ref[j, rnk // 2:rnk // 2 + 1, :] = lo | (off << 16)
        best = best_s[j]
        e = jnp.exp(best - best[0:1, :])
        wgt_ref[j] = e / jnp.sum(e, axis=0, keepdims=True)


def _peer_topk(x, mod, wq, k1, k2, nb, nctx_b, tm):
    n, d = x.shape
    tk = PEER_TOPK
    hp = PEER_HEADS_PER_STEP
    full = lambda a: pl.BlockSpec(a.shape, lambda i, h: (0,) * a.ndim)
    return pl.pallas_call(
        _peer_topk_kernel,
        out_shape=(jax.ShapeDtypeStruct((n, d), F32),
                   jax.ShapeDtypeStruct((PEER_HEADS, tk, n), I32),
                   jax.ShapeDtypeStruct((PEER_HEADS, tk, n), F32),
                   jax.ShapeDtypeStruct((PEER_HEADS, tk // 2, n), I32)),
        grid=(n // tm, PEER_HEADS // hp),
        in_specs=[pl.BlockSpec((tm, d), lambda i, h: (i, 0)),
                  pl.BlockSpec((1, 2, d), lambda i, h: (_group_index(i, nb, nctx_b), 0, 0)),
                  full(wq),
                  pl.BlockSpec((hp, PEER_N_KEYS, LANES), lambda i, h: (h, 0, 0)),
                  pl.BlockSpec((hp, PEER_N_KEYS, LANES), lambda i, h: (h, 0, 0))],
        out_specs=(pl.BlockSpec((tm, d), lambda i, h: (i, 0)),
                   pl.BlockSpec((hp, tk, tm), lambda i, h: (h, 0, i)),
                   pl.BlockSpec((hp, tk, tm), lambda i, h: (h, 0, i)),
                   pl.BlockSpec((hp, tk // 2, tm), lambda i, h: (h, 0, i))),
        scratch_shapes=[pltpu.VMEM((PEER_HEADS, tm, LANES), BF16),
                        pltpu.VMEM((hp, tk, tm), F32), pltpu.VMEM((hp, tk, tm), F32),
                        pltpu.VMEM((hp, tk, tm), F32), pltpu.VMEM((hp, tk, tm), F32),
                        pltpu.VMEM((hp, PEER_CAND_ROWS, tm), F32),
                        pltpu.VMEM((hp, PEER_CAND_ROWS, tm), F32),
                        pltpu.VMEM((hp, tk, tm), F32), pltpu.VMEM((hp, tk, tm), F32)],
        compiler_params=_cparams(("parallel", "arbitrary")),
        name="peer_topk",
    )(x, mod, wq, k1, k2)


PEER_NE = PEER_HEADS * PEER_TOPK
PEER_COLS = PEER_NE * 16
PEER_TOK_UNROLL = 16


def _table_spec(tab):
    return pl.BlockSpec(tab.shape, lambda i: (0, 0), pipeline_mode=pl.Buffered(1))


def _gather_view(off_ref, tab_v, t):
    tiles = []
    tok_ref = off_ref.at[pl.ds(t * (PEER_NE // 2), PEER_NE // 2)]
    for j in range(PEER_NE // 2):
        w = tok_ref[j]
        o0 = pl.multiple_of(w & 0xFFFF, 8)
        o1 = pl.multiple_of(lax.shift_right_logical(w, 16), 8)
        tiles.append(tab_v[pl.ds(o0, 8), :])
        tiles.append(tab_v[pl.ds(o1, 8), :])
    return pltpu.bitcast(jnp.concatenate(tiles, axis=0), BF16)


def _select_mask(px_row):
    shp = (8, PEER_COLS)
    row = lax.broadcasted_iota(I32, shp, 0)
    col = lax.broadcasted_iota(I32, shp, 1)
    sub = (col >> 1) & 7
    fixed = jnp.logical_and((col & 1) == (row >> 2), (sub & 3) == (row & 3))
    return jnp.logical_and(fixed, (sub >> 2).astype(F32) == px_row)


def _split2(x):
    x1 = x.astype(BF16)
    return x1, (x - x1.astype(F32)).astype(BF16)


def _peer_u_kernel(off_ref, h_ref, par_ref, wgt_ref, e16_ref, g16_ref, tab_v, c_ref,
                   px_s, d_s, *, tb):
    px_s[...] = _dot(par_ref[...].astype(BF16), e16_ref[...])

    def tokens(i, carry):
        for u in range(PEER_TOK_UNROLL):
            t = i * PEER_TOK_UNROLL + u
            view = _gather_view(off_ref, tab_v, t)
            h1, h2 = _split2(h_ref[t])
            dd = _dot_nt(jnp.concatenate([h1, h2], axis=0), view)
            dd = jnp.where(_select_mask(px_s[pl.ds(t, 1), :]), dd[0:8] + dd[8:16], 0.0)
            d_s[pl.ds(t, 1), :] = jnp.sum(dd, axis=0, keepdims=True)
        return carry

    lax.fori_loop(0, tb // PEER_TOK_UNROLL, tokens, 0)
    d1, d2 = _split2(d_s[...])
    g16 = g16_ref[...]
    act = _dot(d1, g16) + _dot(d2, g16)
    gelu = 0.5 * act * (1.0 + lax.erf(act * (2.0 ** -0.5)))
    c_ref[...] = gelu * wgt_ref[...]


def _peer_u(off_flat, h3, par, wgt, e16, g16, tab, tb):
    n = h3.shape[0]
    full = lambda a: pl.BlockSpec(a.shape, lambda i: (0,) * a.ndim)
    kern = functools.partial(_peer_u_kernel, tb=tb)
    tok = pl.BlockSpec((tb, PEER_NE), lambda i: (i, 0))
    return pl.pallas_call(
        kern,
        out_shape=jax.ShapeDtypeStruct((n, PEER_NE), F32),
        grid=(n // tb,),
        in_specs=[pl.BlockSpec((tb * PEER_NE // 2,), lambda i: (i,), memory_space=pltpu.SMEM),
                  pl.BlockSpec((tb, 8, LANES), lambda i: (i, 0, 0)),
                  tok, tok, full(e16), full(g16), _table_spec(tab)],
        out_specs=tok,
        scratch_shapes=[pltpu.VMEM((tb, PEER_COLS), F32), pltpu.VMEM((tb, PEER_COLS), F32)],
        compiler_params=_cparams(("arbitrary",)),
        name="peer_u",
    )(off_flat, h3, par, wgt, e16, g16, tab)


def _peer_v_kernel(off_ref, c_ref, par_ref, e16_ref, tab_v, o_ref, px_s, c1_s, c2_s, *, tb):
    e16 = e16_ref[...]
    px_s[...] = _dot(par_ref[...].astype(BF16), e16)
    c1, c2 = _split2(c_ref[...])
    c1_s[...] = _dot(c1, e16)
    c2_s[...] = _dot(c2, e16)

    def tokens(i, carry):
        for u in range(PEER_TOK_UNROLL):
            t = i * PEER_TOK_UNROLL + u
            view = _gather_view(off_ref, tab_v, t)
            sel = _select_mask(px_s[pl.ds(t, 1), :])
            lhs = jnp.concatenate([jnp.where(sel, c1_s[pl.ds(t, 1), :], 0.0),
                                   jnp.where(sel, c2_s[pl.ds(t, 1), :], 0.0)], axis=0)
            out = _dot(lhs.astype(BF16), view)
            o_ref[t] = out[0:8] + out[8:16]
        return carry

    lax.fori_loop(0, tb // PEER_TOK_UNROLL, tokens, 0)


def _peer_v(off_flat, cw, par, e16, tab, tb):
    n = cw.shape[0]
    full = lambda a: pl.BlockSpec(a.shape, lambda i: (0,) * a.ndim)
    kern = functools.partial(_peer_v_kernel, tb=tb)
    tok = pl.BlockSpec((tb, PEER_NE), lambda i: (i, 0))
    return pl.pallas_call(
        kern,
        out_shape=jax.ShapeDtypeStruct((n, 8, LANES), F32),
        grid=(n // tb,),
        in_specs=[pl.BlockSpec((tb * PEER_NE // 2,), lambda i: (i,), memory_space=pltpu.SMEM),
                  tok, tok, full(e16), _table_spec(tab)],
        out_specs=pl.BlockSpec((tb, 8, LANES), lambda i: (i, 0, 0)),
        scratch_shapes=[pltpu.VMEM((tb, PEER_COLS), F32), pltpu.VMEM((tb, PEER_COLS), F32),
                        pltpu.VMEM((tb, PEER_COLS), F32)],
        compiler_params=_cparams(("arbitrary",)),
        name="peer_v",
    )(off_flat, cw, par, e16, tab)


def _pack_table(tab):
    e, d = tab.shape
    bits = lax.bitcast_convert_type(tab.astype(BF16), jnp.uint16).astype(U32)
    words = bits[:, :d // 2] | (bits[:, d // 2:] << 16)
    return words.reshape(e * 4, LANES)


def _ln_res_kernel(x_ref, f_ref, gt_ref, lng_ref, lnb_ref, o_ref, *, alpha):
    y = alpha * x_ref[...] + gt_ref[0] * f_ref[...]
    o_ref[...] = _layer_norm(y, lng_ref[...], lnb_ref[...])


def _ln_res(x, f, gt, lng, lnb, nb, nctx_b, tm, alpha):
    n, d = x.shape
    full = lambda a: pl.BlockSpec(a.shape, lambda i: (0,) * a.ndim)
    s1 = pl.BlockSpec((tm, d), lambda i: (i, 0))
    return pl.pallas_call(
        functools.partial(_ln_res_kernel, alpha=alpha),
        out_shape=jax.ShapeDtypeStruct((n, d), F32),
        grid=(n // tm,),
        in_specs=[s1, s1,
                  pl.BlockSpec((1, 1, d), lambda i: (_group_index(i, nb, nctx_b), 0, 0)),
                  full(lng), full(lnb)],
        out_specs=s1,
        compiler_params=_cparams(("parallel",)),
        name="ln_res",
    )(x, f, gt, lng, lnb)


def _pad_heads(w, n_heads, hd):
    lead = w.shape[:-1]
    w = w.reshape(lead + (n_heads, hd))
    w = jnp.pad(w, [(0, 0)] * len(lead) + [(0, 0), (0, LANES - hd)])
    return w.reshape(lead + (n_heads * LANES,))


def _pad_head_rows(w, n_heads, hd):
    d = w.shape[-1]
    w = w.reshape(n_heads, hd, d)
    w = jnp.pad(w, [(0, 0), (0, LANES - hd), (0, 0)])
    return w.reshape(n_heads * LANES, d)


def _rope_tables(rows, rot_dim, lane_off, ctx_len):
    r_idx = jnp.repeat(jnp.arange(rows), GRID_W).astype(F32)
    c_idx = jnp.tile(jnp.arange(GRID_W), rows).astype(F32)
    n = rot_dim // 4
    inv = ROPE_THETA ** (-jnp.arange(n, dtype=F32) / n)
    ang = jnp.concatenate([r_idx[:, None] * inv, c_idx[:, None] * inv], axis=-1)
    cos = jnp.repeat(jnp.cos(ang), 2, axis=-1)
    sin = jnp.repeat(jnp.sin(ang), 2, axis=-1) * jnp.tile(jnp.array([-1.0, 1.0], F32), rot_dim // 2)
    s = ang.shape[0]
    cos_t = jnp.ones((ctx_len + s, LANES), F32).at[ctx_len:, lane_off:lane_off + rot_dim].set(cos)
    sin_t = jnp.zeros((ctx_len + s, LANES), F32).at[ctx_len:, lane_off:lane_off + rot_dim].set(sin)
    return cos_t, sin_t


def _split_cols(w, widths):
    out, start = [], 0
    for wd in widths:
        out.append(w[..., start:start + wd])
        start += wd
    return out


def _forward(x, c, ctx, c_ctx, ada_w, ada_b, w_in, mla_q_norm, mla_kv_norm, mla_w_uq, mla_w_ukv,
             rwkv_mu, rwkv_w0, rwkv_w2, rwkv_a0, rwkv_a2, rwkv_g2, rwkv_k_k, rwkv_k_a, rwkv_r_k,
             rwkv_ln_g, rwkv_ln_b, gqa_q_norm, gqa_k_norm, win_sink, w_branch, w_out, ln1_g, ln1_b,
             peer_wq, peer_keys, peer_u, peer_v, ln2_g, ln2_b):
    stages = []
    bsz, seq, d = x.shape
    ctx_len = ctx.shape[1]
    depth = ada_w.shape[0]
    alpha = (2 * depth) ** 0.25
    l_tot = ctx_len + seq
    n = bsz * l_tot
    tm = 256
    tm_feat = 128
    assert ctx_len % tm == 0 and seq % tm == 0 and seq % GRID_W == 0
    nb = l_tot // tm
    nctx_b = ctx_len // tm
    rows = seq // GRID_W
    ne = PEER_HEADS * PEER_TOPK

    xs = jnp.concatenate([ctx, x], axis=1).reshape(n, d)
    cos_m, sin_m = _rope_tables(rows, MLA_ROPE, MLA_NOPE, ctx_len)
    cos_h, sin_h = _rope_tables(rows, GQA_HEAD, 0, ctx_len)

    m_rows = 16
    cvec = jnp.zeros((m_rows, d), F32).at[:bsz].set(c).at[bsz].set(c_ctx)
    mla_in = MLA_Q_RANK + MLA_KV_RANK + MLA_ROPE
    rwkv_in = 3 * RWKV_W + 2 * RWKV_W_LORA + 2 * RWKV_A_LORA + RWKV_G_LORA
    gqa_in = (GQA_HEADS + 2 * GQA_KV_HEADS) * GQA_HEAD
    win_in = (WIN_HEADS + 2 * WIN_KV_HEADS) * WIN_HEAD
    rw_widths = (RWKV_W, RWKV_W, RWKV_W, RWKV_W_LORA, RWKV_W_LORA, RWKV_A_LORA, RWKV_A_LORA,
                 RWKV_G_LORA)

    col16 = jnp.arange(PEER_COLS)
    e16 = (col16[None, :] // 16 == jnp.arange(ne)[:, None]).astype(BF16)
    g16 = e16.T

    for lyr in range(depth):
        last = lyr == depth - 1
        mod = _ada(cvec, ada_w[lyr], ada_b[lyr])
        chunks = [mod[:, i * d:(i + 1) * d] for i in range(6)]

        def table(ch):
            lat = ch[:bsz]
            cx = jnp.broadcast_to(ch[bsz][None], (bsz, d))
            return jnp.stack([cx, lat], axis=1).reshape(bsz * 2, d)

        sh1, sc1, gt1, sh2, sc2, gt2 = [table(ch) for ch in chunks]
        mod1 = jnp.stack([1.0 + sc1, sh1], axis=1)
        mod2 = jnp.stack([1.0 + sc2, sh2], axis=1)
        gt1 = gt1[:, None, :]
        gt2 = gt2[:, None, :]

        wi = w_in[lyr]
        w_mla, w_rw, w_gq, w_wn, w_gate = _split_cols(wi, (mla_in, rwkv_in, gqa_in, win_in, N_BRANCH * d))
        zc = lambda k: jnp.zeros((d, k), F32)
        w_mla_p = jnp.concatenate([w_mla[:, :MLA_Q_RANK + MLA_KV_RANK], zc(MLA_NOPE),
                                   w_mla[:, MLA_Q_RANK + MLA_KV_RANK:], zc(LANES - MLA_NOPE - MLA_ROPE)],
                                  axis=1)
        rr, rk_, rv, rwf, rwb, raf, rab, rgi = _split_cols(w_rw, rw_widths)
        hp = lambda w: _pad_heads(w, RWKV_HEADS, RWKV_HEAD)
        w_rw_p = jnp.concatenate([hp(rr), hp(rk_), hp(rv), rwf, rwb, raf, rab, rgi], axis=1)
        mu_parts = _split_cols(rwkv_mu[lyr], rw_widths)
        mu_p = jnp.concatenate([hp(mu_parts[0]), hp(mu_parts[1]), hp(mu_parts[2])] + mu_parts[3:], axis=1)

        def gqa_cols(w, nq, nkv, hd):
            q_, k_, v_ = _split_cols(w, (nq * hd, nkv * hd, nkv * hd))
            return jnp.concatenate([_pad_heads(q_, nq, hd), _pad_heads(k_, nkv, hd),
                                    _pad_heads(v_, nkv, hd)], axis=1)

        w_gq_p = gqa_cols(w_gq, GQA_HEADS, GQA_KV_HEADS, GQA_HEAD)
        w_wn_p = gqa_cols(w_wn, WIN_HEADS, WIN_KV_HEADS, WIN_HEAD)

        pm = _modmm(xs, mod1, w_mla_p.astype(BF16), nb, nctx_b, tm)
        pr = _modmm(xs, mod1, w_rw_p.astype(BF16), nb, nctx_b, tm)
        pg = _modmm(xs, mod1, w_gq_p.astype(BF16), nb, nctx_b, tm, BF16)
        pw = _modmm(xs, mod1, w_wn_p.astype(BF16), nb, nctx_b, tm, BF16)
        gate = _modmm(xs, mod1, w_gate.astype(BF16), nb, nctx_b, tm, BF16)

        uq = mla_w_uq[lyr].reshape(MLA_Q_RANK, MLA_HEADS, MLA_NOPE + MLA_ROPE)
        uq = jnp.pad(uq, [(0, 0), (0, 0), (0, LANES - MLA_NOPE - MLA_ROPE)]).reshape(MLA_Q_RANK, -1)
        ukv = mla_w_ukv[lyr].reshape(MLA_KV_RANK, MLA_HEADS, MLA_NOPE + MLA_V)
        uk = jnp.pad(ukv[:, :, :MLA_NOPE], [(0, 0), (0, 0), (0, LANES - MLA_NOPE)]).reshape(MLA_KV_RANK, -1)
        uv = jnp.pad(ukv[:, :, MLA_NOPE:], [(0, 0), (0, 0), (0, LANES - MLA_V)]).reshape(MLA_KV_RANK, -1)
        qa, ka, va = _mla_prep(pm, mla_q_norm[lyr][None], mla_kv_norm[lyr][None], uq.astype(BF16),
                               uk.astype(BF16), uv.astype(BF16), cos_m, sin_m, nb, tm)
        r3 = lambda a: a.reshape(bsz, l_tot, a.shape[-1])
        ya = _attention(qa, r3(ka), va, ctx_len, n_q=MLA_HEADS, n_kv=MLA_HEADS, tq=256)

        zl = jnp.zeros((RWKV_W_LORA, RWKV_HEADS * LANES), F32)
        w2c = jnp.concatenate([jnp.concatenate([hp(rwkv_w2[lyr, 0]), zl], axis=1),
                               jnp.concatenate([zl, hp(rwkv_w2[lyr, 1])], axis=1)], axis=0)
        a2c = jnp.concatenate([jnp.concatenate([hp(rwkv_a2[lyr, 0]), zl], axis=1),
                               jnp.concatenate([zl, hp(rwkv_a2[lyr, 1])], axis=1)], axis=0)
        w0c = jnp.concatenate([hp(rwkv_w0[lyr, 0]), hp(rwkv_w0[lyr, 1])])[None]
        a0c = jnp.concatenate([hp(rwkv_a0[lyr, 0]), hp(rwkv_a0[lyr, 1])])[None]
        feats = _rwkv_features(pr, mu_p, w0c, w2c.astype(BF16), a0c, a2c.astype(BF16),
                               hp(rwkv_g2[lyr]).astype(BF16), hp(rwkv_k_k[lyr])[None],
                               hp(rwkv_k_a[lyr])[None], hp(rwkv_r_k[lyr].reshape(-1))[None],
                               l_tot // tm_feat, ctx_len // tm_feat, tm_feat)
        f_r, f_v, f_kk, f_g, f_bonus, f_lw, f_k, f_b = feats
        o_scan = _rwkv_scan(f_r, f_v, f_kk, f_lw, f_k, f_b, bsz, ctx_len // RWKV_CHUNK)
        yb = _rwkv_out(o_scan, f_bonus, f_g, hp(rwkv_ln_g[lyr])[None], hp(rwkv_ln_b[lyr])[None], tm)

        pad_g = lambda g: jnp.pad(g, (0, LANES - g.shape[0]))[None]
        qc, kc, vc = _gqa_prep(pg, pad_g(gqa_q_norm[lyr]), pad_g(gqa_k_norm[lyr]), cos_h, sin_h,
                               nb, tm, n_q=GQA_HEADS, n_kv=GQA_KV_HEADS, hd=GQA_HEAD,
                               scale=GQA_SCALE, qk_norm=True, transposed=True)
        yc = _attention(qc, r3(kc), vc, ctx_len, n_q=GQA_HEADS, n_kv=GQA_KV_HEADS, tq=256)

        ones_g = jnp.ones((1, LANES), F32)
        qd, kd, vd = _gqa_prep(pw, ones_g, ones_g, cos_h, sin_h, nb, tm, n_q=WIN_HEADS,
                               n_kv=WIN_KV_HEADS, hd=WIN_HEAD, scale=WIN_SCALE, qk_norm=False,
                               transposed=False)
        yd = _window_attention(r3(qd), r3(kd), r3(vd), win_sink[lyr], ctx_len,
                               n_q=WIN_HEADS, n_kv=WIN_KV_HEADS)

        wb = jnp.stack([_pad_head_rows(w_branch[lyr, i], 8, 64) for i in range(N_BRANCH)]).astype(BF16)
        x_mid = _merge(ya.reshape(n, -1), yb, yc.reshape(n, -1), yd.reshape(n, -1), gate, xs, gt1,
                       wb, w_out[lyr].astype(BF16), ln1_g[lyr][None], ln1_b[lyr][None],
                       nb, nctx_b, tm, alpha, last)
        if last:
            n_p, nb_p, nctx_p = bsz * seq, seq // tm, 0
        else:
            n_p, nb_p, nctx_p = n, nb, nctx_b

        keys = peer_keys[lyr]
        half = PEER_DQ // 2
        k1 = jnp.pad(keys[:, 0], [(0, 0), (0, 0), (0, LANES - half)]).astype(BF16)
        k2 = jnp.pad(keys[:, 1], [(0, 0), (0, 0), (LANES - half, 0)]).astype(BF16)
        h_in, idx_t, wgt_t, off_t = _peer_topk(x_mid, mod2, peer_wq[lyr].astype(BF16), k1, k2,
                                               nb_p, nctx_p, tm)
        idx = idx_t.reshape(ne, n_p).T
        wgt = wgt_t.reshape(ne, n_p).T
        off_flat = off_t.reshape(ne // 2, n_p).T.reshape(-1)
        par = (idx & 1).astype(F32)
        tb = 64
        cw = _peer_u(off_flat, h_in.reshape(n_p, 8, LANES), par, wgt, e16, g16,
                     _pack_table(peer_u[lyr]), tb)
        ffn = _peer_v(off_flat, cw, par, e16, _pack_table(peer_v[lyr]), tb).reshape(n_p, d)
        xs = _ln_res(x_mid, ffn, gt2, ln2_g[lyr][None], ln2_b[lyr][None], nb_p, nctx_p, tm, alpha)
        stages.append(dict(ya=ya, yb=yb, yc=yc, yd=yd, x_mid=x_mid, idx=idx, wgt=wgt, ffn=ffn,
                           x_out=xs))

    return xs.reshape(bsz, seq, d), stages


def kernel(x, c, ctx, c_ctx, ada_w, ada_b, w_in, mla_q_norm, mla_kv_norm, mla_w_uq, mla_w_ukv,
           rwkv_mu, rwkv_w0, rwkv_w2, rwkv_a0, rwkv_a2, rwkv_g2, rwkv_k_k, rwkv_k_a, rwkv_r_k,
           rwkv_ln_g, rwkv_ln_b, gqa_q_norm, gqa_k_norm, win_sink, w_branch, w_out, ln1_g, ln1_b,
           peer_wq, peer_keys, peer_u, peer_v, ln2_g, ln2_b):
    out, _ = _forward(x, c, ctx, c_ctx, ada_w, ada_b, w_in, mla_q_norm, mla_kv_norm, mla_w_uq,
                      mla_w_ukv, rwkv_mu, rwkv_w0, rwkv_w2, rwkv_a0, rwkv_a2, rwkv_g2, rwkv_k_k,
                      rwkv_k_a, rwkv_r_k, rwkv_ln_g, rwkv_ln_b, gqa_q_norm, gqa_k_norm, win_sink,
                      w_branch, w_out, ln1_g, ln1_b, peer_wq, peer_keys, peer_u, peer_v, ln2_g,
                      ln2_b)
    return out
```

```python
import functools
import math

import jax
import jax.numpy as jnp
from jax import lax
from jax.experimental import pallas as pl
from jax.experimental.pallas import tpu as pltpu

F32 = jnp.float32
BF16 = jnp.bfloat16
I32 = jnp.int32
U32 = jnp.uint32

LANES = 128
GRID_W = 64
ROPE_THETA = 10000.0
NEG_INF = -1e30
LN_EPS = 1e-5
RMS_EPS = 1e-6

MLA_HEADS, MLA_Q_RANK, MLA_KV_RANK, MLA_NOPE, MLA_ROPE, MLA_V = 8, 256, 128, 64, 32, 64
MLA_SCALE = (MLA_NOPE + MLA_ROPE) ** -0.5
RWKV_HEADS, RWKV_HEAD = 8, 64
RWKV_W = RWKV_HEADS * RWKV_HEAD
RWKV_W_LORA, RWKV_A_LORA, RWKV_G_LORA = 64, 64, 128
RWKV_GN_EPS = 64e-5
RWKV_CHUNK = 128
GQA_HEADS, GQA_KV_HEADS, GQA_HEAD = 8, 2, 64
GQA_SCALE = GQA_HEAD ** -0.5
WIN_HEADS, WIN_KV_HEADS, WIN_HEAD, WINDOW = 8, 2, 64, 128
WIN_SCALE = WIN_HEAD ** -0.5
PEER_HEADS, PEER_N_KEYS, PEER_TOPK, PEER_DQ = 8, 128, 16, 128
N_BRANCH = 4

VMEM_LIMIT = 56 * 1024 * 1024


def _cparams(sem, vmem=None):
    return pltpu.CompilerParams(dimension_semantics=sem, vmem_limit_bytes=vmem or VMEM_LIMIT)


def _pick_tile(n, cap, mult=LANES):
    best = mult
    for t in range(mult, min(n, cap) + 1, mult):
        if n % t == 0:
            best = t
    return best


def _dot(a, b):
    return jnp.dot(a, b, preferred_element_type=F32)


def _dot_nt(a, b):
    return lax.dot_general(a, b, (((1,), (1,)), ((), ())), preferred_element_type=F32)


def _dot_tn(a, b):
    return lax.dot_general(a, b, (((0,), (0,)), ((), ())), preferred_element_type=F32)


ONES_LANE = 64


def _with_ones_lane(v):
    lane = lax.broadcasted_iota(I32, v.shape, v.ndim - 1)
    return jnp.where((lane & (LANES - 1)) == ONES_LANE, 1.0, v)


def _heads_t(x):
    return jnp.concatenate([x[:, h * LANES:(h + 1) * LANES].T for h in range(x.shape[1] // LANES)],
                           axis=0)


def _layer_norm(y, g, b):
    mu = jnp.mean(y, axis=-1, keepdims=True)
    d = y - mu
    var = jnp.mean(d * d, axis=-1, keepdims=True)
    return d * lax.rsqrt(var + LN_EPS) * g + b


def _swap_pairs(x):
    n = x.shape[-1]
    lane = lax.broadcasted_iota(I32, x.shape, x.ndim - 1)
    nxt = pltpu.roll(x, n - 1, x.ndim - 1)
    prv = pltpu.roll(x, 1, x.ndim - 1)
    return jnp.where((lane & 1) == 0, nxt, prv)


def _rope(x, cos, sin_signed):
    return x * cos + _swap_pairs(x) * sin_signed


def _ada_kernel(c_ref, w_ref, b_ref, o_ref):
    c = c_ref[...]
    s = c * jax.nn.sigmoid(c)
    o_ref[...] = _dot(s.astype(BF16), w_ref[...].astype(BF16)) + b_ref[...]


def _ada(cvec, w, b):
    m, k = cvec.shape
    nc = w.shape[1]
    tn = _pick_tile(nc, 1536)
    return pl.pallas_call(
        _ada_kernel,
        out_shape=jax.ShapeDtypeStruct((m, nc), F32),
        grid=(nc // tn,),
        in_specs=[pl.BlockSpec((m, k), lambda j: (0, 0)),
                  pl.BlockSpec((k, tn), lambda j: (0, j)),
                  pl.BlockSpec((1, tn), lambda j: (0, j))],
        out_specs=pl.BlockSpec((m, tn), lambda j: (0, j)),
        compiler_params=_cparams(("parallel",)),
        name="ada_mod",
    )(cvec, w, b.reshape(1, nc))


def _group_index(i, nb, nctx_b):
    return (i // nb) * 2 + (i % nb >= nctx_b).astype(I32)


def _mm_kernel(x_ref, mod_ref, w_ref, o_ref, *, sub, tm, nb, nctx_b):
    i = pl.program_id(1)
    parts = []
    for u in range(sub):
        m = mod_ref[_group_index(i * sub + u, nb, nctx_b)]
        xm = x_ref[u * tm:(u + 1) * tm, :] * m[0:1, :] + m[1:2, :]
        parts.append(xm.astype(BF16))
    o_ref[...] = _dot(jnp.concatenate(parts, axis=0), w_ref[...]).astype(o_ref.dtype)


def _modmm(x, mod, w, nb, nctx_b, tm, out_dtype=F32):
    n, k = x.shape
    nc = w.shape[1]
    tn = _pick_tile(nc, 2048)
    sub = max(s for s in (4, 2, 1) if (n // tm) % s == 0)
    kern = functools.partial(_mm_kernel, sub=sub, tm=tm, nb=nb, nctx_b=nctx_b)
    return pl.pallas_call(
        kern,
        out_shape=jax.ShapeDtypeStruct((n, nc), out_dtype),
        grid=(nc // tn, n // (tm * sub)),
        in_specs=[pl.BlockSpec((tm * sub, k), lambda j, i: (i, 0)),
                  pl.BlockSpec(mod.shape, lambda j, i: (0, 0, 0)),
                  pl.BlockSpec((k, tn), lambda j, i: (0, j))],
        out_specs=pl.BlockSpec((tm * sub, tn), lambda j, i: (i, j)),
        compiler_params=_cparams(("parallel", "parallel")),
        name="in_proj",
    )(x, mod, w)


def _mla_prep_kernel(p_ref, qn_ref, kvn_ref, wq_ref, wk_ref, wv_ref, cos_ref, sin_ref,
                     qt_ref, k_ref, vt_ref):
    p = p_ref[...]
    dq = p[:, 0:MLA_Q_RANK]
    dkv = p[:, MLA_Q_RANK:MLA_Q_RANK + MLA_KV_RANK]
    krp = p[:, MLA_Q_RANK + MLA_KV_RANK:]
    qn = dq * lax.rsqrt(jnp.mean(dq * dq, axis=-1, keepdims=True) + RMS_EPS) * qn_ref[...]
    kvn = dkv * lax.rsqrt(jnp.mean(dkv * dkv, axis=-1, keepdims=True) + RMS_EPS) * kvn_ref[...]
    kvn = kvn.astype(BF16)
    cos = cos_ref[...]
    sin = sin_ref[...]
    cos_h = jnp.concatenate([cos] * MLA_HEADS, axis=1)
    sin_h = jnp.concatenate([sin] * MLA_HEADS, axis=1)
    q = _rope(_dot(qn.astype(BF16), wq_ref[...]), cos_h, sin_h) * MLA_SCALE
    kr = _rope(krp, cos, sin)
    k = _dot(kvn, wk_ref[...]) + jnp.concatenate([kr] * MLA_HEADS, axis=1)
    qt_ref[0] = _heads_t(q).astype(BF16)
    k_ref[...] = k.astype(BF16)
    vt_ref[0, 0] = _heads_t(_with_ones_lane(_dot(kvn, wv_ref[...]))).astype(BF16)


def _mla_prep(pm, qn, kvn, wq, wk, wv, cos, sin, nb, tm):
    n = pm.shape[0]
    hw = MLA_HEADS * LANES
    bsz = n // (nb * tm)
    full = lambda a: pl.BlockSpec(a.shape, lambda i: (0,) * a.ndim)
    return pl.pallas_call(
        _mla_prep_kernel,
        out_shape=(jax.ShapeDtypeStruct((bsz, hw, nb * tm), BF16),
                   jax.ShapeDtypeStruct((n, hw), BF16),
                   jax.ShapeDtypeStruct((bsz, nb, hw, tm), BF16)),
        grid=(n // tm,),
        in_specs=[pl.BlockSpec((tm, pm.shape[1]), lambda i: (i, 0)),
                  full(qn), full(kvn), full(wq), full(wk), full(wv),
                  pl.BlockSpec((tm, LANES), lambda i: (i % nb, 0)),
                  pl.BlockSpec((tm, LANES), lambda i: (i % nb, 0))],
        out_specs=(pl.BlockSpec((1, hw, tm), lambda i: (i // nb, 0, i % nb)),
                   pl.BlockSpec((tm, hw), lambda i: (i, 0)),
                   pl.BlockSpec((1, 1, hw, tm), lambda i: (i // nb, i % nb, 0, 0))),
        compiler_params=_cparams(("parallel",)),
        name="mla_prep",
    )(pm, qn, kvn, wq, wk, wv, cos, sin)


def _gqa_prep_kernel(p_ref, qn_ref, kn_ref, cos_ref, sin_ref, q_ref, k_ref, v_ref,
                     *, n_q, n_kv, hd, scale, qk_norm, transposed):
    p = p_ref[...].astype(F32)
    cos = cos_ref[...]
    sin = sin_ref[...]

    def head(j, gain):
        x = p[:, j * LANES:(j + 1) * LANES]
        if qk_norm:
            ms = jnp.sum(x * x, axis=-1, keepdims=True) * (1.0 / hd)
            x = x * lax.rsqrt(ms + RMS_EPS) * gain
        return _rope(x, cos, sin)

    qg = qn_ref[...]
    kg = kn_ref[...]
    q = jnp.concatenate([head(j, qg) for j in range(n_q)], axis=1) * scale
    k = jnp.concatenate([head(n_q + j, kg) for j in range(n_kv)], axis=1)
    v = _with_ones_lane(p[:, (n_q + n_kv) * LANES:])
    k_ref[...] = k.astype(BF16)
    if transposed:
        q_ref[0] = _heads_t(q).astype(BF16)
        v_ref[0, 0] = _heads_t(v).astype(BF16)
    else:
        q_ref[...] = q.astype(BF16)
        v_ref[...] = v.astype(BF16)


def _gqa_prep(pg, qn, kn, cos, sin, nb, tm, *, n_q, n_kv, hd, scale, qk_norm, transposed):
    n = pg.shape[0]
    bsz = n // (nb * tm)
    full = lambda a: pl.BlockSpec(a.shape, lambda i: (0,) * a.ndim)
    kern = functools.partial(_gqa_prep_kernel, n_q=n_q, n_kv=n_kv, hd=hd, scale=scale,
                             qk_norm=qk_norm, transposed=transposed)
    if transposed:
        q_shape = jax.ShapeDtypeStruct((bsz, n_q * LANES, nb * tm), BF16)
        v_shape = jax.ShapeDtypeStruct((bsz, nb, n_kv * LANES, tm), BF16)
        q_spec = pl.BlockSpec((1, n_q * LANES, tm), lambda i: (i // nb, 0, i % nb))
        v_spec = pl.BlockSpec((1, 1, n_kv * LANES, tm), lambda i: (i // nb, i % nb, 0, 0))
    else:
        q_shape = jax.ShapeDtypeStruct((n, n_q * LANES), BF16)
        v_shape = jax.ShapeDtypeStruct((n, n_kv * LANES), BF16)
        q_spec = pl.BlockSpec((tm, n_q * LANES), lambda i: (i, 0))
        v_spec = pl.BlockSpec((tm, n_kv * LANES), lambda i: (i, 0))
    return pl.pallas_call(
        kern,
        out_shape=(q_shape, jax.ShapeDtypeStruct((n, n_kv * LANES), BF16), v_shape),
        grid=(n // tm,),
        in_specs=[pl.BlockSpec((tm, pg.shape[1]), lambda i: (i, 0)),
                  full(qn), full(kn),
                  pl.BlockSpec((tm, LANES), lambda i: (i % nb, 0)),
                  pl.BlockSpec((tm, LANES), lambda i: (i % nb, 0))],
        out_specs=(q_spec, pl.BlockSpec((tm, n_kv * LANES), lambda i: (i, 0)), v_spec),
        compiler_params=_cparams(("parallel",)),
        name="gqa_prep",
    )(pg, qn, kn, cos, sin)


ATTN_GROUPS_IN_FLIGHT = 8


def _attn_kernel(qt_ref, k_ref, vt_ref, o_ref, m_s, acc_s, *, n_q, n_kv, tq, tk, nctx_q,
                 nk_ctx, nk_all):
    qi = pl.program_id(1)
    nkb = jnp.where(qi < nctx_q, nk_ctx, nk_all)
    grp = n_q // n_kv
    gs = range(n_kv)
    m_s[...] = jnp.full(m_s.shape, NEG_INF, F32)
    acc_s[...] = jnp.zeros(acc_s.shape, F32)

    def body(kb, carry):
        ks = pl.multiple_of(kb * tk, tk)
        for g0 in range(0, n_kv, ATTN_GROUPS_IN_FLIGHT):
            gb = range(g0, min(g0 + ATTN_GROUPS_IN_FLIGHT, n_kv))
            s, m_old, m_new, p, a, pv = {}, {}, {}, {}, {}, {}
            for g in gb:
                qtg = jnp.concatenate(
                    [qt_ref[0, (g * grp + j) * LANES:(g * grp + j + 1) * LANES, :]
                     for j in range(grp)], axis=1)
                s[g] = _dot(k_ref[0, pl.ds(ks, tk), g * LANES:(g + 1) * LANES], qtg)
            for g in gb:
                m_old[g] = m_s[g]
                m_new[g] = jnp.maximum(m_old[g], jnp.max(s[g], axis=0, keepdims=True))
            for g in gb:
                p[g] = jnp.exp(s[g] - m_new[g]).astype(BF16)
                a[g] = jnp.exp(m_old[g] - m_new[g])
            for g in gb:
                pv[g] = _dot(vt_ref[0, kb, g * LANES:(g + 1) * LANES, :], p[g])
            for g in gb:
                m_s[g] = m_new[g]
                acc_s[g] = a[g] * acc_s[g] + pv[g]
        return carry

    lax.fori_loop(0, nkb, body, 0)
    row = lax.broadcasted_iota(I32, (LANES, 1), 0)
    for g in gs:
        acc = acc_s[g]
        ot = jnp.where(row < ONES_LANE, acc / acc[ONES_LANE:ONES_LANE + 1, :], 0.0)
        for a in range(grp):
            h = g * grp + a
            o_ref[0, :, h * LANES:(h + 1) * LANES] = ot[:, a * tq:(a + 1) * tq].T.astype(o_ref.dtype)


def _attention(qt, k, vt, ctx_len, *, n_q, n_kv, tq):
    b, l, _ = k.shape
    tk = vt.shape[-1]
    rows = (n_q // n_kv) * tq
    kern = functools.partial(_attn_kernel, n_q=n_q, n_kv=n_kv, tq=tq, tk=tk,
                             nctx_q=ctx_len // tq, nk_ctx=ctx_len // tk, nk_all=l // tk)
    return pl.pallas_call(
        kern,
        out_shape=jax.ShapeDtypeStruct((b, l, n_q * LANES), BF16),
        grid=(b, l // tq),
        in_specs=[pl.BlockSpec((1, n_q * LANES, tq), lambda bi, i: (bi, 0, i)),
                  pl.BlockSpec((1, l, n_kv * LANES), lambda bi, i: (bi, 0, 0)),
                  pl.BlockSpec((1, l // tk, n_kv * LANES, tk), lambda bi, i: (bi, 0, 0, 0))],
        out_specs=pl.BlockSpec((1, tq, n_q * LANES), lambda bi, i: (bi, i, 0)),
        scratch_shapes=[pltpu.VMEM((n_kv, 1, rows), F32), pltpu.VMEM((n_kv, LANES, rows), F32)],
        compiler_params=_cparams(("parallel", "arbitrary")),
        name="dense_attn",
    )(qt, k, vt)


def _win_kernel(sink_ref, q_ref, kc_ref, kp_ref, kcur_ref, kn_ref, vc_ref, vp_ref, vcur_ref,
                vn_ref, o_ref, *, n_q, n_kv, nctx_b, nb, ctx_len):
    qi = pl.program_id(1)
    w = WINDOW
    is_lat = (qi >= nctx_b).astype(I32)
    prev_ok = is_lat * (qi - 1 >= nctx_b).astype(I32)
    next_ok = is_lat * (qi + 1 < nb).astype(I32)
    nk = ctx_len + 3 * w
    r = lax.broadcasted_iota(I32, (w, nk), 0)
    c2 = lax.broadcasted_iota(I32, (w, nk), 1)
    c = c2 - ctx_len
    near = jnp.abs(r - (c - w)) <= w
    blk_ok = jnp.where(c < w, prev_ok, jnp.where(c < 2 * w, is_lat, next_ok)) > 0
    valid = jnp.logical_or(c2 < ctx_len, jnp.logical_and(near, blk_ok))
    bias = jnp.where(valid, 0.0, NEG_INF).astype(F32)
    grp = n_q // n_kv
    bias = jnp.concatenate([bias] * grp, axis=0)
    for g in range(n_kv):
        ls = slice(g * LANES, (g + 1) * LANES)
        qg = jnp.concatenate(
            [q_ref[0, :, (g * grp + a) * LANES:(g * grp + a + 1) * LANES] for a in range(grp)],
            axis=0)
        kcat = jnp.concatenate([kc_ref[0, :, ls], kp_ref[0, :, ls], kcur_ref[0, :, ls],
                                kn_ref[0, :, ls]], axis=0)
        vcat = jnp.concatenate([vc_ref[0, :, ls], vp_ref[0, :, ls], vcur_ref[0, :, ls],
                                vn_ref[0, :, ls]], axis=0)
        s = _dot_nt(qg, kcat) + bias
        sk = jnp.concatenate([jnp.full((w, 1), sink_ref[g * grp + a], F32) for a in range(grp)],
                             axis=0)
        m = jnp.maximum(jnp.max(s, axis=1, keepdims=True), sk)
        p = jnp.exp(s - m)
        den = jnp.sum(p, axis=1, keepdims=True) + jnp.exp(sk - m)
        o = _dot(p.astype(BF16), vcat) / den
        for a in range(grp):
            h = g * grp + a
            o_ref[0, :, h * LANES:(h + 1) * LANES] = o[a * w:(a + 1) * w].astype(o_ref.dtype)


def _window_attention(q, k, v, sink, ctx_len, *, n_q, n_kv):
    b, l, _ = q.shape
    w = WINDOW
    nb = l // w
    nctx_b = ctx_len // w
    kw = n_kv * LANES
    kern = functools.partial(_win_kernel, n_q=n_q, n_kv=n_kv, nctx_b=nctx_b, nb=nb,
                             ctx_len=ctx_len)
    ctx_spec = pl.BlockSpec((1, ctx_len, kw), lambda bi, i: (bi, 0, 0))
    prev_spec = pl.BlockSpec((1, w, kw), lambda bi, i: (bi, jnp.maximum(i - 1, 0), 0))
    cur_spec = pl.BlockSpec((1, w, kw), lambda bi, i: (bi, i, 0))
    next_spec = pl.BlockSpec((1, w, kw), lambda bi, i: (bi, jnp.minimum(i + 1, nb - 1), 0))
    return pl.pallas_call(
        kern,
        out_shape=jax.ShapeDtypeStruct((b, l, n_q * LANES), BF16),
        grid=(b, nb),
        in_specs=[pl.BlockSpec(memory_space=pltpu.SMEM),
                  pl.BlockSpec((1, w, n_q * LANES), lambda bi, i: (bi, i, 0)),
                  ctx_spec, prev_spec, cur_spec, next_spec,
                  ctx_spec, prev_spec, cur_spec, next_spec],
        out_specs=pl.BlockSpec((1, w, n_q * LANES), lambda bi, i: (bi, i, 0)),
        compiler_params=_cparams(("parallel", "arbitrary")),
        name="window_attn",
    )(sink, q, k, k, k, k, v, v, v, v)


def _rwkv_feat_kernel(p_ref, hp_ref, hn_ref, mu_ref, w0_ref, w2_ref, a0_ref, a2_ref, g2_ref,
                      kk_ref, ka_ref, rk_ref,
                      r_out, v_out, kk_out, g_out, bonus_out, lw_out, k_out, b_out,
                      *, tm, nb, nctx_b):
    i = pl.program_id(0) % nb
    seq_start = jnp.logical_or(i == 0, i == nctx_b)
    seq_end = jnp.logical_or(i == nctx_b - 1, i == nb - 1)
    p = p_ref[...]
    row = lax.broadcasted_iota(I32, p.shape, 0)
    first = jnp.where(seq_start, 0.0, hp_ref[7:8, :])
    last = jnp.where(seq_end, 0.0, hn_ref[0:1, :])
    prev = jnp.where(row == 0, first, pltpu.roll(p, 1, 0))
    nxt = jnp.where(row == tm - 1, last, pltpu.roll(p, tm - 1, 0))
    mu = mu_ref[...]
    ps = p + mu[0:1, :] * (prev - p) + mu[1:2, :] * (nxt - p)
    hw = RWKV_HEADS * LANES
    r = ps[:, 0:hw]
    k = ps[:, hw:2 * hw]
    v = ps[:, 2 * hw:3 * hw]
    wfb = ps[:, 3 * hw:3 * hw + LANES]
    afb = ps[:, 3 * hw + LANES:3 * hw + 2 * LANES]
    gi = ps[:, 3 * hw + 2 * LANES:]
    kkr = k * kk_ref[...]
    parts = []
    for h in range(RWKV_HEADS):
        x = kkr[:, h * LANES:(h + 1) * LANES]
        nrm = jnp.sqrt(jnp.sum(x * x, axis=-1, keepdims=True))
        parts.append(x / jnp.maximum(nrm, 1e-12))
    kk = jnp.concatenate(parts, axis=1)
    z = w0_ref[...] + _dot(jnp.tanh(wfb).astype(BF16), w2_ref[...])
    lw = -math.exp(-0.5) * jax.nn.sigmoid(z)
    a = jax.nn.sigmoid(a0_ref[...] + _dot(afb.astype(BF16), a2_ref[...]))
    ka = ka_ref[...]
    k0 = k * (1.0 + (a[:, 0:hw] - 1.0) * ka)
    k1 = k * (1.0 + (a[:, hw:] - 1.0) * ka)
    rkk = r * (k0 + k1) * rk_ref[...]
    bparts = []
    for h in range(RWKV_HEADS):
        sl = slice(h * LANES, (h + 1) * LANES)
        bparts.append(jnp.sum(rkk[:, sl], axis=-1, keepdims=True) * v[:, sl])
    r_out[...] = r.astype(r_out.dtype)
    v_out[...] = v.astype(v_out.dtype)
    kk_out[...] = kk.astype(kk_out.dtype)
    g_out[...] = _dot(jax.nn.sigmoid(gi).astype(BF16), g2_ref[...]).astype(g_out.dtype)
    bonus_out[...] = jnp.concatenate(bparts, axis=1).astype(bonus_out.dtype)
    lw_out[0] = lw[:, 0:hw]
    lw_out[1] = lw[:, hw:]
    k_out[0] = k0.astype(k_out.dtype)
    k_out[1] = k1.astype(k_out.dtype)
    b_out[0] = (a[:, 0:hw] * kk).astype(b_out.dtype)
    b_out[1] = (a[:, hw:] * kk).astype(b_out.dtype)


def _rwkv_features(pr, mu, w0, w2, a0, a2, g2, kk, ka, rk, nb, nctx_b, tm):
    n, wid = pr.shape
    hw = RWKV_HEADS * LANES
    full = lambda a: pl.BlockSpec(a.shape, lambda i: (0,) * a.ndim)
    kern = functools.partial(_rwkv_feat_kernel, tm=tm, nb=nb, nctx_b=nctx_b)
    one = jax.ShapeDtypeStruct((n, hw), BF16)
    two = jax.ShapeDtypeStruct((2, n, hw), BF16)
    two_f32 = jax.ShapeDtypeStruct((2, n, hw), F32)
    s1 = pl.BlockSpec((tm, hw), lambda i: (i, 0))
    s2 = pl.BlockSpec((2, tm, hw), lambda i: (0, i, 0))
    t8 = tm // 8
    return pl.pallas_call(
        kern,
        out_shape=(one, one, one, one, one, two_f32, two, two),
        grid=(n // tm,),
        in_specs=[pl.BlockSpec((tm, wid), lambda i: (i, 0)),
                  pl.BlockSpec((8, wid), lambda i: (jnp.maximum(i * t8 - 1, 0), 0)),
                  pl.BlockSpec((8, wid), lambda i: (jnp.minimum((i + 1) * t8, n // 8 - 1), 0)),
                  full(mu), full(w0), full(w2), full(a0), full(a2), full(g2),
                  full(kk), full(ka), full(rk)],
        out_specs=(s1, s1, s1, s1, s1, s2, s2, s2),
        compiler_params=_cparams(("parallel",)),
        name="rwkv_features",
    )(pr, pr, pr, mu, w0, w2, a0, a2, g2, kk, ka, rk)


def _split3_dot(mask_bf16, x):
    x1 = x.astype(BF16)
    r1 = x - x1.astype(F32)
    x2 = r1.astype(BF16)
    x3 = (r1 - x2.astype(F32)).astype(BF16)
    return _dot(mask_bf16, x1) + _dot(mask_bf16, x2) + _dot(mask_bf16, x3)


def _rwkv_scan_kernel(r_ref, v_ref, kk_ref, lw_ref, k_ref, b_ref, o_ref, s_ref):
    d = pl.program_id(1)
    c = pl.program_id(2)
    cs = RWKV_CHUNK

    @pl.when(c == 0)
    def _():
        s_ref[...] = jnp.zeros_like(s_ref)

    rev = d == 1
    t_i = lax.broadcasted_iota(I32, (cs, cs), 0)
    s_i = lax.broadcasted_iota(I32, (cs, cs), 1)
    order = jnp.where(rev, t_i - s_i, s_i - t_i)
    incl = order <= 0
    strict = order < 0
    eye = jnp.where(t_i == s_i, 1.0, 0.0).astype(F32)
    off_masks = []
    for lvl in range(int(math.log2(cs))):
        pair = (t_i >> (lvl + 1)) == (s_i >> (lvl + 1))
        half = (t_i >> lvl) != (s_i >> lvl)
        off = jnp.logical_and(jnp.logical_and(pair, half), strict)
        off_masks.append(jnp.where(off, 1.0, 0.0).astype(F32))
    lw = lw_ref[0]
    cum = _split3_dot(jnp.where(incl, 1.0, 0.0).astype(BF16), lw)
    p_in = jnp.exp(cum)
    p_inv = jnp.exp(-cum)
    p_ex = jnp.exp(cum - lw)
    tot = jnp.where(rev, cum[0:1, :], cum[cs - 1:cs, :])
    p_all = jnp.exp(tot)
    a_t = -kk_ref[...].astype(F32) * p_ex
    r_t = r_ref[...].astype(F32) * p_in
    k_t = k_ref[0].astype(F32) * p_inv
    b_t = b_ref[0].astype(F32) * p_inv
    v = v_ref[...].astype(F32)
    hs = range(RWKV_HEADS)
    sls = [slice(h * LANES, (h + 1) * LANES) for h in hs]
    vb = [v[:, sl].astype(BF16) for sl in sls]
    bk = [jnp.concatenate([b_t[:, sl], k_t[:, sl]], axis=0).astype(BF16) for sl in sls]
    gm = [_dot_nt(jnp.concatenate([a_t[:, sls[h]], r_t[:, sls[h]]], axis=0).astype(BF16), bk[h])
          for h in hs]
    m_ab = [jnp.where(strict, g[0:cs, 0:cs], 0.0) for g in gm]
    mkv = [_dot(jnp.where(strict, gm[h][0:cs, cs:], 0.0).astype(BF16), vb[h]) for h in hs]
    x = [eye + m * off_masks[0] for m in m_ab]
    for lvl in range(1, len(off_masks)):
        xb = [xx.astype(BF16) for xx in x]
        t1 = [_dot(xb[h], (m_ab[h] * off_masks[lvl]).astype(BF16)).astype(BF16) for h in hs]
        x = [x[h] + _dot(t1[h], xb[h]) for h in hs]
    z = [_dot(x[h].astype(BF16),
              jnp.concatenate([a_t[:, sls[h]], mkv[h]], axis=1).astype(BF16)) for h in hs]
    zb = [zz.astype(BF16) for zz in z]
    gy = [_dot(jnp.where(incl, gm[h][cs:, 0:cs], 0.0).astype(BF16), zb[h]) for h in hs]
    y0 = [_dot(jnp.where(incl, gm[h][cs:, cs:], 0.0).astype(BF16), vb[h]) + gy[h][:, LANES:]
          for h in hs]
    s0 = [s_ref[h] for h in hs]
    s0b = [s.astype(BF16) for s in s0]
    u = [_dot_nt(zb[h][:, 0:LANES], s0b[h]) + z[h][:, LANES:] for h in hs]
    for h in hs:
        g_mat = r_t[:, sls[h]] + gy[h][:, 0:LANES]
        o_ref[0, :, sls[h]] = _dot_nt(g_mat.astype(BF16), s0b[h]) + y0[h]
    for h in hs:
        uv = jnp.concatenate([u[h], v[:, sls[h]]], axis=0).astype(BF16)
        s_ref[h] = (s0[h] + _dot_tn(uv, bk[h])) * p_all[:, sls[h]]


def _rwkv_scan(r, v, kk, lw, k, bb, bsz, nctx_c):
    n, hw = r.shape
    cs = RWKV_CHUNK
    nc = n // bsz // cs

    def blk(bi, d, c):
        rc = jnp.where(c < nctx_c, nctx_c - 1 - c, nc - 1 - (c - nctx_c))
        return bi * nc + jnp.where(d == 0, c, rc)

    s1 = pl.BlockSpec((cs, hw), lambda bi, d, c: (blk(bi, d, c), 0))
    s2 = pl.BlockSpec((1, cs, hw), lambda bi, d, c: (d, blk(bi, d, c), 0))
    return pl.pallas_call(
        _rwkv_scan_kernel,
        out_shape=jax.ShapeDtypeStruct((2, n, hw), F32),
        grid=(bsz, 2, nc),
        in_specs=[s1, s1, s1, s2, s2, s2],
        out_specs=s2,
        scratch_shapes=[pltpu.VMEM((RWKV_HEADS, LANES, LANES), F32)],
        compiler_params=_cparams(("parallel", "parallel", "arbitrary")),
        name="rwkv_scan",
    )(r, v, kk, lw, k, bb)


def _rwkv_out_kernel(o_ref, bonus_ref, g_ref, lng_ref, lnb_ref, y_ref):
    o = o_ref[0] + o_ref[1]
    lane = lax.broadcasted_iota(I32, (1, LANES), 1)
    real = lane < RWKV_HEAD
    lng = lng_ref[...]
    lnb = lnb_ref[...]
    parts = []
    for h in range(RWKV_HEADS):
        sl = slice(h * LANES, (h + 1) * LANES)
        x = o[:, sl]
        mu = jnp.sum(x, axis=-1, keepdims=True) * (1.0 / RWKV_HEAD)
        dlt = jnp.where(real, x - mu, 0.0)
        var = jnp.sum(dlt * dlt, axis=-1, keepdims=True) * (1.0 / RWKV_HEAD)
        parts.append(dlt * lax.rsqrt(var + RWKV_GN_EPS) * lng[:, sl] + lnb[:, sl])
    y = (jnp.concatenate(parts, axis=1) + bonus_ref[...].astype(F32)) * g_ref[...].astype(F32)
    y_ref[...] = y.astype(y_ref.dtype)


def _rwkv_out(o, bonus, g, lng, lnb, tm):
    _, n, hw = o.shape
    full = lambda a: pl.BlockSpec(a.shape, lambda i: (0,) * a.ndim)
    s1 = pl.BlockSpec((tm, hw), lambda i: (i, 0))
    return pl.pallas_call(
        _rwkv_out_kernel,
        out_shape=jax.ShapeDtypeStruct((n, hw), BF16),
        grid=(n // tm,),
        in_specs=[pl.BlockSpec((2, tm, hw), lambda i: (0, i, 0)), s1, s1, full(lng), full(lnb)],
        out_specs=s1,
        compiler_params=_cparams(("parallel",)),
        name="rwkv_out",
    )(o, bonus, g, lng, lnb)


def _merge_kernel(ya_ref, yb_ref, yc_ref, yd_ref, gate_ref, x_ref, gt_ref, wb_ref, wo_ref,
                  lng_ref, lnb_ref, o_ref, *, alpha, d):
    ys = (ya_ref, yb_ref, yc_ref, yd_ref)
    acc = None
    for i in range(N_BRANCH):
        gate = jax.nn.sigmoid(gate_ref[:, i * d:(i + 1) * d].astype(F32))
        term = gate * _dot(ys[i][...], wb_ref[i])
        acc = term if acc is None else acc + term
    mix = _dot(acc.astype(BF16), wo_ref[...])
    y = alpha * x_ref[...] + gt_ref[0] * mix
    o_ref[...] = _layer_norm(y, lng_ref[...], lnb_ref[...])


def _merge(ya, yb, yc, yd, gate, x, gt, wb, wo, lng, lnb, nb, nctx_b, tm, alpha, latent_only):
    n, d = x.shape
    hw = ya.shape[1]
    full = lambda a: pl.BlockSpec(a.shape, lambda i: (0,) * a.ndim)
    if latent_only:
        nbl = nb - nctx_b
        n_out = n // nb * nbl
        src = lambda i: (i // nbl) * nb + nctx_b + i % nbl
    else:
        n_out = n
        src = lambda i: i
    sy = pl.BlockSpec((tm, hw), lambda i: (src(i), 0))
    kern = functools.partial(_merge_kernel, alpha=alpha, d=d)
    return pl.pallas_call(
        kern,
        out_shape=jax.ShapeDtypeStruct((n_out, d), F32),
        grid=(n_out // tm,),
        in_specs=[sy, sy, sy, sy,
                  pl.BlockSpec((tm, N_BRANCH * d), lambda i: (src(i), 0)),
                  pl.BlockSpec((tm, d), lambda i: (src(i), 0)),
                  pl.BlockSpec((1, 1, d), lambda i: (_group_index(src(i), nb, nctx_b), 0, 0)),
                  full(wb), full(wo), full(lng), full(lnb)],
        out_specs=pl.BlockSpec((tm, d), lambda i: (i, 0)),
        compiler_params=_cparams(("parallel",)),
        name="merge",
    )(ya, yb, yc, yd, gate, x, gt, wb, wo, lng, lnb)


def _extract_topk(srcs, n_rows, k, val_refs, pos_refs):
    rio = lax.broadcasted_iota(I32, srcs[0].shape, 0).astype(F32)
    js = range(len(srcs))
    for rnk in range(k):
        m = [jnp.max(s, axis=0, keepdims=True) for s in srcs]
        pos = [jnp.min(jnp.where(srcs[j] == m[j], rio, float(n_rows)), axis=0, keepdims=True)
               for j in js]
        for j in js:
            val_refs[j][rnk:rnk + 1, :] = m[j]
            pos_refs[j][rnk:rnk + 1, :] = pos[j]
        srcs = [jnp.where(rio == pos[j], -jnp.inf, srcs[j]) for j in js]


PEER_CAND_COUNTS = tuple(PEER_TOPK // (a + 1) for a in range(PEER_TOPK))
PEER_N_CAND = sum(PEER_CAND_COUNTS)
PEER_CAND_ROWS = -(-PEER_N_CAND // 8) * 8


PEER_HEADS_PER_STEP = 4


def _peer_topk_kernel(x_ref, mod_ref, wq_ref, k1_ref, k2_ref, h_ref, idx_ref, wgt_ref, off_ref,
                      q_s, v1_s, i1_s, v2_s, i2_s, cand_s, cidx_s, best_s, pos_s):
    step = pl.program_id(1)
    hps = range(PEER_HEADS_PER_STEP)

    @pl.when(step == 0)
    def _():
        m = mod_ref[0]
        hh = x_ref[...] * m[0:1, :] + m[1:2, :]
        h_ref[...] = hh
        q = _dot(hh.astype(BF16), wq_ref[...])
        for a in range(PEER_HEADS):
            q_s[a] = q[:, a * LANES:(a + 1) * LANES].astype(BF16)

    qh = [q_s[step * PEER_HEADS_PER_STEP + j] for j in hps]
    scores = [_dot_nt(k1_ref[j], qh[j]) for j in hps] + [_dot_nt(k2_ref[j], qh[j]) for j in hps]
    _extract_topk(scores, PEER_N_KEYS, PEER_TOPK,
                  [v1_s.at[j] for j in hps] + [v2_s.at[j] for j in hps],
                  [i1_s.at[j] for j in hps] + [i2_s.at[j] for j in hps])
    for j in hps:
        row = 0
        for a, cnt in enumerate(PEER_CAND_COUNTS):
            cand_s[j, row:row + cnt, :] = v1_s[j, a:a + 1, :] + v2_s[j, 0:cnt, :]
            cidx_s[j, row:row + cnt, :] = (i1_s[j, a:a + 1, :] * float(PEER_N_KEYS)
                                           + i2_s[j, 0:cnt, :])
            row += cnt
        pad = PEER_CAND_ROWS - PEER_N_CAND
        if pad:
            cand_s[j, PEER_N_CAND:, :] = jnp.full((pad, cand_s.shape[2]), -jnp.inf, F32)
            cidx_s[j, PEER_N_CAND:, :] = jnp.zeros((pad, cand_s.shape[2]), F32)
    _extract_topk([cand_s[j] for j in hps], PEER_CAND_ROWS, PEER_TOPK,
                  [best_s.at[j] for j in hps], [pos_s.at[j] for j in hps])
    for j in hps:
        cidx = cidx_s[j]
        rio = lax.broadcasted_iota(I32, cidx.shape, 0).astype(F32)
        lo = None
        for rnk in range(PEER_TOPK):
            sel = rio == pos_s[j, rnk:rnk + 1, :]
            e_id = jnp.max(jnp.where(sel, cidx, -1.0), axis=0, keepdims=True).astype(I32)
            idx_ref[j, rnk:rnk + 1, :] = e_id
            off = (e_id >> 1) * 8
            if rnk % 2 == 0:
                lo = off
            else:
                off_ref[j, rnk // 2:rnk // 2 + 1, :] = lo | (off << 16)
        best = best_s[j]
        e = jnp.exp(best - best[0:1, :])
        wgt_ref[j] = e / jnp.sum(e, axis=0, keepdims=True)


def _peer_topk(x, mod, wq, k1, k2, nb, nctx_b, tm):
    n, d = x.shape
    tk = PEER_TOPK
    hp = PEER_HEADS_PER_STEP
    full = lambda a: pl.BlockSpec(a.shape, lambda i, h: (0,) * a.ndim)
    return pl.pallas_call(
        _peer_topk_kernel,
        out_shape=(jax.ShapeDtypeStruct((n, d), F32),
                   jax.ShapeDtypeStruct((PEER_HEADS, tk, n), I32),
                   jax.ShapeDtypeStruct((PEER_HEADS, tk, n), F32),
                   jax.ShapeDtypeStruct((PEER_HEADS, tk // 2, n), I32)),
        grid=(n // tm, PEER_HEADS // hp),
        in_specs=[pl.BlockSpec((tm, d), lambda i, h: (i, 0)),
                  pl.BlockSpec((1, 2, d), lambda i, h: (_group_index(i, nb, nctx_b), 0, 0)),
                  full(wq),
                  pl.BlockSpec((hp, PEER_N_KEYS, LANES), lambda i, h: (h, 0, 0)),
                  pl.BlockSpec((hp, PEER_N_KEYS, LANES), lambda i, h: (h, 0, 0))],
        out_specs=(pl.BlockSpec((tm, d), lambda i, h: (i, 0)),
                   pl.BlockSpec((hp, tk, tm), lambda i, h: (h, 0, i)),
                   pl.BlockSpec((hp, tk, tm), lambda i, h: (h, 0, i)),
                   pl.BlockSpec((hp, tk // 2, tm), lambda i, h: (h, 0, i))),
        scratch_shapes=[pltpu.VMEM((PEER_HEADS, tm, LANES), BF16),
                        pltpu.VMEM((hp, tk, tm), F32), pltpu.VMEM((hp, tk, tm), F32),
                        pltpu.VMEM((hp, tk, tm), F32), pltpu.VMEM((hp, tk, tm), F32),
                        pltpu.VMEM((hp, PEER_CAND_ROWS, tm), F32),
                        pltpu.VMEM((hp, PEER_CAND_ROWS, tm), F32),
                        pltpu.VMEM((hp, tk, tm), F32), pltpu.VMEM((hp, tk, tm), F32)],
        compiler_params=_cparams(("parallel", "arbitrary")),
        name="peer_topk",
    )(x, mod, wq, k1, k2)


PEER_NE = PEER_HEADS * PEER_TOPK
PEER_COLS = PEER_NE * 16
PEER_TOK_UNROLL = 16


def _table_spec(tab):
    return pl.BlockSpec(tab.shape, lambda i: (0, 0), pipeline_mode=pl.Buffered(1))


def _gather_view(off_ref, tab_v, t):
    tiles = []
    tok_ref = off_ref.at[pl.ds(t * (PEER_NE // 2), PEER_NE // 2)]
    for j in range(PEER_NE // 2):
        w = tok_ref[j]
        o0 = pl.multiple_of(w & 0xFFFF, 8)
        o1 = pl.multiple_of(lax.shift_right_logical(w, 16), 8)
        tiles.append(tab_v[pl.ds(o0, 8), :])
        tiles.append(tab_v[pl.ds(o1, 8), :])
    return pltpu.bitcast(jnp.concatenate(tiles, axis=0), BF16)


def _select_mask(px_row):
    shp = (8, PEER_COLS)
    row = lax.broadcasted_iota(I32, shp, 0)
    col = lax.broadcasted_iota(I32, shp, 1)
    sub = (col >> 1) & 7
    fixed = jnp.logical_and((col & 1) == (row >> 2), (sub & 3) == (row & 3))
    return jnp.logical_and(fixed, (sub >> 2).astype(F32) == px_row)


def _split2(x):
    x1 = x.astype(BF16)
    return x1, (x - x1.astype(F32)).astype(BF16)


def _peer_u_kernel(off_ref, h_ref, par_ref, wgt_ref, e16_ref, g16_ref, tab_v, c_ref,
                   px_s, d_s, *, tb):
    px_s[...] = _dot(par_ref[...].astype(BF16), e16_ref[...])

    def tokens(i, carry):
        for u in range(PEER_TOK_UNROLL):
            t = i * PEER_TOK_UNROLL + u
            view = _gather_view(off_ref, tab_v, t)
            h1, h2 = _split2(h_ref[t])
            dd = _dot_nt(jnp.concatenate([h1, h2], axis=0), view)
            dd = jnp.where(_select_mask(px_s[pl.ds(t, 1), :]), dd[0:8] + dd[8:16], 0.0)
            d_s[pl.ds(t, 1), :] = jnp.sum(dd, axis=0, keepdims=True)
        return carry

    lax.fori_loop(0, tb // PEER_TOK_UNROLL, tokens, 0)
    d1, d2 = _split2(d_s[...])
    g16 = g16_ref[...]
    act = _dot(d1, g16) + _dot(d2, g16)
    gelu = 0.5 * act * (1.0 + lax.erf(act * (2.0 ** -0.5)))
    c_ref[...] = gelu * wgt_ref[...]


def _peer_u(off_flat, h3, par, wgt, e16, g16, tab, tb):
    n = h3.shape[0]
    full = lambda a: pl.BlockSpec(a.shape, lambda i: (0,) * a.ndim)
    kern = functools.partial(_peer_u_kernel, tb=tb)
    tok = pl.BlockSpec((tb, PEER_NE), lambda i: (i, 0))
    return pl.pallas_call(
        kern,
        out_shape=jax.ShapeDtypeStruct((n, PEER_NE), F32),
        grid=(n // tb,),
        in_specs=[pl.BlockSpec((tb * PEER_NE // 2,), lambda i: (i,), memory_space=pltpu.SMEM),
                  pl.BlockSpec((tb, 8, LANES), lambda i: (i, 0, 0)),
                  tok, tok, full(e16), full(g16), _table_spec(tab)],
        out_specs=tok,
        scratch_shapes=[pltpu.VMEM((tb, PEER_COLS), F32), pltpu.VMEM((tb, PEER_COLS), F32)],
        compiler_params=_cparams(("arbitrary",)),
        name="peer_u",
    )(off_flat, h3, par, wgt, e16, g16, tab)


def _peer_v_kernel(off_ref, c_ref, par_ref, e16_ref, tab_v, o_ref, px_s, c1_s, c2_s, *, tb):
    e16 = e16_ref[...]
    px_s[...] = _dot(par_ref[...].astype(BF16), e16)
    c1, c2 = _split2(c_ref[...])
    c1_s[...] = _dot(c1, e16)
    c2_s[...] = _dot(c2, e16)

    def tokens(i, carry):
        for u in range(PEER_TOK_UNROLL):
            t = i * PEER_TOK_UNROLL + u
            view = _gather_view(off_ref, tab_v, t)
            sel = _select_mask(px_s[pl.ds(t, 1), :])
            lhs = jnp.concatenate([jnp.where(sel, c1_s[pl.ds(t, 1), :], 0.0),
                                   jnp.where(sel, c2_s[pl.ds(t, 1), :], 0.0)], axis=0)
            out = _dot(lhs.astype(BF16), view)
            o_ref[t] = out[0:8] + out[8:16]
        return carry

    lax.fori_loop(0, tb // PEER_TOK_UNROLL, tokens, 0)


def _peer_v(off_flat, cw, par, e16, tab, tb):
    n = cw.shape[0]
    full = lambda a: pl.BlockSpec(a.shape, lambda i: (0,) * a.ndim)
    kern = functools.partial(_peer_v_kernel, tb=tb)
    tok = pl.BlockSpec((tb, PEER_NE), lambda i: (i, 0))
    return pl.pallas_call(
        kern,
        out_shape=jax.ShapeDtypeStruct((n, 8, LANES), F32),
        grid=(n // tb,),
        in_specs=[pl.BlockSpec((tb * PEER_NE // 2,), lambda i: (i,), memory_space=pltpu.SMEM),
                  tok, tok, full(e16), _table_spec(tab)],
        out_specs=pl.BlockSpec((tb, 8, LANES), lambda i: (i, 0, 0)),
        scratch_shapes=[pltpu.VMEM((tb, PEER_COLS), F32), pltpu.VMEM((tb, PEER_COLS), F32),
                        pltpu.VMEM((tb, PEER_COLS), F32)],
        compiler_params=_cparams(("arbitrary",)),
        name="peer_v",
    )(off_flat, cw, par, e16, tab)


def _pack_table(tab):
    e, d = tab.shape
    bits = lax.bitcast_convert_type(tab.astype(BF16), jnp.uint16).astype(U32)
    words = bits[:, :d // 2] | (bits[:, d // 2:] << 16)
    return words.reshape(e * 4, LANES)


def _ln_res_kernel(x_ref, f_ref, gt_ref, lng_ref, lnb_ref, o_ref, *, alpha):
    y = alpha * x_ref[...] + gt_ref[0] * f_ref[...]
    o_ref[...] = _layer_norm(y, lng_ref[...], lnb_ref[...])


def _ln_res(x, f, gt, lng, lnb, nb, nctx_b, tm, alpha):
    n, d = x.shape
    full = lambda a: pl.BlockSpec(a.shape, lambda i: (0,) * a.ndim)
    s1 = pl.BlockSpec((tm, d), lambda i: (i, 0))
    return pl.pallas_call(
        functools.partial(_ln_res_kernel, alpha=alpha),
        out_shape=jax.ShapeDtypeStruct((n, d), F32),
        grid=(n // tm,),
        in_specs=[s1, s1,
                  pl.BlockSpec((1, 1, d), lambda i: (_group_index(i, nb, nctx_b), 0, 0)),
                  full(lng), full(lnb)],
        out_specs=s1,
        compiler_params=_cparams(("parallel",)),
        name="ln_res",
    )(x, f, gt, lng, lnb)


def _pad_heads(w, n_heads, hd):
    lead = w.shape[:-1]
    w = w.reshape(lead + (n_heads, hd))
    w = jnp.pad(w, [(0, 0)] * len(lead) + [(0, 0), (0, LANES - hd)])
    return w.reshape(lead + (n_heads * LANES,))


def _pad_head_rows(w, n_heads, hd):
    d = w.shape[-1]
    w = w.reshape(n_heads, hd, d)
    w = jnp.pad(w, [(0, 0), (0, LANES - hd), (0, 0)])
    return w.reshape(n_heads * LANES, d)


def _rope_tables(rows, rot_dim, lane_off, ctx_len):
    r_idx = jnp.repeat(jnp.arange(rows), GRID_W).astype(F32)
    c_idx = jnp.tile(jnp.arange(GRID_W), rows).astype(F32)
    n = rot_dim // 4
    inv = ROPE_THETA ** (-jnp.arange(n, dtype=F32) / n)
    ang = jnp.concatenate([r_idx[:, None] * inv, c_idx[:, None] * inv], axis=-1)
    cos = jnp.repeat(jnp.cos(ang), 2, axis=-1)
    sin = jnp.repeat(jnp.sin(ang), 2, axis=-1) * jnp.tile(jnp.array([-1.0, 1.0], F32), rot_dim // 2)
    s = ang.shape[0]
    cos_t = jnp.ones((ctx_len + s, LANES), F32).at[ctx_len:, lane_off:lane_off + rot_dim].set(cos)
    sin_t = jnp.zeros((ctx_len + s, LANES), F32).at[ctx_len:, lane_off:lane_off + rot_dim].set(sin)
    return cos_t, sin_t


def _split_cols(w, widths):
    out, start = [], 0
    for wd in widths:
        out.append(w[..., start:start + wd])
        start += wd
    return out


def _forward(x, c, ctx, c_ctx, ada_w, ada_b, w_in, mla_q_norm, mla_kv_norm, mla_w_uq, mla_w_ukv,
             rwkv_mu, rwkv_w0, rwkv_w2, rwkv_a0, rwkv_a2, rwkv_g2, rwkv_k_k, rwkv_k_a, rwkv_r_k,
             rwkv_ln_g, rwkv_ln_b, gqa_q_norm, gqa_k_norm, win_sink, w_branch, w_out, ln1_g, ln1_b,
             peer_wq, peer_keys, peer_u, peer_v, ln2_g, ln2_b):
    stages = []
    bsz, seq, d = x.shape
    ctx_len = ctx.shape[1]
    depth = ada_w.shape[0]
    alpha = (2 * depth) ** 0.25
    l_tot = ctx_len + seq
    n = bsz * l_tot
    tm = 256
    tm_feat = 128
    assert ctx_len % tm == 0 and seq % tm == 0 and seq % GRID_W == 0
    nb = l_tot // tm
    nctx_b = ctx_len // tm
    rows = seq // GRID_W
    ne = PEER_HEADS * PEER_TOPK

    xs = jnp.concatenate([ctx, x], axis=1).reshape(n, d)
    cos_m, sin_m = _rope_tables(rows, MLA_ROPE, MLA_NOPE, ctx_len)
    cos_h, sin_h = _rope_tables(rows, GQA_HEAD, 0, ctx_len)

    m_rows = 16
    cvec = jnp.zeros((m_rows, d), F32).at[:bsz].set(c).at[bsz].set(c_ctx)
    mla_in = MLA_Q_RANK + MLA_KV_RANK + MLA_ROPE
    rwkv_in = 3 * RWKV_W + 2 * RWKV_W_LORA + 2 * RWKV_A_LORA + RWKV_G_LORA
    gqa_in = (GQA_HEADS + 2 * GQA_KV_HEADS) * GQA_HEAD
    win_in = (WIN_HEADS + 2 * WIN_KV_HEADS) * WIN_HEAD
    rw_widths = (RWKV_W, RWKV_W, RWKV_W, RWKV_W_LORA, RWKV_W_LORA, RWKV_A_LORA, RWKV_A_LORA,
                 RWKV_G_LORA)

    col16 = jnp.arange(PEER_COLS)
    e16 = (col16[None, :] // 16 == jnp.arange(ne)[:, None]).astype(BF16)
    g16 = e16.T

    for lyr in range(depth):
        last = lyr == depth - 1
        mod = _ada(cvec, ada_w[lyr], ada_b[lyr])
        chunks = [mod[:, i * d:(i + 1) * d] for i in range(6)]

        def table(ch):
            lat = ch[:bsz]
            cx = jnp.broadcast_to(ch[bsz][None], (bsz, d))
            return jnp.stack([cx, lat], axis=1).reshape(bsz * 2, d)

        sh1, sc1, gt1, sh2, sc2, gt2 = [table(ch) for ch in chunks]
        mod1 = jnp.stack([1.0 + sc1, sh1], axis=1)
        mod2 = jnp.stack([1.0 + sc2, sh2], axis=1)
        gt1 = gt1[:, None, :]
        gt2 = gt2[:, None, :]

        wi = w_in[lyr]
        w_mla, w_rw, w_gq, w_wn, w_gate = _split_cols(wi, (mla_in, rwkv_in, gqa_in, win_in, N_BRANCH * d))
        zc = lambda k: jnp.zeros((d, k), F32)
        w_mla_p = jnp.concatenate([w_mla[:, :MLA_Q_RANK + MLA_KV_RANK], zc(MLA_NOPE),
                                   w_mla[:, MLA_Q_RANK + MLA_KV_RANK:], zc(LANES - MLA_NOPE - MLA_ROPE)],
                                  axis=1)
        rr, rk_, rv, rwf, rwb, raf, rab, rgi = _split_cols(w_rw, rw_widths)
        hp = lambda w: _pad_heads(w, RWKV_HEADS, RWKV_HEAD)
        w_rw_p = jnp.concatenate([hp(rr), hp(rk_), hp(rv), rwf, rwb, raf, rab, rgi], axis=1)
        mu_parts = _split_cols(rwkv_mu[lyr], rw_widths)
        mu_p = jnp.concatenate([hp(mu_parts[0]), hp(mu_parts[1]), hp(mu_parts[2])] + mu_parts[3:], axis=1)

        def gqa_cols(w, nq, nkv, hd):
            q_, k_, v_ = _split_cols(w, (nq * hd, nkv * hd, nkv * hd))
            return jnp.concatenate([_pad_heads(q_, nq, hd), _pad_heads(k_, nkv, hd),
                                    _pad_heads(v_, nkv, hd)], axis=1)

        w_gq_p = gqa_cols(w_gq, GQA_HEADS, GQA_KV_HEADS, GQA_HEAD)
        w_wn_p = gqa_cols(w_wn, WIN_HEADS, WIN_KV_HEADS, WIN_HEAD)

        pm = _modmm(xs, mod1, w_mla_p.astype(BF16), nb, nctx_b, tm)
        pr = _modmm(xs, mod1, w_rw_p.astype(BF16), nb, nctx_b, tm)
        pg = _modmm(xs, mod1, w_gq_p.astype(BF16), nb, nctx_b, tm, BF16)
        pw = _modmm(xs, mod1, w_wn_p.astype(BF16), nb, nctx_b, tm, BF16)
        gate = _modmm(xs, mod1, w_gate.astype(BF16), nb, nctx_b, tm, BF16)

        uq = mla_w_uq[lyr].reshape(MLA_Q_RANK, MLA_HEADS, MLA_NOPE + MLA_ROPE)
        uq = jnp.pad(uq, [(0, 0), (0, 0), (0, LANES - MLA_NOPE - MLA_ROPE)]).reshape(MLA_Q_RANK, -1)
        ukv = mla_w_ukv[lyr].reshape(MLA_KV_RANK, MLA_HEADS, MLA_NOPE + MLA_V)
        uk = jnp.pad(ukv[:, :, :MLA_NOPE], [(0, 0), (0, 0), (0, LANES - MLA_NOPE)]).reshape(MLA_KV_RANK, -1)
        uv = jnp.pad(ukv[:, :, MLA_NOPE:], [(0, 0), (0, 0), (0, LANES - MLA_V)]).reshape(MLA_KV_RANK, -1)
        qa, ka, va = _mla_prep(pm, mla_q_norm[lyr][None], mla_kv_norm[lyr][None], uq.astype(BF16),
                               uk.astype(BF16), uv.astype(BF16), cos_m, sin_m, nb, tm)
        r3 = lambda a: a.reshape(bsz, l_tot, a.shape[-1])
        ya = _attention(qa, r3(ka), va, ctx_len, n_q=MLA_HEADS, n_kv=MLA_HEADS, tq=256)

        zl = jnp.zeros((RWKV_W_LORA, RWKV_HEADS * LANES), F32)
        w2c = jnp.concatenate([jnp.concatenate([hp(rwkv_w2[lyr, 0]), zl], axis=1),
                               jnp.concatenate([zl, hp(rwkv_w2[lyr, 1])], axis=1)], axis=0)
        a2c = jnp.concatenate([jnp.concatenate([hp(rwkv_a2[lyr, 0]), zl], axis=1),
                               jnp.concatenate([zl, hp(rwkv_a2[lyr, 1])], axis=1)], axis=0)
        w0c = jnp.concatenate([hp(rwkv_w0[lyr, 0]), hp(rwkv_w0[lyr, 1])])[None]
        a0c = jnp.concatenate([hp(rwkv_a0[lyr, 0]), hp(rwkv_a0[lyr, 1])])[None]
        feats = _rwkv_features(pr, mu_p, w0c, w2c.astype(BF16), a0c, a2c.astype(BF16),
                               hp(rwkv_g2[lyr]).astype(BF16), hp(rwkv_k_k[lyr])[None],
                               hp(rwkv_k_a[lyr])[None], hp(rwkv_r_k[lyr].reshape(-1))[None],
                               l_tot // tm_feat, ctx_len // tm_feat, tm_feat)
        f_r, f_v, f_kk, f_g, f_bonus, f_lw, f_k, f_b = feats
        o_scan = _rwkv_scan(f_r, f_v, f_kk, f_lw, f_k, f_b, bsz, ctx_len // RWKV_CHUNK)
        yb = _rwkv_out(o_scan, f_bonus, f_g, hp(rwkv_ln_g[lyr])[None], hp(rwkv_ln_b[lyr])[None], tm)

        pad_g = lambda g: jnp.pad(g, (0, LANES - g.shape[0]))[None]
        qc, kc, vc = _gqa_prep(pg, pad_g(gqa_q_norm[lyr]), pad_g(gqa_k_norm[lyr]), cos_h, sin_h,
                               nb, tm, n_q=GQA_HEADS, n_kv=GQA_KV_HEADS, hd=GQA_HEAD,
                               scale=GQA_SCALE, qk_norm=True, transposed=True)
        yc = _attention(qc, r3(kc), vc, ctx_len, n_q=GQA_HEADS, n_kv=GQA_KV_HEADS, tq=256)

        ones_g = jnp.ones((1, LANES), F32)
        qd, kd, vd = _gqa_prep(pw, ones_g, ones_g, cos_h, sin_h, nb, tm, n_q=WIN_HEADS,
                               n_kv=WIN_KV_HEADS, hd=WIN_HEAD, scale=WIN_SCALE, qk_norm=False,
                               transposed=False)
        yd = _window_attention(r3(qd), r3(kd), r3(vd), win_sink[lyr], ctx_len,
                               n_q=WIN_HEADS, n_kv=WIN_KV_HEADS)

        wb = jnp.stack([_pad_head_rows(w_branch[lyr, i], 8, 64) for i in range(N_BRANCH)]).astype(BF16)
        x_mid = _merge(ya.reshape(n, -1), yb, yc.reshape(n, -1), yd.reshape(n, -1), gate, xs, gt1,
                       wb, w_out[lyr].astype(BF16), ln1_g[lyr][None], ln1_b[lyr][None],
                       nb, nctx_b, tm, alpha, last)
        if last:
            n_p, nb_p, nctx_p = bsz * seq, seq // tm, 0
        else:
            n_p, nb_p, nctx_p = n, nb, nctx_b

        keys = peer_keys[lyr]
        half = PEER_DQ // 2
        k1 = jnp.pad(keys[:, 0], [(0, 0), (0, 0), (0, LANES - half)]).astype(BF16)
        k2 = jnp.pad(keys[:, 1], [(0, 0), (0, 0), (LANES - half, 0)]).astype(BF16)
        h_in, idx_t, wgt_t, off_t = _peer_topk(x_mid, mod2, peer_wq[lyr].astype(BF16), k1, k2,
                                               nb_p, nctx_p, tm)
        idx = idx_t.reshape(ne, n_p).T
        wgt = wgt_t.reshape(ne, n_p).T
        off_flat = off_t.reshape(ne // 2, n_p).T.reshape(-1)
        par = (idx & 1).astype(F32)
        tb = 128
        cw = _peer_u(off_flat, h_in.reshape(n_p, 8, LANES), par, wgt, e16, g16,
                     _pack_table(peer_u[lyr]), tb)
        ffn = _peer_v(off_flat, cw, par, e16, _pack_table(peer_v[lyr]), tb).reshape(n_p, d)
        xs = _ln_res(x_mid, ffn, gt2, ln2_g[lyr][None], ln2_b[lyr][None], nb_p, nctx_p, tm, alpha)
        stages.append(dict(ya=ya, yb=yb, yc=yc, yd=yd, x_mid=x_mid, idx=idx, wgt=wgt, ffn=ffn,
                           x_out=xs))

    return xs.reshape(bsz, seq, d), stages


def kernel(x, c, ctx, c_ctx, ada_w, ada_b, w_in, mla_q_norm, mla_kv_norm, mla_w_uq, mla_w_ukv,
           rwkv_mu, rwkv_w0, rwkv_w2, rwkv_a0, rwkv_a2, rwkv_g2, rwkv_k_k, rwkv_k_a, rwkv_r_k,
           rwkv_ln_g, rwkv_ln_b, gqa_q_norm, gqa_k_norm, win_sink, w_branch, w_out, ln1_g, ln1_b,
           peer_wq, peer_keys, peer_u, peer_v, ln2_g, ln2_b):
    out, _ = _forward(x, c, ctx, c_ctx, ada_w, ada_b, w_in, mla_q_norm, mla_kv_norm, mla_w_uq,
                      mla_w_ukv, rwkv_mu, rwkv_w0, rwkv_w2, rwkv_a0, rwkv_a2, rwkv_g2, rwkv_k_k,
                      rwkv_k_a, rwkv_r_k, rwkv_ln_g, rwkv_ln_b, gqa_q_norm, gqa_k_norm, win_sink,
                      w_branch, w_out, ln1_g, ln1_b, peer_wq, peer_keys, peer_u, peer_v, ln2_g,
                      ln2_b)
    return out
```

```python
import functools
import math

import jax
import jax.numpy as jnp
from jax import lax
from jax.experimental import pallas as pl
from jax.experimental.pallas import tpu as pltpu

F32 = jnp.float32
BF16 = jnp.bfloat16
I32 = jnp.int32
U32 = jnp.uint32

LANES = 128
GRID_W = 64
ROPE_THETA = 10000.0
NEG_INF = -1e30
LN_EPS = 1e-5
RMS_EPS = 1e-6

MLA_HEADS, MLA_Q_RANK, MLA_KV_RANK, MLA_NOPE, MLA_ROPE, MLA_V = 8, 256, 128, 64, 32, 64
MLA_SCALE = (MLA_NOPE + MLA_ROPE) ** -0.5
RWKV_HEADS, RWKV_HEAD = 8, 64
RWKV_W = RWKV_HEADS * RWKV_HEAD
RWKV_W_LORA, RWKV_A_LORA, RWKV_G_LORA = 64, 64, 128
RWKV_GN_EPS = 64e-5
RWKV_CHUNK = 128
GQA_HEADS, GQA_KV_HEADS, GQA_HEAD = 8, 2, 64
GQA_SCALE = GQA_HEAD ** -0.5
WIN_HEADS, WIN_KV_HEADS, WIN_HEAD, WINDOW = 8, 2, 64, 128
WIN_SCALE = WIN_HEAD ** -0.5
PEER_HEADS, PEER_N_KEYS, PEER_TOPK, PEER_DQ = 8, 128, 16, 128
N_BRANCH = 4

VMEM_LIMIT = 56 * 1024 * 1024


def _cparams(sem, vmem=None):
    return pltpu.CompilerParams(dimension_semantics=sem, vmem_limit_bytes=vmem or VMEM_LIMIT)


def _pick_tile(n, cap, mult=LANES):
    best = mult
    for t in range(mult, min(n, cap) + 1, mult):
        if n % t == 0:
            best = t
    return best


def _dot(a, b):
    return jnp.dot(a, b, preferred_element_type=F32)


def _dot_nt(a, b):
    return lax.dot_general(a, b, (((1,), (1,)), ((), ())), preferred_element_type=F32)


def _dot_tn(a, b):
    return lax.dot_general(a, b, (((0,), (0,)), ((), ())), preferred_element_type=F32)


ONES_LANE = 64


def _with_ones_lane(v):
    lane = lax.broadcasted_iota(I32, v.shape, v.ndim - 1)
    return jnp.where((lane & (LANES - 1)) == ONES_LANE, 1.0, v)


def _heads_t(x):
    return jnp.concatenate([x[:, h * LANES:(h + 1) * LANES].T for h in range(x.shape[1] // LANES)],
                           axis=0)


def _layer_norm(y, g, b):
    mu = jnp.mean(y, axis=-1, keepdims=True)
    d = y - mu
    var = jnp.mean(d * d, axis=-1, keepdims=True)
    return d * lax.rsqrt(var + LN_EPS) * g + b


def _swap_pairs(x):
    n = x.shape[-1]
    lane = lax.broadcasted_iota(I32, x.shape, x.ndim - 1)
    nxt = pltpu.roll(x, n - 1, x.ndim - 1)
    prv = pltpu.roll(x, 1, x.ndim - 1)
    return jnp.where((lane & 1) == 0, nxt, prv)


def _rope(x, cos, sin_signed):
    return x * cos + _swap_pairs(x) * sin_signed


def _ada_kernel(c_ref, w_ref, b_ref, o_ref):
    c = c_ref[...]
    s = c * jax.nn.sigmoid(c)
    o_ref[...] = _dot(s.astype(BF16), w_ref[...].astype(BF16)) + b_ref[...]


def _ada(cvec, w, b):
    m, k = cvec.shape
    nc = w.shape[1]
    tn = _pick_tile(nc, 1536)
    return pl.pallas_call(
        _ada_kernel,
        out_shape=jax.ShapeDtypeStruct((m, nc), F32),
        grid=(nc // tn,),
        in_specs=[pl.BlockSpec((m, k), lambda j: (0, 0)),
                  pl.BlockSpec((k, tn), lambda j: (0, j)),
                  pl.BlockSpec((1, tn), lambda j: (0, j))],
        out_specs=pl.BlockSpec((m, tn), lambda j: (0, j)),
        compiler_params=_cparams(("parallel",)),
        name="ada_mod",
    )(cvec, w, b.reshape(1, nc))


def _group_index(i, nb, nctx_b):
    return (i // nb) * 2 + (i % nb >= nctx_b).astype(I32)


def _mm_kernel(x_ref, mod_ref, w_ref, o_ref, *, sub, tm, nb, nctx_b):
    i = pl.program_id(1)
    parts = []
    for u in range(sub):
        m = mod_ref[_group_index(i * sub + u, nb, nctx_b)]
        xm = x_ref[u * tm:(u + 1) * tm, :] * m[0:1, :] + m[1:2, :]
        parts.append(xm.astype(BF16))
    o_ref[...] = _dot(jnp.concatenate(parts, axis=0), w_ref[...]).astype(o_ref.dtype)


def _modmm(x, mod, w, nb, nctx_b, tm, out_dtype=F32):
    n, k = x.shape
    nc = w.shape[1]
    tn = _pick_tile(nc, 2048)
    sub = max(s for s in (4, 2, 1) if (n // tm) % s == 0)
    kern = functools.partial(_mm_kernel, sub=sub, tm=tm, nb=nb, nctx_b=nctx_b)
    return pl.pallas_call(
        kern,
        out_shape=jax.ShapeDtypeStruct((n, nc), out_dtype),
        grid=(nc // tn, n // (tm * sub)),
        in_specs=[pl.BlockSpec((tm * sub, k), lambda j, i: (i, 0)),
                  pl.BlockSpec(mod.shape, lambda j, i: (0, 0, 0)),
                  pl.BlockSpec((k, tn), lambda j, i: (0, j))],
        out_specs=pl.BlockSpec((tm * sub, tn), lambda j, i: (i, j)),
        compiler_params=_cparams(("parallel", "parallel")),
        name="in_proj",
    )(x, mod, w)


def _mla_prep_kernel(p_ref, qn_ref, kvn_ref, wq_ref, wk_ref, wv_ref, cos_ref, sin_ref,
                     qt_ref, k_ref, vt_ref):
    p = p_ref[...]
    dq = p[:, 0:MLA_Q_RANK]
    dkv = p[:, MLA_Q_RANK:MLA_Q_RANK + MLA_KV_RANK]
    krp = p[:, MLA_Q_RANK + MLA_KV_RANK:]
    qn = dq * lax.rsqrt(jnp.mean(dq * dq, axis=-1, keepdims=True) + RMS_EPS) * qn_ref[...]
    kvn = dkv * lax.rsqrt(jnp.mean(dkv * dkv, axis=-1, keepdims=True) + RMS_EPS) * kvn_ref[...]
    kvn = kvn.astype(BF16)
    cos = cos_ref[...]
    sin = sin_ref[...]
    cos_h = jnp.concatenate([cos] * MLA_HEADS, axis=1)
    sin_h = jnp.concatenate([sin] * MLA_HEADS, axis=1)
    q = _rope(_dot(qn.astype(BF16), wq_ref[...]), cos_h, sin_h) * MLA_SCALE
    kr = _rope(krp, cos, sin)
    k = _dot(kvn, wk_ref[...]) + jnp.concatenate([kr] * MLA_HEADS, axis=1)
    qt_ref[0] = _heads_t(q).astype(BF16)
    k_ref[...] = k.astype(BF16)
    vt_ref[0, 0] = _heads_t(_with_ones_lane(_dot(kvn, wv_ref[...]))).astype(BF16)


def _mla_prep(pm, qn, kvn, wq, wk, wv, cos, sin, nb, tm):
    n = pm.shape[0]
    hw = MLA_HEADS * LANES
    bsz = n // (nb * tm)
    full = lambda a: pl.BlockSpec(a.shape, lambda i: (0,) * a.ndim)
    return pl.pallas_call(
        _mla_prep_kernel,
        out_shape=(jax.ShapeDtypeStruct((bsz, hw, nb * tm), BF16),
                   jax.ShapeDtypeStruct((n, hw), BF16),
                   jax.ShapeDtypeStruct((bsz, nb, hw, tm), BF16)),
        grid=(n // tm,),
        in_specs=[pl.BlockSpec((tm, pm.shape[1]), lambda i: (i, 0)),
                  full(qn), full(kvn), full(wq), full(wk), full(wv),
                  pl.BlockSpec((tm, LANES), lambda i: (i % nb, 0)),
                  pl.BlockSpec((tm, LANES), lambda i: (i % nb, 0))],
        out_specs=(pl.BlockSpec((1, hw, tm), lambda i: (i // nb, 0, i % nb)),
                   pl.BlockSpec((tm, hw), lambda i: (i, 0)),
                   pl.BlockSpec((1, 1, hw, tm), lambda i: (i // nb, i % nb, 0, 0))),
        compiler_params=_cparams(("parallel",)),
        name="mla_prep",
    )(pm, qn, kvn, wq, wk, wv, cos, sin)


def _gqa_prep_kernel(p_ref, qn_ref, kn_ref, cos_ref, sin_ref, q_ref, k_ref, v_ref,
                     *, n_q, n_kv, hd, scale, qk_norm, transposed):
    p = p_ref[...].astype(F32)
    cos = cos_ref[...]
    sin = sin_ref[...]

    def head(j, gain):
        x = p[:, j * LANES:(j + 1) * LANES]
        if qk_norm:
            ms = jnp.sum(x * x, axis=-1, keepdims=True) * (1.0 / hd)
            x = x * lax.rsqrt(ms + RMS_EPS) * gain
        return _rope(x, cos, sin)

    qg = qn_ref[...]
    kg = kn_ref[...]
    q = jnp.concatenate([head(j, qg) for j in range(n_q)], axis=1) * scale
    k = jnp.concatenate([head(n_q + j, kg) for j in range(n_kv)], axis=1)
    v = _with_ones_lane(p[:, (n_q + n_kv) * LANES:])
    k_ref[...] = k.astype(BF16)
    if transposed:
        q_ref[0] = _heads_t(q).astype(BF16)
        v_ref[0, 0] = _heads_t(v).astype(BF16)
    else:
        q_ref[...] = q.astype(BF16)
        v_ref[...] = v.astype(BF16)


def _gqa_prep(pg, qn, kn, cos, sin, nb, tm, *, n_q, n_kv, hd, scale, qk_norm, transposed):
    n = pg.shape[0]
    bsz = n // (nb * tm)
    full = lambda a: pl.BlockSpec(a.shape, lambda i: (0,) * a.ndim)
    kern = functools.partial(_gqa_prep_kernel, n_q=n_q, n_kv=n_kv, hd=hd, scale=scale,
                             qk_norm=qk_norm, transposed=transposed)
    if transposed:
        q_shape = jax.ShapeDtypeStruct((bsz, n_q * LANES, nb * tm), BF16)
        v_shape = jax.ShapeDtypeStruct((bsz, nb, n_kv * LANES, tm), BF16)
        q_spec = pl.BlockSpec((1, n_q * LANES, tm), lambda i: (i // nb, 0, i % nb))
        v_spec = pl.BlockSpec((1, 1, n_kv * LANES, tm), lambda i: (i // nb, i % nb, 0, 0))
    else:
        q_shape = jax.ShapeDtypeStruct((n, n_q * LANES), BF16)
        v_shape = jax.ShapeDtypeStruct((n, n_kv * LANES), BF16)
        q_spec = pl.BlockSpec((tm, n_q * LANES), lambda i: (i, 0))
        v_spec = pl.BlockSpec((tm, n_kv * LANES), lambda i: (i, 0))
    return pl.pallas_call(
        kern,
        out_shape=(q_shape, jax.ShapeDtypeStruct((n, n_kv * LANES), BF16), v_shape),
        grid=(n // tm,),
        in_specs=[pl.BlockSpec((tm, pg.shape[1]), lambda i: (i, 0)),
                  full(qn), full(kn),
                  pl.BlockSpec((tm, LANES), lambda i: (i % nb, 0)),
                  pl.BlockSpec((tm, LANES), lambda i: (i % nb, 0))],
        out_specs=(q_spec, pl.BlockSpec((tm, n_kv * LANES), lambda i: (i, 0)), v_spec),
        compiler_params=_cparams(("parallel",)),
        name="gqa_prep",
    )(pg, qn, kn, cos, sin)


ATTN_GROUPS_IN_FLIGHT = 8


def _attn_kernel(qt_ref, k_ref, vt_ref, o_ref, m_s, acc_s, *, n_q, n_kv, tq, tk, nctx_q,
                 nk_ctx, nk_all):
    qi = pl.program_id(1)
    nkb = jnp.where(qi < nctx_q, nk_ctx, nk_all)
    grp = n_q // n_kv
    gs = range(n_kv)
    m_s[...] = jnp.full(m_s.shape, NEG_INF, F32)
    acc_s[...] = jnp.zeros(acc_s.shape, F32)

    def body(kb, carry):
        ks = pl.multiple_of(kb * tk, tk)
        for g0 in range(0, n_kv, ATTN_GROUPS_IN_FLIGHT):
            gb = range(g0, min(g0 + ATTN_GROUPS_IN_FLIGHT, n_kv))
            s, m_old, m_new, p, a, pv = {}, {}, {}, {}, {}, {}
            for g in gb:
                qtg = jnp.concatenate(
                    [qt_ref[0, (g * grp + j) * LANES:(g * grp + j + 1) * LANES, :]
                     for j in range(grp)], axis=1)
                s[g] = _dot(k_ref[0, pl.ds(ks, tk), g * LANES:(g + 1) * LANES], qtg)
            for g in gb:
                m_old[g] = m_s[g]
                m_new[g] = jnp.maximum(m_old[g], jnp.max(s[g], axis=0, keepdims=True))
            for g in gb:
                p[g] = jnp.exp(s[g] - m_new[g]).astype(BF16)
                a[g] = jnp.exp(m_old[g] - m_new[g])
            for g in gb:
                pv[g] = _dot(vt_ref[0, kb, g * LANES:(g + 1) * LANES, :], p[g])
            for g in gb:
                m_s[g] = m_new[g]
                acc_s[g] = a[g] * acc_s[g] + pv[g]
        return carry

    lax.fori_loop(0, nkb, body, 0)
    row = lax.broadcasted_iota(I32, (LANES, 1), 0)
    for g in gs:
        acc = acc_s[g]
        ot = jnp.where(row < ONES_LANE, acc / acc[ONES_LANE:ONES_LANE + 1, :], 0.0)
        for a in range(grp):
            h = g * grp + a
            o_ref[0, :, h * LANES:(h + 1) * LANES] = ot[:, a * tq:(a + 1) * tq].T.astype(o_ref.dtype)


def _attention(qt, k, vt, ctx_len, *, n_q, n_kv, tq):
    b, l, _ = k.shape
    tk = vt.shape[-1]
    rows = (n_q // n_kv) * tq
    kern = functools.partial(_attn_kernel, n_q=n_q, n_kv=n_kv, tq=tq, tk=tk,
                             nctx_q=ctx_len // tq, nk_ctx=ctx_len // tk, nk_all=l // tk)
    return pl.pallas_call(
        kern,
        out_shape=jax.ShapeDtypeStruct((b, l, n_q * LANES), BF16),
        grid=(b, l // tq),
        in_specs=[pl.BlockSpec((1, n_q * LANES, tq), lambda bi, i: (bi, 0, i)),
                  pl.BlockSpec((1, l, n_kv * LANES), lambda bi, i: (bi, 0, 0)),
                  pl.BlockSpec((1, l // tk, n_kv * LANES, tk), lambda bi, i: (bi, 0, 0, 0))],
        out_specs=pl.BlockSpec((1, tq, n_q * LANES), lambda bi, i: (bi, i, 0)),
        scratch_shapes=[pltpu.VMEM((n_kv, 1, rows), F32), pltpu.VMEM((n_kv, LANES, rows), F32)],
        compiler_params=_cparams(("parallel", "arbitrary")),
        name="dense_attn",
    )(qt, k, vt)


def _win_kernel(sink_ref, q_ref, kc_ref, kp_ref, kcur_ref, kn_ref, vc_ref, vp_ref, vcur_ref,
                vn_ref, o_ref, *, n_q, n_kv, nctx_b, nb, ctx_len):
    qi = pl.program_id(1)
    w = WINDOW
    is_lat = (qi >= nctx_b).astype(I32)
    prev_ok = is_lat * (qi - 1 >= nctx_b).astype(I32)
    next_ok = is_lat * (qi + 1 < nb).astype(I32)
    nk = ctx_len + 3 * w
    r = lax.broadcasted_iota(I32, (w, nk), 0)
    c2 = lax.broadcasted_iota(I32, (w, nk), 1)
    c = c2 - ctx_len
    near = jnp.abs(r - (c - w)) <= w
    blk_ok = jnp.where(c < w, prev_ok, jnp.where(c < 2 * w, is_lat, next_ok)) > 0
    valid = jnp.logical_or(c2 < ctx_len, jnp.logical_and(near, blk_ok))
    bias = jnp.where(valid, 0.0, NEG_INF).astype(F32)
    grp = n_q // n_kv
    bias = jnp.concatenate([bias] * grp, axis=0)
    for g in range(n_kv):
        ls = slice(g * LANES, (g + 1) * LANES)
        qg = jnp.concatenate(
            [q_ref[0, :, (g * grp + a) * LANES:(g * grp + a + 1) * LANES] for a in range(grp)],
            axis=0)
        kcat = jnp.concatenate([kc_ref[0, :, ls], kp_ref[0, :, ls], kcur_ref[0, :, ls],
                                kn_ref[0, :, ls]], axis=0)
        vcat = jnp.concatenate([vc_ref[0, :, ls], vp_ref[0, :, ls], vcur_ref[0, :, ls],
                                vn_ref[0, :, ls]], axis=0)
        s = _dot_nt(qg, kcat) + bias
        sk = jnp.concatenate([jnp.full((w, 1), sink_ref[g * grp + a], F32) for a in range(grp)],
                             axis=0)
        m = jnp.maximum(jnp.max(s, axis=1, keepdims=True), sk)
        p = jnp.exp(s - m)
        den = jnp.sum(p, axis=1, keepdims=True) + jnp.exp(sk - m)
        o = _dot(p.astype(BF16), vcat) / den
        for a in range(grp):
            h = g * grp + a
            o_ref[0, :, h * LANES:(h + 1) * LANES] = o[a * w:(a + 1) * w].astype(o_ref.dtype)


def _window_attention(q, k, v, sink, ctx_len, *, n_q, n_kv):
    b, l, _ = q.shape
    w = WINDOW
    nb = l // w
    nctx_b = ctx_len // w
    kw = n_kv * LANES
    kern = functools.partial(_win_kernel, n_q=n_q, n_kv=n_kv, nctx_b=nctx_b, nb=nb,
                             ctx_len=ctx_len)
    ctx_spec = pl.BlockSpec((1, ctx_len, kw), lambda bi, i: (bi, 0, 0))
    prev_spec = pl.BlockSpec((1, w, kw), lambda bi, i: (bi, jnp.maximum(i - 1, 0), 0))
    cur_spec = pl.BlockSpec((1, w, kw), lambda bi, i: (bi, i, 0))
    next_spec = pl.BlockSpec((1, w, kw), lambda bi, i: (bi, jnp.minimum(i + 1, nb - 1), 0))
    return pl.pallas_call(
        kern,
        out_shape=jax.ShapeDtypeStruct((b, l, n_q * LANES), BF16),
        grid=(b, nb),
        in_specs=[pl.BlockSpec(memory_space=pltpu.SMEM),
                  pl.BlockSpec((1, w, n_q * LANES), lambda bi, i: (bi, i, 0)),
                  ctx_spec, prev_spec, cur_spec, next_spec,
                  ctx_spec, prev_spec, cur_spec, next_spec],
        out_specs=pl.BlockSpec((1, w, n_q * LANES), lambda bi, i: (bi, i, 0)),
        compiler_params=_cparams(("parallel", "arbitrary")),
        name="window_attn",
    )(sink, q, k, k, k, k, v, v, v, v)


def _rwkv_feat_kernel(p_ref, hp_ref, hn_ref, mu_ref, w0_ref, w2_ref, a0_ref, a2_ref, g2_ref,
                      kk_ref, ka_ref, rk_ref,
                      r_out, v_out, kk_out, g_out, bonus_out, lw_out, k_out, b_out,
                      *, tm, nb, nctx_b):
    i = pl.program_id(0) % nb
    seq_start = jnp.logical_or(i == 0, i == nctx_b)
    seq_end = jnp.logical_or(i == nctx_b - 1, i == nb - 1)
    p = p_ref[...]
    row = lax.broadcasted_iota(I32, p.shape, 0)
    first = jnp.where(seq_start, 0.0, hp_ref[7:8, :])
    last = jnp.where(seq_end, 0.0, hn_ref[0:1, :])
    prev = jnp.where(row == 0, first, pltpu.roll(p, 1, 0))
    nxt = jnp.where(row == tm - 1, last, pltpu.roll(p, tm - 1, 0))
    mu = mu_ref[...]
    ps = p + mu[0:1, :] * (prev - p) + mu[1:2, :] * (nxt - p)
    hw = RWKV_HEADS * LANES
    r = ps[:, 0:hw]
    k = ps[:, hw:2 * hw]
    v = ps[:, 2 * hw:3 * hw]
    wfb = ps[:, 3 * hw:3 * hw + LANES]
    afb = ps[:, 3 * hw + LANES:3 * hw + 2 * LANES]
    gi = ps[:, 3 * hw + 2 * LANES:]
    kkr = k * kk_ref[...]
    parts = []
    for h in range(RWKV_HEADS):
        x = kkr[:, h * LANES:(h + 1) * LANES]
        nrm = jnp.sqrt(jnp.sum(x * x, axis=-1, keepdims=True))
        parts.append(x / jnp.maximum(nrm, 1e-12))
    kk = jnp.concatenate(parts, axis=1)
    z = w0_ref[...] + _dot(jnp.tanh(wfb).astype(BF16), w2_ref[...])
    lw = -math.exp(-0.5) * jax.nn.sigmoid(z)
    a = jax.nn.sigmoid(a0_ref[...] + _dot(afb.astype(BF16), a2_ref[...]))
    ka = ka_ref[...]
    k0 = k * (1.0 + (a[:, 0:hw] - 1.0) * ka)
    k1 = k * (1.0 + (a[:, hw:] - 1.0) * ka)
    rkk = r * (k0 + k1) * rk_ref[...]
    bparts = []
    for h in range(RWKV_HEADS):
        sl = slice(h * LANES, (h + 1) * LANES)
        bparts.append(jnp.sum(rkk[:, sl], axis=-1, keepdims=True) * v[:, sl])
    r_out[...] = r.astype(r_out.dtype)
    v_out[...] = v.astype(v_out.dtype)
    kk_out[...] = kk.astype(kk_out.dtype)
    g_out[...] = _dot(jax.nn.sigmoid(gi).astype(BF16), g2_ref[...]).astype(g_out.dtype)
    bonus_out[...] = jnp.concatenate(bparts, axis=1).astype(bonus_out.dtype)
    lw_out[0] = lw[:, 0:hw]
    lw_out[1] = lw[:, hw:]
    k_out[0] = k0.astype(k_out.dtype)
    k_out[1] = k1.astype(k_out.dtype)
    b_out[0] = (a[:, 0:hw] * kk).astype(b_out.dtype)
    b_out[1] = (a[:, hw:] * kk).astype(b_out.dtype)


def _rwkv_features(pr, mu, w0, w2, a0, a2, g2, kk, ka, rk, nb, nctx_b, tm):
    n, wid = pr.shape
    hw = RWKV_HEADS * LANES
    full = lambda a: pl.BlockSpec(a.shape, lambda i: (0,) * a.ndim)
    kern = functools.partial(_rwkv_feat_kernel, tm=tm, nb=nb, nctx_b=nctx_b)
    one = jax.ShapeDtypeStruct((n, hw), BF16)
    two = jax.ShapeDtypeStruct((2, n, hw), BF16)
    two_f32 = jax.ShapeDtypeStruct((2, n, hw), F32)
    s1 = pl.BlockSpec((tm, hw), lambda i: (i, 0))
    s2 = pl.BlockSpec((2, tm, hw), lambda i: (0, i, 0))
    t8 = tm // 8
    return pl.pallas_call(
        kern,
        out_shape=(one, one, one, one, one, two_f32, two, two),
        grid=(n // tm,),
        in_specs=[pl.BlockSpec((tm, wid), lambda i: (i, 0)),
                  pl.BlockSpec((8, wid), lambda i: (jnp.maximum(i * t8 - 1, 0), 0)),
                  pl.BlockSpec((8, wid), lambda i: (jnp.minimum((i + 1) * t8, n // 8 - 1), 0)),
                  full(mu), full(w0), full(w2), full(a0), full(a2), full(g2),
                  full(kk), full(ka), full(rk)],
        out_specs=(s1, s1, s1, s1, s1, s2, s2, s2),
        compiler_params=_cparams(("parallel",)),
        name="rwkv_features",
    )(pr, pr, pr, mu, w0, w2, a0, a2, g2, kk, ka, rk)


def _split3_dot(mask_bf16, x):
    x1 = x.astype(BF16)
    r1 = x - x1.astype(F32)
    x2 = r1.astype(BF16)
    x3 = (r1 - x2.astype(F32)).astype(BF16)
    return _dot(mask_bf16, x1) + _dot(mask_bf16, x2) + _dot(mask_bf16, x3)


def _rwkv_scan_kernel(r_ref, v_ref, kk_ref, lw_ref, k_ref, b_ref, o_ref, s_ref):
    d = pl.program_id(1)
    c = pl.program_id(2)
    cs = RWKV_CHUNK

    @pl.when(c == 0)
    def _():
        s_ref[...] = jnp.zeros_like(s_ref)

    rev = d == 1
    t_i = lax.broadcasted_iota(I32, (cs, cs), 0)
    s_i = lax.broadcasted_iota(I32, (cs, cs), 1)
    order = jnp.where(rev, t_i - s_i, s_i - t_i)
    incl = order <= 0
    strict = order < 0
    eye = jnp.where(t_i == s_i, 1.0, 0.0).astype(F32)
    off_masks = []
    for lvl in range(int(math.log2(cs))):
        pair = (t_i >> (lvl + 1)) == (s_i >> (lvl + 1))
        half = (t_i >> lvl) != (s_i >> lvl)
        off = jnp.logical_and(jnp.logical_and(pair, half), strict)
        off_masks.append(jnp.where(off, 1.0, 0.0).astype(F32))
    lw = lw_ref[0]
    cum = _split3_dot(jnp.where(incl, 1.0, 0.0).astype(BF16), lw)
    p_in = jnp.exp(cum)
    p_inv = jnp.exp(-cum)
    p_ex = jnp.exp(cum - lw)
    tot = jnp.where(rev, cum[0:1, :], cum[cs - 1:cs, :])
    p_all = jnp.exp(tot)
    a_t = -kk_ref[...].astype(F32) * p_ex
    r_t = r_ref[...].astype(F32) * p_in
    k_t = k_ref[0].astype(F32) * p_inv
    b_t = b_ref[0].astype(F32) * p_inv
    v = v_ref[...].astype(F32)
    hs = range(RWKV_HEADS)
    sls = [slice(h * LANES, (h + 1) * LANES) for h in hs]
    vb = [v[:, sl].astype(BF16) for sl in sls]
    bk = [jnp.concatenate([b_t[:, sl], k_t[:, sl]], axis=0).astype(BF16) for sl in sls]
    gm = [_dot_nt(jnp.concatenate([a_t[:, sls[h]], r_t[:, sls[h]]], axis=0).astype(BF16), bk[h])
          for h in hs]
    m_ab = [jnp.where(strict, g[0:cs, 0:cs], 0.0) for g in gm]
    mkv = [_dot(jnp.where(strict, gm[h][0:cs, cs:], 0.0).astype(BF16), vb[h]) for h in hs]
    x = [eye + m * off_masks[0] for m in m_ab]
    for lvl in range(1, len(off_masks)):
        xb = [xx.astype(BF16) for xx in x]
        t1 = [_dot(xb[h], (m_ab[h] * off_masks[lvl]).astype(BF16)).astype(BF16) for h in hs]
        x = [x[h] + _dot(t1[h], xb[h]) for h in hs]
    z = [_dot(x[h].astype(BF16),
              jnp.concatenate([a_t[:, sls[h]], mkv[h]], axis=1).astype(BF16)) for h in hs]
    zb = [zz.astype(BF16) for zz in z]
    gy = [_dot(jnp.where(incl, gm[h][cs:, 0:cs], 0.0).astype(BF16), zb[h]) for h in hs]
    y0 = [_dot(jnp.where(incl, gm[h][cs:, cs:], 0.0).astype(BF16), vb[h]) + gy[h][:, LANES:]
          for h in hs]
    s0 = [s_ref[h] for h in hs]
    s0b = [s.astype(BF16) for s in s0]
    u = [_dot_nt(zb[h][:, 0:LANES], s0b[h]) + z[h][:, LANES:] for h in hs]
    for h in hs:
        g_mat = r_t[:, sls[h]] + gy[h][:, 0:LANES]
        o_ref[0, :, sls[h]] = _dot_nt(g_mat.astype(BF16), s0b[h]) + y0[h]
    for h in hs:
        uv = jnp.concatenate([u[h], v[:, sls[h]]], axis=0).astype(BF16)
        s_ref[h] = (s0[h] + _dot_tn(uv, bk[h])) * p_all[:, sls[h]]


def _rwkv_scan(r, v, kk, lw, k, bb, bsz, nctx_c):
    n, hw = r.shape
    cs = RWKV_CHUNK
    nc = n // bsz // cs

    def blk(bi, d, c):
        rc = jnp.where(c < nctx_c, nctx_c - 1 - c, nc - 1 - (c - nctx_c))
        return bi * nc + jnp.where(d == 0, c, rc)

    s1 = pl.BlockSpec((cs, hw), lambda bi, d, c: (blk(bi, d, c), 0))
    s2 = pl.BlockSpec((1, cs, hw), lambda bi, d, c: (d, blk(bi, d, c), 0))
    return pl.pallas_call(
        _rwkv_scan_kernel,
        out_shape=jax.ShapeDtypeStruct((2, n, hw), F32),
        grid=(bsz, 2, nc),
        in_specs=[s1, s1, s1, s2, s2, s2],
        out_specs=s2,
        scratch_shapes=[pltpu.VMEM((RWKV_HEADS, LANES, LANES), F32)],
        compiler_params=_cparams(("parallel", "parallel", "arbitrary")),
        name="rwkv_scan",
    )(r, v, kk, lw, k, bb)


def _rwkv_out_kernel(o_ref, bonus_ref, g_ref, lng_ref, lnb_ref, y_ref):
    o = o_ref[0] + o_ref[1]
    lane = lax.broadcasted_iota(I32, (1, LANES), 1)
    real = lane < RWKV_HEAD
    lng = lng_ref[...]
    lnb = lnb_ref[...]
    parts = []
    for h in range(RWKV_HEADS):
        sl = slice(h * LANES, (h + 1) * LANES)
        x = o[:, sl]
        mu = jnp.sum(x, axis=-1, keepdims=True) * (1.0 / RWKV_HEAD)
        dlt = jnp.where(real, x - mu, 0.0)
        var = jnp.sum(dlt * dlt, axis=-1, keepdims=True) * (1.0 / RWKV_HEAD)
        parts.append(dlt * lax.rsqrt(var + RWKV_GN_EPS) * lng[:, sl] + lnb[:, sl])
    y = (jnp.concatenate(parts, axis=1) + bonus_ref[...].astype(F32)) * g_ref[...].astype(F32)
    y_ref[...] = y.astype(y_ref.dtype)


def _rwkv_out(o, bonus, g, lng, lnb, tm):
    _, n, hw = o.shape
    full = lambda a: pl.BlockSpec(a.shape, lambda i: (0,) * a.ndim)
    s1 = pl.BlockSpec((tm, hw), lambda i: (i, 0))
    return pl.pallas_call(
        _rwkv_out_kernel,
        out_shape=jax.ShapeDtypeStruct((n, hw), BF16),
        grid=(n // tm,),
        in_specs=[pl.BlockSpec((2, tm, hw), lambda i: (0, i, 0)), s1, s1, full(lng), full(lnb)],
        out_specs=s1,
        compiler_params=_cparams(("parallel",)),
        name="rwkv_out",
    )(o, bonus, g, lng, lnb)


def _merge_kernel(ya_ref, yb_ref, yc_ref, yd_ref, gate_ref, x_ref, gt_ref, wb_ref, wo_ref,
                  lng_ref, lnb_ref, o_ref, *, alpha, d):
    ys = (ya_ref, yb_ref, yc_ref, yd_ref)
    acc = None
    for i in range(N_BRANCH):
        gate = jax.nn.sigmoid(gate_ref[:, i * d:(i + 1) * d].astype(F32))
        term = gate * _dot(ys[i][...], wb_ref[i])
        acc = term if acc is None else acc + term
    mix = _dot(acc.astype(BF16), wo_ref[...])
    y = alpha * x_ref[...] + gt_ref[0] * mix
    o_ref[...] = _layer_norm(y, lng_ref[...], lnb_ref[...])


def _merge(ya, yb, yc, yd, gate, x, gt, wb, wo, lng, lnb, nb, nctx_b, tm, alpha, latent_only):
    n, d = x.shape
    hw = ya.shape[1]
    full = lambda a: pl.BlockSpec(a.shape, lambda i: (0,) * a.ndim)
    if latent_only:
        nbl = nb - nctx_b
        n_out = n // nb * nbl
        src = lambda i: (i // nbl) * nb + nctx_b + i % nbl
    else:
        n_out = n
        src = lambda i: i
    sy = pl.BlockSpec((tm, hw), lambda i: (src(i), 0))
    kern = functools.partial(_merge_kernel, alpha=alpha, d=d)
    return pl.pallas_call(
        kern,
        out_shape=jax.ShapeDtypeStruct((n_out, d), F32),
        grid=(n_out // tm,),
        in_specs=[sy, sy, sy, sy,
                  pl.BlockSpec((tm, N_BRANCH * d), lambda i: (src(i), 0)),
                  pl.BlockSpec((tm, d), lambda i: (src(i), 0)),
                  pl.BlockSpec((1, 1, d), lambda i: (_group_index(src(i), nb, nctx_b), 0, 0)),
                  full(wb), full(wo), full(lng), full(lnb)],
        out_specs=pl.BlockSpec((tm, d), lambda i: (i, 0)),
        compiler_params=_cparams(("parallel",)),
        name="merge",
    )(ya, yb, yc, yd, gate, x, gt, wb, wo, lng, lnb)


def _extract_topk(srcs, n_rows, k, val_refs, pos_refs):
    rio = lax.broadcasted_iota(I32, srcs[0].shape, 0).astype(F32)
    js = range(len(srcs))
    for rnk in range(k):
        m = [jnp.max(s, axis=0, keepdims=True) for s in srcs]
        pos = [jnp.min(jnp.where(srcs[j] == m[j], rio, float(n_rows)), axis=0, keepdims=True)
               for j in js]
        for j in js:
            val_refs[j][rnk:rnk + 1, :] = m[j]
            pos_refs[j][rnk:rnk + 1, :] = pos[j]
        srcs = [jnp.where(rio == pos[j], -jnp.inf, srcs[j]) for j in js]


PEER_CAND_COUNTS = tuple(PEER_TOPK // (a + 1) for a in range(PEER_TOPK))
PEER_N_CAND = sum(PEER_CAND_COUNTS)
PEER_CAND_ROWS = -(-PEER_N_CAND // 8) * 8


PEER_HEADS_PER_STEP = 4


def _peer_topk_kernel(x_ref, mod_ref, wq_ref, k1_ref, k2_ref, h_ref, idx_ref, wgt_ref, off_ref,
                      q_s, v1_s, i1_s, v2_s, i2_s, cand_s, cidx_s, best_s, pos_s):
    step = pl.program_id(1)
    hps = range(PEER_HEADS_PER_STEP)

    @pl.when(step == 0)
    def _():
        m = mod_ref[0]
        hh = x_ref[...] * m[0:1, :] + m[1:2, :]
        h_ref[...] = hh
        q = _dot(hh.astype(BF16), wq_ref[...])
        for a in range(PEER_HEADS):
            q_s[a] = q[:, a * LANES:(a + 1) * LANES].astype(BF16)

    qh = [q_s[step * PEER_HEADS_PER_STEP + j] for j in hps]
    scores = [_dot_nt(k1_ref[j], qh[j]) for j in hps] + [_dot_nt(k2_ref[j], qh[j]) for j in hps]
    _extract_topk(scores, PEER_N_KEYS, PEER_TOPK,
                  [v1_s.at[j] for j in hps] + [v2_s.at[j] for j in hps],
                  [i1_s.at[j] for j in hps] + [i2_s.at[j] for j in hps])
    for j in hps:
        row = 0
        for a, cnt in enumerate(PEER_CAND_COUNTS):
            cand_s[j, row:row + cnt, :] = v1_s[j, a:a + 1, :] + v2_s[j, 0:cnt, :]
            cidx_s[j, row:row + cnt, :] = (i1_s[j, a:a + 1, :] * float(PEER_N_KEYS)
                                           + i2_s[j, 0:cnt, :])
            row += cnt
        pad = PEER_CAND_ROWS - PEER_N_CAND
        if pad:
            cand_s[j, PEER_N_CAND:, :] = jnp.full((pad, cand_s.shape[2]), -jnp.inf, F32)
            cidx_s[j, PEER_N_CAND:, :] = jnp.zeros((pad, cand_s.shape[2]), F32)
    _extract_topk([cand_s[j] for j in hps], PEER_CAND_ROWS, PEER_TOPK,
                  [best_s.at[j] for j in hps], [pos_s.at[j] for j in hps])
    for j in hps:
        cidx = cidx_s[j]
        rio = lax.broadcasted_iota(I32, cidx.shape, 0).astype(F32)
        lo = None
        for rnk in range(PEER_TOPK):
            sel = rio == pos_s[j, rnk:rnk + 1, :]
            e_id = jnp.max(jnp.where(sel, cidx, -1.0), axis=0, keepdims=True).astype(I32)
            idx_ref[j, rnk:rnk + 1, :] = e_id
            off = (e_id >> 1) * 8
            if rnk % 2 == 0:
                lo = off
            else:
                off_ref[j, rnk // 2:rnk // 2 + 1, :] = lo | (off << 16)
        best = best_s[j]
        e = jnp.exp(best - best[0:1, :])
        wgt_ref[j] = e / jnp.sum(e, axis=0, keepdims=True)


def _peer_topk(x, mod, wq, k1, k2, nb, nctx_b, tm):
    n, d = x.shape
    tk = PEER_TOPK
    hp = PEER_HEADS_PER_STEP
    full = lambda a: pl.BlockSpec(a.shape, lambda i, h: (0,) * a.ndim)
    return pl.pallas_call(
        _peer_topk_kernel,
        out_shape=(jax.ShapeDtypeStruct((n, d), F32),
                   jax.ShapeDtypeStruct((PEER_HEADS, tk, n), I32),
                   jax.ShapeDtypeStruct((PEER_HEADS, tk, n), F32),
                   jax.ShapeDtypeStruct((PEER_HEADS, tk // 2, n), I32)),
        grid=(n // tm, PEER_HEADS // hp),
        in_specs=[pl.BlockSpec((tm, d), lambda i, h: (i, 0)),
                  pl.BlockSpec((1, 2, d), lambda i, h: (_group_index(i, nb, nctx_b), 0, 0)),
                  full(wq),
                  pl.BlockSpec((hp, PEER_N_KEYS, LANES), lambda i, h: (h, 0, 0)),
                  pl.BlockSpec((hp, PEER_N_KEYS, LANES), lambda i, h: (h, 0, 0))],
        out_specs=(pl.BlockSpec((tm, d), lambda i, h: (i, 0)),
                   pl.BlockSpec((hp, tk, tm), lambda i, h: (h, 0, i)),
                   pl.BlockSpec((hp, tk, tm), lambda i, h: (h, 0, i)),
                   pl.BlockSpec((hp, tk // 2, tm), lambda i, h: (h, 0, i))),
        scratch_shapes=[pltpu.VMEM((PEER_HEADS, tm, LANES), BF16),
                        pltpu.VMEM((hp, tk, tm), F32), pltpu.VMEM((hp, tk, tm), F32),
                        pltpu.VMEM((hp, tk, tm), F32), pltpu.VMEM((hp, tk, tm), F32),
                        pltpu.VMEM((hp, PEER_CAND_ROWS, tm), F32),
                        pltpu.VMEM((hp, PEER_CAND_ROWS, tm), F32),
                        pltpu.VMEM((hp, tk, tm), F32), pltpu.VMEM((hp, tk, tm), F32)],
        compiler_params=_cparams(("parallel", "arbitrary")),
        name="peer_topk",
    )(x, mod, wq, k1, k2)


PEER_NE = PEER_HEADS * PEER_TOPK
PEER_COLS = PEER_NE * 16
PEER_TOK_UNROLL = 32


def _table_spec(tab):
    return pl.BlockSpec(tab.shape, lambda i: (0, 0), pipeline_mode=pl.Buffered(1))


def _gather_view(off_ref, tab_v, t):
    tiles = []
    tok_ref = off_ref.at[pl.ds(t * (PEER_NE // 2), PEER_NE // 2)]
    for j in range(PEER_NE // 2):
        w = tok_ref[j]
        o0 = pl.multiple_of(w & 0xFFFF, 8)
        o1 = pl.multiple_of(lax.shift_right_logical(w, 16), 8)
        tiles.append(tab_v[pl.ds(o0, 8), :])
        tiles.append(tab_v[pl.ds(o1, 8), :])
    return pltpu.bitcast(jnp.concatenate(tiles, axis=0), BF16)


def _select_mask(px_row):
    shp = (8, PEER_COLS)
    row = lax.broadcasted_iota(I32, shp, 0)
    col = lax.broadcasted_iota(I32, shp, 1)
    sub = (col >> 1) & 7
    fixed = jnp.logical_and((col & 1) == (row >> 2), (sub & 3) == (row & 3))
    return jnp.logical_and(fixed, (sub >> 2).astype(F32) == px_row)


def _split2(x):
    x1 = x.astype(BF16)
    return x1, (x - x1.astype(F32)).astype(BF16)


def _peer_u_kernel(off_ref, h_ref, par_ref, wgt_ref, e16_ref, g16_ref, tab_v, c_ref,
                   px_s, d_s, *, tb):
    px_s[...] = _dot(par_ref[...].astype(BF16), e16_ref[...])

    def tokens(i, carry):
        for u in range(PEER_TOK_UNROLL):
            t = i * PEER_TOK_UNROLL + u
            view = _gather_view(off_ref, tab_v, t)
            h1, h2 = _split2(h_ref[t])
            dd = _dot_nt(jnp.concatenate([h1, h2], axis=0), view)
            dd = jnp.where(_select_mask(px_s[pl.ds(t, 1), :]), dd[0:8] + dd[8:16], 0.0)
            d_s[pl.ds(t, 1), :] = jnp.sum(dd, axis=0, keepdims=True)
        return carry

    lax.fori_loop(0, tb // PEER_TOK_UNROLL, tokens, 0)
    d1, d2 = _split2(d_s[...])
    g16 = g16_ref[...]
    act = _dot(d1, g16) + _dot(d2, g16)
    gelu = 0.5 * act * (1.0 + lax.erf(act * (2.0 ** -0.5)))
    c_ref[...] = gelu * wgt_ref[...]


def _peer_u(off_flat, h3, par, wgt, e16, g16, tab, tb):
    n = h3.shape[0]
    full = lambda a: pl.BlockSpec(a.shape, lambda i: (0,) * a.ndim)
    kern = functools.partial(_peer_u_kernel, tb=tb)
    tok = pl.BlockSpec((tb, PEER_NE), lambda i: (i, 0))
    return pl.pallas_call(
        kern,
        out_shape=jax.ShapeDtypeStruct((n, PEER_NE), F32),
        grid=(n // tb,),
        in_specs=[pl.BlockSpec((tb * PEER_NE // 2,), lambda i: (i,), memory_space=pltpu.SMEM),
                  pl.BlockSpec((tb, 8, LANES), lambda i: (i, 0, 0)),
                  tok, tok, full(e16), full(g16), _table_spec(tab)],
        out_specs=tok,
        scratch_shapes=[pltpu.VMEM((tb, PEER_COLS), F32), pltpu.VMEM((tb, PEER_COLS), F32)],
        compiler_params=_cparams(("arbitrary",)),
        name="peer_u",
    )(off_flat, h3, par, wgt, e16, g16, tab)


def _peer_v_kernel(off_ref, c_ref, par_ref, e16_ref, tab_v, o_ref, px_s, c1_s, c2_s, *, tb):
    e16 = e16_ref[...]
    px_s[...] = _dot(par_ref[...].astype(BF16), e16)
    c1, c2 = _split2(c_ref[...])
    c1_s[...] = _dot(c1, e16)
    c2_s[...] = _dot(c2, e16)

    def tokens(i, carry):
        for u in range(PEER_TOK_UNROLL):
            t = i * PEER_TOK_UNROLL + u
            view = _gather_view(off_ref, tab_v, t)
            sel = _select_mask(px_s[pl.ds(t, 1), :])
            lhs = jnp.concatenate([jnp.where(sel, c1_s[pl.ds(t, 1), :], 0.0),
                                   jnp.where(sel, c2_s[pl.ds(t, 1), :], 0.0)], axis=0)
            out = _dot(lhs.astype(BF16), view)
            o_ref[t] = out[0:8] + out[8:16]
        return carry

    lax.fori_loop(0, tb // PEER_TOK_UNROLL, tokens, 0)


def _peer_v(off_flat, cw, par, e16, tab, tb):
    n = cw.shape[0]
    full = lambda a: pl.BlockSpec(a.shape, lambda i: (0,) * a.ndim)
    kern = functools.partial(_peer_v_kernel, tb=tb)
    tok = pl.BlockSpec((tb, PEER_NE), lambda i: (i, 0))
    return pl.pallas_call(
        kern,
        out_shape=jax.ShapeDtypeStruct((n, 8, LANES), F32),
        grid=(n // tb,),
        in_specs=[pl.BlockSpec((tb * PEER_NE // 2,), lambda i: (i,), memory_space=pltpu.SMEM),
                  tok, tok, full(e16), _table_spec(tab)],
        out_specs=pl.BlockSpec((tb, 8, LANES), lambda i: (i, 0, 0)),
        scratch_shapes=[pltpu.VMEM((tb, PEER_COLS), F32), pltpu.VMEM((tb, PEER_COLS), F32),
                        pltpu.VMEM((tb, PEER_COLS), F32)],
        compiler_params=_cparams(("arbitrary",)),
        name="peer_v",
    )(off_flat, cw, par, e16, tab)


def _pack_table(tab):
    e, d = tab.shape
    bits = lax.bitcast_convert_type(tab.astype(BF16), jnp.uint16).astype(U32)
    words = bits[:, :d // 2] | (bits[:, d // 2:] << 16)
    return words.reshape(e * 4, LANES)


def _ln_res_kernel(x_ref, f_ref, gt_ref, lng_ref, lnb_ref, o_ref, *, alpha):
    y = alpha * x_ref[...] + gt_ref[0] * f_ref[...]
    o_ref[...] = _layer_norm(y, lng_ref[...], lnb_ref[...])


def _ln_res(x, f, gt, lng, lnb, nb, nctx_b, tm, alpha):
    n, d = x.shape
    full = lambda a: pl.BlockSpec(a.shape, lambda i: (0,) * a.ndim)
    s1 = pl.BlockSpec((tm, d), lambda i: (i, 0))
    return pl.pallas_call(
        functools.partial(_ln_res_kernel, alpha=alpha),
        out_shape=jax.ShapeDtypeStruct((n, d), F32),
        grid=(n // tm,),
        in_specs=[s1, s1,
                  pl.BlockSpec((1, 1, d), lambda i: (_group_index(i, nb, nctx_b), 0, 0)),
                  full(lng), full(lnb)],
        out_specs=s1,
        compiler_params=_cparams(("parallel",)),
        name="ln_res",
    )(x, f, gt, lng, lnb)


def _pad_heads(w, n_heads, hd):
    lead = w.shape[:-1]
    w = w.reshape(lead + (n_heads, hd))
    w = jnp.pad(w, [(0, 0)] * len(lead) + [(0, 0), (0, LANES - hd)])
    return w.reshape(lead + (n_heads * LANES,))


def _pad_head_rows(w, n_heads, hd):
    d = w.shape[-1]
    w = w.reshape(n_heads, hd, d)
    w = jnp.pad(w, [(0, 0), (0, LANES - hd), (0, 0)])
    return w.reshape(n_heads * LANES, d)


def _rope_tables(rows, rot_dim, lane_off, ctx_len):
    r_idx = jnp.repeat(jnp.arange(rows), GRID_W).astype(F32)
    c_idx = jnp.tile(jnp.arange(GRID_W), rows).astype(F32)
    n = rot_dim // 4
    inv = ROPE_THETA ** (-jnp.arange(n, dtype=F32) / n)
    ang = jnp.concatenate([r_idx[:, None] * inv, c_idx[:, None] * inv], axis=-1)
    cos = jnp.repeat(jnp.cos(ang), 2, axis=-1)
    sin = jnp.repeat(jnp.sin(ang), 2, axis=-1) * jnp.tile(jnp.array([-1.0, 1.0], F32), rot_dim // 2)
    s = ang.shape[0]
    cos_t = jnp.ones((ctx_len + s, LANES), F32).at[ctx_len:, lane_off:lane_off + rot_dim].set(cos)
    sin_t = jnp.zeros((ctx_len + s, LANES), F32).at[ctx_len:, lane_off:lane_off + rot_dim].set(sin)
    return cos_t, sin_t


def _split_cols(w, widths):
    out, start = [], 0
    for wd in widths:
        out.append(w[..., start:start + wd])
        start += wd
    return out


def _forward(x, c, ctx, c_ctx, ada_w, ada_b, w_in, mla_q_norm, mla_kv_norm, mla_w_uq, mla_w_ukv,
             rwkv_mu, rwkv_w0, rwkv_w2, rwkv_a0, rwkv_a2, rwkv_g2, rwkv_k_k, rwkv_k_a, rwkv_r_k,
             rwkv_ln_g, rwkv_ln_b, gqa_q_norm, gqa_k_norm, win_sink, w_branch, w_out, ln1_g, ln1_b,
             peer_wq, peer_keys, peer_u, peer_v, ln2_g, ln2_b):
    stages = []
    bsz, seq, d = x.shape
    ctx_len = ctx.shape[1]
    depth = ada_w.shape[0]
    alpha = (2 * depth) ** 0.25
    l_tot = ctx_len + seq
    n = bsz * l_tot
    tm = 256
    tm_feat = 128
    assert ctx_len % tm == 0 and seq % tm == 0 and seq % GRID_W == 0
    nb = l_tot // tm
    nctx_b = ctx_len // tm
    rows = seq // GRID_W
    ne = PEER_HEADS * PEER_TOPK

    xs = jnp.concatenate([ctx, x], axis=1).reshape(n, d)
    cos_m, sin_m = _rope_tables(rows, MLA_ROPE, MLA_NOPE, ctx_len)
    cos_h, sin_h = _rope_tables(rows, GQA_HEAD, 0, ctx_len)

    m_rows = 16
    cvec = jnp.zeros((m_rows, d), F32).at[:bsz].set(c).at[bsz].set(c_ctx)
    mla_in = MLA_Q_RANK + MLA_KV_RANK + MLA_ROPE
    rwkv_in = 3 * RWKV_W + 2 * RWKV_W_LORA + 2 * RWKV_A_LORA + RWKV_G_LORA
    gqa_in = (GQA_HEADS + 2 * GQA_KV_HEADS) * GQA_HEAD
    win_in = (WIN_HEADS + 2 * WIN_KV_HEADS) * WIN_HEAD
    rw_widths = (RWKV_W, RWKV_W, RWKV_W, RWKV_W_LORA, RWKV_W_LORA, RWKV_A_LORA, RWKV_A_LORA,
                 RWKV_G_LORA)

    col16 = jnp.arange(PEER_COLS)
    e16 = (col16[None, :] // 16 == jnp.arange(ne)[:, None]).astype(BF16)
    g16 = e16.T

    for lyr in range(depth):
        last = lyr == depth - 1
        mod = _ada(cvec, ada_w[lyr], ada_b[lyr])
        chunks = [mod[:, i * d:(i + 1) * d] for i in range(6)]

        def table(ch):
            lat = ch[:bsz]
            cx = jnp.broadcast_to(ch[bsz][None], (bsz, d))
            return jnp.stack([cx, lat], axis=1).reshape(bsz * 2, d)

        sh1, sc1, gt1, sh2, sc2, gt2 = [table(ch) for ch in chunks]
        mod1 = jnp.stack([1.0 + sc1, sh1], axis=1)
        mod2 = jnp.stack([1.0 + sc2, sh2], axis=1)
        gt1 = gt1[:, None, :]
        gt2 = gt2[:, None, :]

        wi = w_in[lyr]
        w_mla, w_rw, w_gq, w_wn, w_gate = _split_cols(wi, (mla_in, rwkv_in, gqa_in, win_in, N_BRANCH * d))
        zc = lambda k: jnp.zeros((d, k), F32)
        w_mla_p = jnp.concatenate([w_mla[:, :MLA_Q_RANK + MLA_KV_RANK], zc(MLA_NOPE),
                                   w_mla[:, MLA_Q_RANK + MLA_KV_RANK:], zc(LANES - MLA_NOPE - MLA_ROPE)],
                                  axis=1)
        rr, rk_, rv, rwf, rwb, raf, rab, rgi = _split_cols(w_rw, rw_widths)
        hp = lambda w: _pad_heads(w, RWKV_HEADS, RWKV_HEAD)
        w_rw_p = jnp.concatenate([hp(rr), hp(rk_), hp(rv), rwf, rwb, raf, rab, rgi], axis=1)
        mu_parts = _split_cols(rwkv_mu[lyr], rw_widths)
        mu_p = jnp.concatenate([hp(mu_parts[0]), hp(mu_parts[1]), hp(mu_parts[2])] + mu_parts[3:], axis=1)

        def gqa_cols(w, nq, nkv, hd):
            q_, k_, v_ = _split_cols(w, (nq * hd, nkv * hd, nkv * hd))
            return jnp.concatenate([_pad_heads(q_, nq, hd), _pad_heads(k_, nkv, hd),
                                    _pad_heads(v_, nkv, hd)], axis=1)

        w_gq_p = gqa_cols(w_gq, GQA_HEADS, GQA_KV_HEADS, GQA_HEAD)
        w_wn_p = gqa_cols(w_wn, WIN_HEADS, WIN_KV_HEADS, WIN_HEAD)

        pm = _modmm(xs, mod1, w_mla_p.astype(BF16), nb, nctx_b, tm)
        pr = _modmm(xs, mod1, w_rw_p.astype(BF16), nb, nctx_b, tm)
        pg = _modmm(xs, mod1, w_gq_p.astype(BF16), nb, nctx_b, tm, BF16)
        pw = _modmm(xs, mod1, w_wn_p.astype(BF16), nb, nctx_b, tm, BF16)
        gate = _modmm(xs, mod1, w_gate.astype(BF16), nb, nctx_b, tm, BF16)

        uq = mla_w_uq[lyr].reshape(MLA_Q_RANK, MLA_HEADS, MLA_NOPE + MLA_ROPE)
        uq = jnp.pad(uq, [(0, 0), (0, 0), (0, LANES - MLA_NOPE - MLA_ROPE)]).reshape(MLA_Q_RANK, -1)
        ukv = mla_w_ukv[lyr].reshape(MLA_KV_RANK, MLA_HEADS, MLA_NOPE + MLA_V)
        uk = jnp.pad(ukv[:, :, :MLA_NOPE], [(0, 0), (0, 0), (0, LANES - MLA_NOPE)]).reshape(MLA_KV_RANK, -1)
        uv = jnp.pad(ukv[:, :, MLA_NOPE:], [(0, 0), (0, 0), (0, LANES - MLA_V)]).reshape(MLA_KV_RANK, -1)
        qa, ka, va = _mla_prep(pm, mla_q_norm[lyr][None], mla_kv_norm[lyr][None], uq.astype(BF16),
                               uk.astype(BF16), uv.astype(BF16), cos_m, sin_m, nb, tm)
        r3 = lambda a: a.reshape(bsz, l_tot, a.shape[-1])
        ya = _attention(qa, r3(ka), va, ctx_len, n_q=MLA_HEADS, n_kv=MLA_HEADS, tq=256)

        zl = jnp.zeros((RWKV_W_LORA, RWKV_HEADS * LANES), F32)
        w2c = jnp.concatenate([jnp.concatenate([hp(rwkv_w2[lyr, 0]), zl], axis=1),
                               jnp.concatenate([zl, hp(rwkv_w2[lyr, 1])], axis=1)], axis=0)
        a2c = jnp.concatenate([jnp.concatenate([hp(rwkv_a2[lyr, 0]), zl], axis=1),
                               jnp.concatenate([zl, hp(rwkv_a2[lyr, 1])], axis=1)], axis=0)
        w0c = jnp.concatenate([hp(rwkv_w0[lyr, 0]), hp(rwkv_w0[lyr, 1])])[None]
        a0c = jnp.concatenate([hp(rwkv_a0[lyr, 0]), hp(rwkv_a0[lyr, 1])])[None]
        feats = _rwkv_features(pr, mu_p, w0c, w2c.astype(BF16), a0c, a2c.astype(BF16),
                               hp(rwkv_g2[lyr]).astype(BF16), hp(rwkv_k_k[lyr])[None],
                               hp(rwkv_k_a[lyr])[None], hp(rwkv_r_k[lyr].reshape(-1))[None],
                               l_tot // tm_feat, ctx_len // tm_feat, tm_feat)
        f_r, f_v, f_kk, f_g, f_bonus, f_lw, f_k, f_b = feats
        o_scan = _rwkv_scan(f_r, f_v, f_kk, f_lw, f_k, f_b, bsz, ctx_len // RWKV_CHUNK)
        yb = _rwkv_out(o_scan, f_bonus, f_g, hp(rwkv_ln_g[lyr])[None], hp(rwkv_ln_b[lyr])[None], tm)

        pad_g = lambda g: jnp.pad(g, (0, LANES - g.shape[0]))[None]
        qc, kc, vc = _gqa_prep(pg, pad_g(gqa_q_norm[lyr]), pad_g(gqa_k_norm[lyr]), cos_h, sin_h,
                               nb, tm, n_q=GQA_HEADS, n_kv=GQA_KV_HEADS, hd=GQA_HEAD,
                               scale=GQA_SCALE, qk_norm=True, transposed=True)
        yc = _attention(qc, r3(kc), vc, ctx_len, n_q=GQA_HEADS, n_kv=GQA_KV_HEADS, tq=256)

        ones_g = jnp.ones((1, LANES), F32)
        qd, kd, vd = _gqa_prep(pw, ones_g, ones_g, cos_h, sin_h, nb, tm, n_q=WIN_HEADS,
                               n_kv=WIN_KV_HEADS, hd=WIN_HEAD, scale=WIN_SCALE, qk_norm=False,
                               transposed=False)
        yd = _window_attention(r3(qd), r3(kd), r3(vd), win_sink[lyr], ctx_len,
                               n_q=WIN_HEADS, n_kv=WIN_KV_HEADS)

        wb = jnp.stack([_pad_head_rows(w_branch[lyr, i], 8, 64) for i in range(N_BRANCH)]).astype(BF16)
        x_mid = _merge(ya.reshape(n, -1), yb, yc.reshape(n, -1), yd.reshape(n, -1), gate, xs, gt1,
                       wb, w_out[lyr].astype(BF16), ln1_g[lyr][None], ln1_b[lyr][None],
                       nb, nctx_b, tm, alpha, last)
        if last:
            n_p, nb_p, nctx_p = bsz * seq, seq // tm, 0
        else:
            n_p, nb_p, nctx_p = n, nb, nctx_b

        keys = peer_keys[lyr]
        half = PEER_DQ // 2
        k1 = jnp.pad(keys[:, 0], [(0, 0), (0, 0), (0, LANES - half)]).astype(BF16)
        k2 = jnp.pad(keys[:, 1], [(0, 0), (0, 0), (LANES - half, 0)]).astype(BF16)
        h_in, idx_t, wgt_t, off_t = _peer_topk(x_mid, mod2, peer_wq[lyr].astype(BF16), k1, k2,
                                               nb_p, nctx_p, tm)
        idx = idx_t.reshape(ne, n_p).T
        wgt = wgt_t.reshape(ne, n_p).T
        off_flat = off_t.reshape(ne // 2, n_p).T.reshape(-1)
        par = (idx & 1).astype(F32)
        tb = 128
        cw = _peer_u(off_flat, h_in.reshape(n_p, 8, LANES), par, wgt, e16, g16,
                     _pack_table(peer_u[lyr]), tb)
        ffn = _peer_v(off_flat, cw, par, e16, _pack_table(peer_v[lyr]), tb).reshape(n_p, d)
        xs = _ln_res(x_mid, ffn, gt2, ln2_g[lyr][None], ln2_b[lyr][None], nb_p, nctx_p, tm, alpha)
        stages.append(dict(ya=ya, yb=yb, yc=yc, yd=yd, x_mid=x_mid, idx=idx, wgt=wgt, ffn=ffn,
                           x_out=xs))

    return xs.reshape(bsz, seq, d), stages


def kernel(x, c, ctx, c_ctx, ada_w, ada_b, w_in, mla_q_norm, mla_kv_norm, mla_w_uq, mla_w_ukv,
           rwkv_mu, rwkv_w0, rwkv_w2, rwkv_a0, rwkv_a2, rwkv_g2, rwkv_k_k, rwkv_k_a, rwkv_r_k,
           rwkv_ln_g, rwkv_ln_b, gqa_q_norm, gqa_k_norm, win_sink, w_branch, w_out, ln1_g, ln1_b,
           peer_wq, peer_keys, peer_u, peer_v, ln2_g, ln2_b):
    out, _ = _forward(x, c, ctx, c_ctx, ada_w, ada_b, w_in, mla_q_norm, mla_kv_norm, mla_w_uq,
                      mla_w_ukv, rwkv_mu, rwkv_w0, rwkv_w2, rwkv_a0, rwkv_a2, rwkv_g2, rwkv_k_k,
                      rwkv_k_a, rwkv_r_k, rwkv_ln_g, rwkv_ln_b, gqa_q_norm, gqa_k_norm, win_sink,
                      w_branch, w_out, ln1_g, ln1_b, peer_wq, peer_keys, peer_u, peer_v, ln2_g,
                      ln2_b)
    return out
```

```python
import functools
import math

import jax
import jax.numpy as jnp
from jax import lax
from jax.experimental import pallas as pl
from jax.experimental.pallas import tpu as pltpu

F32 = jnp.float32
BF16 = jnp.bfloat16
I32 = jnp.int32
U32 = jnp.uint32

LANES = 128
GRID_W = 64
ROPE_THETA = 10000.0
NEG_INF = -1e30
LN_EPS = 1e-5
RMS_EPS = 1e-6

MLA_HEADS, MLA_Q_RANK, MLA_KV_RANK, MLA_NOPE, MLA_ROPE, MLA_V = 8, 256, 128, 64, 32, 64
MLA_SCALE = (MLA_NOPE + MLA_ROPE) ** -0.5
RWKV_HEADS, RWKV_HEAD = 8, 64
RWKV_W = RWKV_HEADS * RWKV_HEAD
RWKV_W_LORA, RWKV_A_LORA, RWKV_G_LORA = 64, 64, 128
RWKV_GN_EPS = 64e-5
RWKV_CHUNK = 128
GQA_HEADS, GQA_KV_HEADS, GQA_HEAD = 8, 2, 64
GQA_SCALE = GQA_HEAD ** -0.5
WIN_HEADS, WIN_KV_HEADS, WIN_HEAD, WINDOW = 8, 2, 64, 128
WIN_SCALE = WIN_HEAD ** -0.5
PEER_HEADS, PEER_N_KEYS, PEER_TOPK, PEER_DQ = 8, 128, 16, 128
N_BRANCH = 4

VMEM_LIMIT = 56 * 1024 * 1024


def _cparams(sem, vmem=None):
    return pltpu.CompilerParams(dimension_semantics=sem, vmem_limit_bytes=vmem or VMEM_LIMIT)


def _pick_tile(n, cap, mult=LANES):
    best = mult
    for t in range(mult, min(n, cap) + 1, mult):
        if n % t == 0:
            best = t
    return best


def _dot(a, b):
    return jnp.dot(a, b, preferred_element_type=F32)


def _dot_nt(a, b):
    return lax.dot_general(a, b, (((1,), (1,)), ((), ())), preferred_element_type=F32)


def _dot_tn(a, b):
    return lax.dot_general(a, b, (((0,), (0,)), ((), ())), preferred_element_type=F32)


ONES_LANE = 64


def _with_ones_lane(v):
    lane = lax.broadcasted_iota(I32, v.shape, v.ndim - 1)
    return jnp.where((lane & (LANES - 1)) == ONES_LANE, 1.0, v)


def _heads_t(x):
    return jnp.concatenate([x[:, h * LANES:(h + 1) * LANES].T for h in range(x.shape[1] // LANES)],
                           axis=0)


def _layer_norm(y, g, b):
    mu = jnp.mean(y, axis=-1, keepdims=True)
    d = y - mu
    var = jnp.mean(d * d, axis=-1, keepdims=True)
    return d * lax.rsqrt(var + LN_EPS) * g + b


def _swap_pairs(x):
    n = x.shape[-1]
    lane = lax.broadcasted_iota(I32, x.shape, x.ndim - 1)
    nxt = pltpu.roll(x, n - 1, x.ndim - 1)
    prv = pltpu.roll(x, 1, x.ndim - 1)
    return jnp.where((lane & 1) == 0, nxt, prv)


def _rope(x, cos, sin_signed):
    return x * cos + _swap_pairs(x) * sin_signed


def _ada_kernel(c_ref, w_ref, b_ref, o_ref):
    c = c_ref[...]
    s = c * jax.nn.sigmoid(c)
    o_ref[...] = _dot(s.astype(BF16), w_ref[...].astype(BF16)) + b_ref[...]


def _ada(cvec, w, b):
    m, k = cvec.shape
    nc = w.shape[1]
    tn = _pick_tile(nc, 1536)
    return pl.pallas_call(
        _ada_kernel,
        out_shape=jax.ShapeDtypeStruct((m, nc), F32),
        grid=(nc // tn,),
        in_specs=[pl.BlockSpec((m, k), lambda j: (0, 0)),
                  pl.BlockSpec((k, tn), lambda j: (0, j)),
                  pl.BlockSpec((1, tn), lambda j: (0, j))],
        out_specs=pl.BlockSpec((m, tn), lambda j: (0, j)),
        compiler_params=_cparams(("parallel",)),
        name="ada_mod",
    )(cvec, w, b.reshape(1, nc))


def _group_index(i, nb, nctx_b):
    return (i // nb) * 2 + (i % nb >= nctx_b).astype(I32)


def _mm_kernel(x_ref, mod_ref, w_ref, o_ref, *, sub, tm, nb, nctx_b):
    i = pl.program_id(1)
    parts = []
    for u in range(sub):
        m = mod_ref[_group_index(i * sub + u, nb, nctx_b)]
        xm = x_ref[u * tm:(u + 1) * tm, :] * m[0:1, :] + m[1:2, :]
        parts.append(xm.astype(BF16))
    o_ref[...] = _dot(jnp.concatenate(parts, axis=0), w_ref[...]).astype(o_ref.dtype)


def _modmm(x, mod, w, nb, nctx_b, tm, out_dtype=F32):
    n, k = x.shape
    nc = w.shape[1]
    tn = _pick_tile(nc, 2048)
    sub = max(s for s in (4, 2, 1) if (n // tm) % s == 0)
    kern = functools.partial(_mm_kernel, sub=sub, tm=tm, nb=nb, nctx_b=nctx_b)
    return pl.pallas_call(
        kern,
        out_shape=jax.ShapeDtypeStruct((n, nc), out_dtype),
        grid=(nc // tn, n // (tm * sub)),
        in_specs=[pl.BlockSpec((tm * sub, k), lambda j, i: (i, 0)),
                  pl.BlockSpec(mod.shape, lambda j, i: (0, 0, 0)),
                  pl.BlockSpec((k, tn), lambda j, i: (0, j))],
        out_specs=pl.BlockSpec((tm * sub, tn), lambda j, i: (i, j)),
        compiler_params=_cparams(("parallel", "parallel")),
        name="in_proj",
    )(x, mod, w)


def _mla_prep_kernel(p_ref, qn_ref, kvn_ref, wq_ref, wk_ref, wv_ref, cos_ref, sin_ref,
                     qt_ref, k_ref, vt_ref):
    p = p_ref[...]
    dq = p[:, 0:MLA_Q_RANK]
    dkv = p[:, MLA_Q_RANK:MLA_Q_RANK + MLA_KV_RANK]
    krp = p[:, MLA_Q_RANK + MLA_KV_RANK:]
    qn = dq * lax.rsqrt(jnp.mean(dq * dq, axis=-1, keepdims=True) + RMS_EPS) * qn_ref[...]
    kvn = dkv * lax.rsqrt(jnp.mean(dkv * dkv, axis=-1, keepdims=True) + RMS_EPS) * kvn_ref[...]
    kvn = kvn.astype(BF16)
    cos = cos_ref[...]
    sin = sin_ref[...]
    cos_h = jnp.concatenate([cos] * MLA_HEADS, axis=1)
    sin_h = jnp.concatenate([sin] * MLA_HEADS, axis=1)
    q = _rope(_dot(qn.astype(BF16), wq_ref[...]), cos_h, sin_h) * MLA_SCALE
    kr = _rope(krp, cos, sin)
    k = _dot(kvn, wk_ref[...]) + jnp.concatenate([kr] * MLA_HEADS, axis=1)
    qt_ref[0] = _heads_t(q).astype(BF16)
    k_ref[...] = k.astype(BF16)
    vt_ref[0, 0] = _heads_t(_with_ones_lane(_dot(kvn, wv_ref[...]))).astype(BF16)


def _mla_prep(pm, qn, kvn, wq, wk, wv, cos, sin, nb, tm):
    n = pm.shape[0]
    hw = MLA_HEADS * LANES
    bsz = n // (nb * tm)
    full = lambda a: pl.BlockSpec(a.shape, lambda i: (0,) * a.ndim)
    return pl.pallas_call(
        _mla_prep_kernel,
        out_shape=(jax.ShapeDtypeStruct((bsz, hw, nb * tm), BF16),
                   jax.ShapeDtypeStruct((n, hw), BF16),
                   jax.ShapeDtypeStruct((bsz, nb, hw, tm), BF16)),
        grid=(n // tm,),
        in_specs=[pl.BlockSpec((tm, pm.shape[1]), lambda i: (i, 0)),
                  full(qn), full(kvn), full(wq), full(wk), full(wv),
                  pl.BlockSpec((tm, LANES), lambda i: (i % nb, 0)),
                  pl.BlockSpec((tm, LANES), lambda i: (i % nb, 0))],
        out_specs=(pl.BlockSpec((1, hw, tm), lambda i: (i // nb, 0, i % nb)),
                   pl.BlockSpec((tm, hw), lambda i: (i, 0)),
                   pl.BlockSpec((1, 1, hw, tm), lambda i: (i // nb, i % nb, 0, 0))),
        compiler_params=_cparams(("parallel",)),
        name="mla_prep",
    )(pm, qn, kvn, wq, wk, wv, cos, sin)


def _gqa_prep_kernel(p_ref, qn_ref, kn_ref, cos_ref, sin_ref, q_ref, k_ref, v_ref,
                     *, n_q, n_kv, hd, scale, qk_norm, transposed):
    p = p_ref[...].astype(F32)
    cos = cos_ref[...]
    sin = sin_ref[...]

    def head(j, gain):
        x = p[:, j * LANES:(j + 1) * LANES]
        if qk_norm:
            ms = jnp.sum(x * x, axis=-1, keepdims=True) * (1.0 / hd)
            x = x * lax.rsqrt(ms + RMS_EPS) * gain
        return _rope(x, cos, sin)

    qg = qn_ref[...]
    kg = kn_ref[...]
    q = jnp.concatenate([head(j, qg) for j in range(n_q)], axis=1) * scale
    k = jnp.concatenate([head(n_q + j, kg) for j in range(n_kv)], axis=1)
    v = _with_ones_lane(p[:, (n_q + n_kv) * LANES:])
    k_ref[...] = k.astype(BF16)
    if transposed:
        q_ref[0] = _heads_t(q).astype(BF16)
        v_ref[0, 0] = _heads_t(v).astype(BF16)
    else:
        q_ref[...] = q.astype(BF16)
        v_ref[...] = v.astype(BF16)


def _gqa_prep(pg, qn, kn, cos, sin, nb, tm, *, n_q, n_kv, hd, scale, qk_norm, transposed):
    n = pg.shape[0]
    bsz = n // (nb * tm)
    full = lambda a: pl.BlockSpec(a.shape, lambda i: (0,) * a.ndim)
    kern = functools.partial(_gqa_prep_kernel, n_q=n_q, n_kv=n_kv, hd=hd, scale=scale,
                             qk_norm=qk_norm, transposed=transposed)
    if transposed:
        q_shape = jax.ShapeDtypeStruct((bsz, n_q * LANES, nb * tm), BF16)
        v_shape = jax.ShapeDtypeStruct((bsz, nb, n_kv * LANES, tm), BF16)
        q_spec = pl.BlockSpec((1, n_q * LANES, tm), lambda i: (i // nb, 0, i % nb))
        v_spec = pl.BlockSpec((1, 1, n_kv * LANES, tm), lambda i: (i // nb, i % nb, 0, 0))
    else:
        q_shape = jax.ShapeDtypeStruct((n, n_q * LANES), BF16)
        v_shape = jax.ShapeDtypeStruct((n, n_kv * LANES), BF16)
        q_spec = pl.BlockSpec((tm, n_q * LANES), lambda i: (i, 0))
        v_spec = pl.BlockSpec((tm, n_kv * LANES), lambda i: (i, 0))
    return pl.pallas_call(
        kern,
        out_shape=(q_shape, jax.ShapeDtypeStruct((n, n_kv * LANES), BF16), v_shape),
        grid=(n // tm,),
        in_specs=[pl.BlockSpec((tm, pg.shape[1]), lambda i: (i, 0)),
                  full(qn), full(kn),
                  pl.BlockSpec((tm, LANES), lambda i: (i % nb, 0)),
                  pl.BlockSpec((tm, LANES), lambda i: (i % nb, 0))],
        out_specs=(q_spec, pl.BlockSpec((tm, n_kv * LANES), lambda i: (i, 0)), v_spec),
        compiler_params=_cparams(("parallel",)),
        name="gqa_prep",
    )(pg, qn, kn, cos, sin)


ATTN_GROUPS_IN_FLIGHT = 8


def _attn_kernel(qt_ref, k_ref, vt_ref, o_ref, m_s, acc_s, *, n_q, n_kv, tq, tk, nctx_q,
                 nk_ctx, nk_all):
    qi = pl.program_id(1)
    nkb = jnp.where(qi < nctx_q, nk_ctx, nk_all)
    grp = n_q // n_kv
    gs = range(n_kv)
    m_s[...] = jnp.full(m_s.shape, NEG_INF, F32)
    acc_s[...] = jnp.zeros(acc_s.shape, F32)

    def body(kb, carry):
        ks = pl.multiple_of(kb * tk, tk)
        for g0 in range(0, n_kv, ATTN_GROUPS_IN_FLIGHT):
            gb = range(g0, min(g0 + ATTN_GROUPS_IN_FLIGHT, n_kv))
            s, m_old, m_new, p, a, pv = {}, {}, {}, {}, {}, {}
            for g in gb:
                qtg = jnp.concatenate(
                    [qt_ref[0, (g * grp + j) * LANES:(g * grp + j + 1) * LANES, :]
                     for j in range(grp)], axis=1)
                s[g] = _dot(k_ref[0, pl.ds(ks, tk), g * LANES:(g + 1) * LANES], qtg)
            for g in gb:
                m_old[g] = m_s[g]
                m_new[g] = jnp.maximum(m_old[g], jnp.max(s[g], axis=0, keepdims=True))
            for g in gb:
                p[g] = jnp.exp(s[g] - m_new[g]).astype(BF16)
                a[g] = jnp.exp(m_old[g] - m_new[g])
            for g in gb:
                pv[g] = _dot(vt_ref[0, kb, g * LANES:(g + 1) * LANES, :], p[g])
            for g in gb:
                m_s[g] = m_new[g]
                acc_s[g] = a[g] * acc_s[g] + pv[g]
        return carry

    lax.fori_loop(0, nkb, body, 0)
    row = lax.broadcasted_iota(I32, (LANES, 1), 0)
    for g in gs:
        acc = acc_s[g]
        ot = jnp.where(row < ONES_LANE, acc / acc[ONES_LANE:ONES_LANE + 1, :], 0.0)
        for a in range(grp):
            h = g * grp + a
            o_ref[0, :, h * LANES:(h + 1) * LANES] = ot[:, a * tq:(a + 1) * tq].T.astype(o_ref.dtype)


def _attention(qt, k, vt, ctx_len, *, n_q, n_kv, tq):
    b, l, _ = k.shape
    tk = vt.shape[-1]
    rows = (n_q // n_kv) * tq
    kern = functools.partial(_attn_kernel, n_q=n_q, n_kv=n_kv, tq=tq, tk=tk,
                             nctx_q=ctx_len // tq, nk_ctx=ctx_len // tk, nk_all=l // tk)
    return pl.pallas_call(
        kern,
        out_shape=jax.ShapeDtypeStruct((b, l, n_q * LANES), BF16),
        grid=(b, l // tq),
        in_specs=[pl.BlockSpec((1, n_q * LANES, tq), lambda bi, i: (bi, 0, i)),
                  pl.BlockSpec((1, l, n_kv * LANES), lambda bi, i: (bi, 0, 0)),
                  pl.BlockSpec((1, l // tk, n_kv * LANES, tk), lambda bi, i: (bi, 0, 0, 0))],
        out_specs=pl.BlockSpec((1, tq, n_q * LANES), lambda bi, i: (bi, i, 0)),
        scratch_shapes=[pltpu.VMEM((n_kv, 1, rows), F32), pltpu.VMEM((n_kv, LANES, rows), F32)],
        compiler_params=_cparams(("parallel", "arbitrary")),
        name="dense_attn",
    )(qt, k, vt)


def _win_kernel(sink_ref, q_ref, kc_ref, kp_ref, kcur_ref, kn_ref, vc_ref, vp_ref, vcur_ref,
                vn_ref, o_ref, *, n_q, n_kv, nctx_b, nb, ctx_len):
    qi = pl.program_id(1)
    w = WINDOW
    is_lat = (qi >= nctx_b).astype(I32)
    prev_ok = is_lat * (qi - 1 >= nctx_b).astype(I32)
    next_ok = is_lat * (qi + 1 < nb).astype(I32)
    nk = ctx_len + 3 * w
    r = lax.broadcasted_iota(I32, (w, nk), 0)
    c2 = lax.broadcasted_iota(I32, (w, nk), 1)
    c = c2 - ctx_len
    near = jnp.abs(r - (c - w)) <= w
    blk_ok = jnp.where(c < w, prev_ok, jnp.where(c < 2 * w, is_lat, next_ok)) > 0
    valid = jnp.logical_or(c2 < ctx_len, jnp.logical_and(near, blk_ok))
    bias = jnp.where(valid, 0.0, NEG_INF).astype(F32)
    grp = n_q // n_kv
    bias = jnp.concatenate([bias] * grp, axis=0)
    for g in range(n_kv):
        ls = slice(g * LANES, (g + 1) * LANES)
        qg = jnp.concatenate(
            [q_ref[0, :, (g * grp + a) * LANES:(g * grp + a + 1) * LANES] for a in range(grp)],
            axis=0)
        kcat = jnp.concatenate([kc_ref[0, :, ls], kp_ref[0, :, ls], kcur_ref[0, :, ls],
                                kn_ref[0, :, ls]], axis=0)
        vcat = jnp.concatenate([vc_ref[0, :, ls], vp_ref[0, :, ls], vcur_ref[0, :, ls],
                                vn_ref[0, :, ls]], axis=0)
        s = _dot_nt(qg, kcat) + bias
        sk = jnp.concatenate([jnp.full((w, 1), sink_ref[g * grp + a], F32) for a in range(grp)],
                             axis=0)
        m = jnp.maximum(jnp.max(s, axis=1, keepdims=True), sk)
        p = jnp.exp(s - m)
        den = jnp.sum(p, axis=1, keepdims=True) + jnp.exp(sk - m)
        o = _dot(p.astype(BF16), vcat) / den
        for a in range(grp):
            h = g * grp + a
            o_ref[0, :, h * LANES:(h + 1) * LANES] = o[a * w:(a + 1) * w].astype(o_ref.dtype)


def _window_attention(q, k, v, sink, ctx_len, *, n_q, n_kv):
    b, l, _ = q.shape
    w = WINDOW
    nb = l // w
    nctx_b = ctx_len // w
    kw = n_kv * LANES
    kern = functools.partial(_win_kernel, n_q=n_q, n_kv=n_kv, nctx_b=nctx_b, nb=nb,
                             ctx_len=ctx_len)
    ctx_spec = pl.BlockSpec((1, ctx_len, kw), lambda bi, i: (bi, 0, 0))
    prev_spec = pl.BlockSpec((1, w, kw), lambda bi, i: (bi, jnp.maximum(i - 1, 0), 0))
    cur_spec = pl.BlockSpec((1, w, kw), lambda bi, i: (bi, i, 0))
    next_spec = pl.BlockSpec((1, w, kw), lambda bi, i: (bi, jnp.minimum(i + 1, nb - 1), 0))
    return pl.pallas_call(
        kern,
        out_shape=jax.ShapeDtypeStruct((b, l, n_q * LANES), BF16),
        grid=(b, nb),
        in_specs=[pl.BlockSpec(memory_space=pltpu.SMEM),
                  pl.BlockSpec((1, w, n_q * LANES), lambda bi, i: (bi, i, 0)),
                  ctx_spec, prev_spec, cur_spec, next_spec,
                  ctx_spec, prev_spec, cur_spec, next_spec],
        out_specs=pl.BlockSpec((1, w, n_q * LANES), lambda bi, i: (bi, i, 0)),
        compiler_params=_cparams(("parallel", "arbitrary")),
        name="window_attn",
    )(sink, q, k, k, k, k, v, v, v, v)


def _rwkv_feat_kernel(p_ref, hp_ref, hn_ref, mu_ref, w0_ref, w2_ref, a0_ref, a2_ref, g2_ref,
                      kk_ref, ka_ref, rk_ref,
                      r_out, v_out, kk_out, g_out, bonus_out, lw_out, k_out, b_out,
                      *, tm, nb, nctx_b):
    i = pl.program_id(0) % nb
    seq_start = jnp.logical_or(i == 0, i == nctx_b)
    seq_end = jnp.logical_or(i == nctx_b - 1, i == nb - 1)
    p = p_ref[...]
    row = lax.broadcasted_iota(I32, p.shape, 0)
    first = jnp.where(seq_start, 0.0, hp_ref[7:8, :])
    last = jnp.where(seq_end, 0.0, hn_ref[0:1, :])
    prev = jnp.where(row == 0, first, pltpu.roll(p, 1, 0))
    nxt = jnp.where(row == tm - 1, last, pltpu.roll(p, tm - 1, 0))
    mu = mu_ref[...]
    ps = p + mu[0:1, :] * (prev - p) + mu[1:2, :] * (nxt - p)
    hw = RWKV_HEADS * LANES
    r = ps[:, 0:hw]
    k = ps[:, hw:2 * hw]
    v = ps[:, 2 * hw:3 * hw]
    wfb = ps[:, 3 * hw:3 * hw + LANES]
    afb = ps[:, 3 * hw + LANES:3 * hw + 2 * LANES]
    gi = ps[:, 3 * hw + 2 * LANES:]
    kkr = k * kk_ref[...]
    parts = []
    for h in range(RWKV_HEADS):
        x = kkr[:, h * LANES:(h + 1) * LANES]
        nrm = jnp.sqrt(jnp.sum(x * x, axis=-1, keepdims=True))
        parts.append(x / jnp.maximum(nrm, 1e-12))
    kk = jnp.concatenate(parts, axis=1)
    z = w0_ref[...] + _dot(jnp.tanh(wfb).astype(BF16), w2_ref[...])
    lw = -math.exp(-0.5) * jax.nn.sigmoid(z)
    a = jax.nn.sigmoid(a0_ref[...] + _dot(afb.astype(BF16), a2_ref[...]))
    ka = ka_ref[...]
    k0 = k * (1.0 + (a[:, 0:hw] - 1.0) * ka)
    k1 = k * (1.0 + (a[:, hw:] - 1.0) * ka)
    rkk = r * (k0 + k1) * rk_ref[...]
    bparts = []
    for h in range(RWKV_HEADS):
        sl = slice(h * LANES, (h + 1) * LANES)
        bparts.append(jnp.sum(rkk[:, sl], axis=-1, keepdims=True) * v[:, sl])
    r_out[...] = r.astype(r_out.dtype)
    v_out[...] = v.astype(v_out.dtype)
    kk_out[...] = kk.astype(kk_out.dtype)
    g_out[...] = _dot(jax.nn.sigmoid(gi).astype(BF16), g2_ref[...]).astype(g_out.dtype)
    bonus_out[...] = jnp.concatenate(bparts, axis=1).astype(bonus_out.dtype)
    lw_out[0] = lw[:, 0:hw]
    lw_out[1] = lw[:, hw:]
    k_out[0] = k0.astype(k_out.dtype)
    k_out[1] = k1.astype(k_out.dtype)
    b_out[0] = (a[:, 0:hw] * kk).astype(b_out.dtype)
    b_out[1] = (a[:, hw:] * kk).astype(b_out.dtype)


def _rwkv_features(pr, mu, w0, w2, a0, a2, g2, kk, ka, rk, nb, nctx_b, tm):
    n, wid = pr.shape
    hw = RWKV_HEADS * LANES
    full = lambda a: pl.BlockSpec(a.shape, lambda i: (0,) * a.ndim)
    kern = functools.partial(_rwkv_feat_kernel, tm=tm, nb=nb, nctx_b=nctx_b)
    one = jax.ShapeDtypeStruct((n, hw), BF16)
    two = jax.ShapeDtypeStruct((2, n, hw), BF16)
    two_f32 = jax.ShapeDtypeStruct((2, n, hw), F32)
    s1 = pl.BlockSpec((tm, hw), lambda i: (i, 0))
    s2 = pl.BlockSpec((2, tm, hw), lambda i: (0, i, 0))
    t8 = tm // 8
    return pl.pallas_call(
        kern,
        out_shape=(one, one, one, one, one, two_f32, two, two),
        grid=(n // tm,),
        in_specs=[pl.BlockSpec((tm, wid), lambda i: (i, 0)),
                  pl.BlockSpec((8, wid), lambda i: (jnp.maximum(i * t8 - 1, 0), 0)),
                  pl.BlockSpec((8, wid), lambda i: (jnp.minimum((i + 1) * t8, n // 8 - 1), 0)),
                  full(mu), full(w0), full(w2), full(a0), full(a2), full(g2),
                  full(kk), full(ka), full(rk)],
        out_specs=(s1, s1, s1, s1, s1, s2, s2, s2),
        compiler_params=_cparams(("parallel",)),
        name="rwkv_features",
    )(pr, pr, pr, mu, w0, w2, a0, a2, g2, kk, ka, rk)


def _split3_dot(mask_bf16, x):
    x1 = x.astype(BF16)
    r1 = x - x1.astype(F32)
    x2 = r1.astype(BF16)
    x3 = (r1 - x2.astype(F32)).astype(BF16)
    return _dot(mask_bf16, x1) + _dot(mask_bf16, x2) + _dot(mask_bf16, x3)


def _rwkv_scan_kernel(r_ref, v_ref, kk_ref, lw_ref, k_ref, b_ref, o_ref, s_ref):
    d = pl.program_id(1)
    c = pl.program_id(2)
    cs = RWKV_CHUNK

    @pl.when(c == 0)
    def _():
        s_ref[...] = jnp.zeros_like(s_ref)

    rev = d == 1
    t_i = lax.broadcasted_iota(I32, (cs, cs), 0)
    s_i = lax.broadcasted_iota(I32, (cs, cs), 1)
    order = jnp.where(rev, t_i - s_i, s_i - t_i)
    incl = order <= 0
    strict = order < 0
    eye = jnp.where(t_i == s_i, 1.0, 0.0).astype(F32)
    off_masks = []
    for lvl in range(int(math.log2(cs))):
        pair = (t_i >> (lvl + 1)) == (s_i >> (lvl + 1))
        half = (t_i >> lvl) != (s_i >> lvl)
        off = jnp.logical_and(jnp.logical_and(pair, half), strict)
        off_masks.append(jnp.where(off, 1.0, 0.0).astype(F32))
    lw = lw_ref[0]
    cum = _split3_dot(jnp.where(incl, 1.0, 0.0).astype(BF16), lw)
    p_in = jnp.exp(cum)
    p_inv = jnp.exp(-cum)
    p_ex = jnp.exp(cum - lw)
    tot = jnp.where(rev, cum[0:1, :], cum[cs - 1:cs, :])
    p_all = jnp.exp(tot)
    a_t = -kk_ref[...].astype(F32) * p_ex
    r_t = r_ref[...].astype(F32) * p_in
    k_t = k_ref[0].astype(F32) * p_inv
    b_t = b_ref[0].astype(F32) * p_inv
    v = v_ref[...].astype(F32)
    hs = range(RWKV_HEADS)
    sls = [slice(h * LANES, (h + 1) * LANES) for h in hs]
    vb = [v[:, sl].astype(BF16) for sl in sls]
    bk = [jnp.concatenate([b_t[:, sl], k_t[:, sl]], axis=0).astype(BF16) for sl in sls]
    gm = [_dot_nt(jnp.concatenate([a_t[:, sls[h]], r_t[:, sls[h]]], axis=0).astype(BF16), bk[h])
          for h in hs]
    m_ab = [jnp.where(strict, g[0:cs, 0:cs], 0.0) for g in gm]
    mkv = [_dot(jnp.where(strict, gm[h][0:cs, cs:], 0.0).astype(BF16), vb[h]) for h in hs]
    x = [eye + m * off_masks[0] for m in m_ab]
    for lvl in range(1, len(off_masks)):
        xb = [xx.astype(BF16) for xx in x]
        t1 = [_dot(xb[h], (m_ab[h] * off_masks[lvl]).astype(BF16)).astype(BF16) for h in hs]
        x = [x[h] + _dot(t1[h], xb[h]) for h in hs]
    z = [_dot(x[h].astype(BF16),
              jnp.concatenate([a_t[:, sls[h]], mkv[h]], axis=1).astype(BF16)) for h in hs]
    zb = [zz.astype(BF16) for zz in z]
    gy = [_dot(jnp.where(incl, gm[h][cs:, 0:cs], 0.0).astype(BF16), zb[h]) for h in hs]
    y0 = [_dot(jnp.where(incl, gm[h][cs:, cs:], 0.0).astype(BF16), vb[h]) + gy[h][:, LANES:]
          for h in hs]
    s0 = [s_ref[h] for h in hs]
    s0b = [s.astype(BF16) for s in s0]
    u = [_dot_nt(zb[h][:, 0:LANES], s0b[h]) + z[h][:, LANES:] for h in hs]
    for h in hs:
        g_mat = r_t[:, sls[h]] + gy[h][:, 0:LANES]
        o_ref[0, :, sls[h]] = _dot_nt(g_mat.astype(BF16), s0b[h]) + y0[h]
    for h in hs:
        uv = jnp.concatenate([u[h], v[:, sls[h]]], axis=0).astype(BF16)
        s_ref[h] = (s0[h] + _dot_tn(uv, bk[h])) * p_all[:, sls[h]]


def _rwkv_scan(r, v, kk, lw, k, bb, bsz, nctx_c):
    n, hw = r.shape
    cs = RWKV_CHUNK
    nc = n // bsz // cs

    def blk(bi, d, c):
        rc = jnp.where(c < nctx_c, nctx_c - 1 - c, nc - 1 - (c - nctx_c))
        return bi * nc + jnp.where(d == 0, c, rc)

    s1 = pl.BlockSpec((cs, hw), lambda bi, d, c: (blk(bi, d, c), 0))
    s2 = pl.BlockSpec((1, cs, hw), lambda bi, d, c: (d, blk(bi, d, c), 0))
    return pl.pallas_call(
        _rwkv_scan_kernel,
        out_shape=jax.ShapeDtypeStruct((2, n, hw), F32),
        grid=(bsz, 2, nc),
        in_specs=[s1, s1, s1, s2, s2, s2],
        out_specs=s2,
        scratch_shapes=[pltpu.VMEM((RWKV_HEADS, LANES, LANES), F32)],
        compiler_params=_cparams(("parallel", "parallel", "arbitrary")),
        name="rwkv_scan",
    )(r, v, kk, lw, k, bb)


def _rwkv_out_kernel(o_ref, bonus_ref, g_ref, lng_ref, lnb_ref, y_ref):
    o = o_ref[0] + o_ref[1]
    lane = lax.broadcasted_iota(I32, (1, LANES), 1)
    real = lane < RWKV_HEAD
    lng = lng_ref[...]
    lnb = lnb_ref[...]
    parts = []
    for h in range(RWKV_HEADS):
        sl = slice(h * LANES, (h + 1) * LANES)
        x = o[:, sl]
        mu = jnp.sum(x, axis=-1, keepdims=True) * (1.0 / RWKV_HEAD)
        dlt = jnp.where(real, x - mu, 0.0)
        var = jnp.sum(dlt * dlt, axis=-1, keepdims=True) * (1.0 / RWKV_HEAD)
        parts.append(dlt * lax.rsqrt(var + RWKV_GN_EPS) * lng[:, sl] + lnb[:, sl])
    y = (jnp.concatenate(parts, axis=1) + bonus_ref[...].astype(F32)) * g_ref[...].astype(F32)
    y_ref[...] = y.astype(y_ref.dtype)


def _rwkv_out(o, bonus, g, lng, lnb, tm):
    _, n, hw = o.shape
    full = lambda a: pl.BlockSpec(a.shape, lambda i: (0,) * a.ndim)
    s1 = pl.BlockSpec((tm, hw), lambda i: (i, 0))
    return pl.pallas_call(
        _rwkv_out_kernel,
        out_shape=jax.ShapeDtypeStruct((n, hw), BF16),
        grid=(n // tm,),
        in_specs=[pl.BlockSpec((2, tm, hw), lambda i: (0, i, 0)), s1, s1, full(lng), full(lnb)],
        out_specs=s1,
        compiler_params=_cparams(("parallel",)),
        name="rwkv_out",
    )(o, bonus, g, lng, lnb)


def _merge_kernel(ya_ref, yb_ref, yc_ref, yd_ref, gate_ref, x_ref, gt_ref, wb_ref, wo_ref,
                  lng_ref, lnb_ref, o_ref, *, alpha, d):
    ys = (ya_ref, yb_ref, yc_ref, yd_ref)
    acc = None
    for i in range(N_BRANCH):
        gate = jax.nn.sigmoid(gate_ref[:, i * d:(i + 1) * d].astype(F32))
        term = gate * _dot(ys[i][...], wb_ref[i])
        acc = term if acc is None else acc + term
    mix = _dot(acc.astype(BF16), wo_ref[...])
    y = alpha * x_ref[...] + gt_ref[0] * mix
    o_ref[...] = _layer_norm(y, lng_ref[...], lnb_ref[...])


def _merge(ya, yb, yc, yd, gate, x, gt, wb, wo, lng, lnb, nb, nctx_b, tm, alpha, latent_only):
    n, d = x.shape
    hw = ya.shape[1]
    full = lambda a: pl.BlockSpec(a.shape, lambda i: (0,) * a.ndim)
    if latent_only:
        nbl = nb - nctx_b
        n_out = n // nb * nbl
        src = lambda i: (i // nbl) * nb + nctx_b + i % nbl
    else:
        n_out = n
        src = lambda i: i
    sy = pl.BlockSpec((tm, hw), lambda i: (src(i), 0))
    kern = functools.partial(_merge_kernel, alpha=alpha, d=d)
    return pl.pallas_call(
        kern,
        out_shape=jax.ShapeDtypeStruct((n_out, d), F32),
        grid=(n_out // tm,),
        in_specs=[sy, sy, sy, sy,
                  pl.BlockSpec((tm, N_BRANCH * d), lambda i: (src(i), 0)),
                  pl.BlockSpec((tm, d), lambda i: (src(i), 0)),
                  pl.BlockSpec((1, 1, d), lambda i: (_group_index(src(i), nb, nctx_b), 0, 0)),
                  full(wb), full(wo), full(lng), full(lnb)],
        out_specs=pl.BlockSpec((tm, d), lambda i: (i, 0)),
        compiler_params=_cparams(("parallel",)),
        name="merge",
    )(ya, yb, yc, yd, gate, x, gt, wb, wo, lng, lnb)


def _extract_topk(srcs, n_rows, k, val_refs, pos_refs):
    rio = lax.broadcasted_iota(I32, srcs[0].shape, 0).astype(F32)
    js = range(len(srcs))
    for rnk in range(k):
        m = [jnp.max(s, axis=0, keepdims=True) for s in srcs]
        pos = [jnp.min(jnp.where(srcs[j] == m[j], rio, float(n_rows)), axis=0, keepdims=True)
               for j in js]
        for j in js:
            val_refs[j][rnk:rnk + 1, :] = m[j]
            pos_refs[j][rnk:rnk + 1, :] = pos[j]
        srcs = [jnp.where(rio == pos[j], -jnp.inf, srcs[j]) for j in js]


PEER_CAND_COUNTS = tuple(PEER_TOPK // (a + 1) for a in range(PEER_TOPK))
PEER_N_CAND = sum(PEER_CAND_COUNTS)
PEER_CAND_ROWS = -(-PEER_N_CAND // 8) * 8


PEER_HEADS_PER_STEP = 4


def _peer_topk_kernel(x_ref, mod_ref, wq_ref, k1_ref, k2_ref, h_ref, idx_ref, wgt_ref, off_ref,
                      q_s, v1_s, i1_s, v2_s, i2_s, cand_s, cidx_s, best_s, pos_s):
    step = pl.program_id(1)
    hps = range(PEER_HEADS_PER_STEP)

    @pl.when(step == 0)
    def _():
        m = mod_ref[0]
        hh = x_ref[...] * m[0:1, :] + m[1:2, :]
        h_ref[...] = hh
        q = _dot(hh.astype(BF16), wq_ref[...])
        for a in range(PEER_HEADS):
            q_s[a] = q[:, a * LANES:(a + 1) * LANES].astype(BF16)

    qh = [q_s[step * PEER_HEADS_PER_STEP + j] for j in hps]
    scores = [_dot_nt(k1_ref[j], qh[j]) for j in hps] + [_dot_nt(k2_ref[j], qh[j]) for j in hps]
    _extract_topk(scores, PEER_N_KEYS, PEER_TOPK,
                  [v1_s.at[j] for j in hps] + [v2_s.at[j] for j in hps],
                  [i1_s.at[j] for j in hps] + [i2_s.at[j] for j in hps])
    for j in hps:
        row = 0
        for a, cnt in enumerate(PEER_CAND_COUNTS):
            cand_s[j, row:row + cnt, :] = v1_s[j, a:a + 1, :] + v2_s[j, 0:cnt, :]
            cidx_s[j, row:row + cnt, :] = (i1_s[j, a:a + 1, :] * float(PEER_N_KEYS)
                                           + i2_s[j, 0:cnt, :])
            row += cnt
        pad = PEER_CAND_ROWS - PEER_N_CAND
        if pad:
            cand_s[j, PEER_N_CAND:, :] = jnp.full((pad, cand_s.shape[2]), -jnp.inf, F32)
            cidx_s[j, PEER_N_CAND:, :] = jnp.zeros((pad, cand_s.shape[2]), F32)
    _extract_topk([cand_s[j] for j in hps], PEER_CAND_ROWS, PEER_TOPK,
                  [best_s.at[j] for j in hps], [pos_s.at[j] for j in hps])
    for j in hps:
        cidx = cidx_s[j]
        rio = lax.broadcasted_iota(I32, cidx.shape, 0).astype(F32)
        lo = None
        for rnk in range(PEER_TOPK):
            sel = rio == pos_s[j, rnk:rnk + 1, :]
            e_id = jnp.max(jnp.where(sel, cidx, -1.0), axis=0, keepdims=True).astype(I32)
            idx_ref[j, rnk:rnk + 1, :] = e_id
            off = (e_id >> 1) * 8
            if rnk % 2 == 0:
                lo = off
            else:
                off_ref[j, rnk // 2:rnk // 2 + 1, :] = lo | (off << 16)
        best = best_s[j]
        e = jnp.exp(best - best[0:1, :])
        wgt_ref[j] = e / jnp.sum(e, axis=0, keepdims=True)


def _peer_topk(x, mod, wq, k1, k2, nb, nctx_b, tm):
    n, d = x.shape
    tk = PEER_TOPK
    hp = PEER_HEADS_PER_STEP
    full = lambda a: pl.BlockSpec(a.shape, lambda i, h: (0,) * a.ndim)
    return pl.pallas_call(
        _peer_topk_kernel,
        out_shape=(jax.ShapeDtypeStruct((n, d), F32),
                   jax.ShapeDtypeStruct((PEER_HEADS, tk, n), I32),
                   jax.ShapeDtypeStruct((PEER_HEADS, tk, n), F32),
                   jax.ShapeDtypeStruct((PEER_HEADS, tk // 2, n), I32)),
        grid=(n // tm, PEER_HEADS // hp),
        in_specs=[pl.BlockSpec((tm, d), lambda i, h: (i, 0)),
                  pl.BlockSpec((1, 2, d), lambda i, h: (_group_index(i, nb, nctx_b), 0, 0)),
                  full(wq),
                  pl.BlockSpec((hp, PEER_N_KEYS, LANES), lambda i, h: (h, 0, 0)),
                  pl.BlockSpec((hp, PEER_N_KEYS, LANES), lambda i, h: (h, 0, 0))],
        out_specs=(pl.BlockSpec((tm, d), lambda i, h: (i, 0)),
                   pl.BlockSpec((hp, tk, tm), lambda i, h: (h, 0, i)),
                   pl.BlockSpec((hp, tk, tm), lambda i, h: (h, 0, i)),
                   pl.BlockSpec((hp, tk // 2, tm), lambda i, h: (h, 0, i))),
        scratch_shapes=[pltpu.VMEM((PEER_HEADS, tm, LANES), BF16),
                        pltpu.VMEM((hp, tk, tm), F32), pltpu.VMEM((hp, tk, tm), F32),
                        pltpu.VMEM((hp, tk, tm), F32), pltpu.VMEM((hp, tk, tm), F32),
                        pltpu.VMEM((hp, PEER_CAND_ROWS, tm), F32),
                        pltpu.VMEM((hp, PEER_CAND_ROWS, tm), F32),
                        pltpu.VMEM((hp, tk, tm), F32), pltpu.VMEM((hp, tk, tm), F32)],
        compiler_params=_cparams(("parallel", "arbitrary")),
        name="peer_topk",
    )(x, mod, wq, k1, k2)


PEER_NE = PEER_HEADS * PEER_TOPK
PEER_COLS = PEER_NE * 16
PEER_TOK_UNROLL = 32


def _table_spec(tab):
    return pl.BlockSpec(tab.shape, lambda i: (0, 0), pipeline_mode=pl.Buffered(1))


def _gather_view(off_ref, tab_v, t):
    tiles = []
    tok_ref = off_ref.at[pl.ds(t * (PEER_NE // 2), PEER_NE // 2)]
    for j in range(PEER_NE // 2):
        w = tok_ref[j]
        o0 = pl.multiple_of(w & 0xFFFF, 8)
        o1 = pl.multiple_of(lax.shift_right_logical(w, 16), 8)
        tiles.append(tab_v[pl.ds(o0, 8), :])
        tiles.append(tab_v[pl.ds(o1, 8), :])
    return pltpu.bitcast(jnp.concatenate(tiles, axis=0), BF16)


def _select_mask(px_row):
    shp = (8, PEER_COLS)
    row = lax.broadcasted_iota(I32, shp, 0)
    col = lax.broadcasted_iota(I32, shp, 1)
    sub = (col >> 1) & 7
    fixed = jnp.logical_and((col & 1) == (row >> 2), (sub & 3) == (row & 3))
    return jnp.logical_and(fixed, (sub >> 2).astype(F32) == px_row)


def _split2(x):
    x1 = x.astype(BF16)
    return x1, (x - x1.astype(F32)).astype(BF16)


def _peer_u_kernel(off_ref, h_ref, par_ref, wgt_ref, e16_ref, g16_ref, tab_v, c_ref,
                   px_s, d_s, *, tb):
    px_s[...] = _dot(par_ref[...].astype(BF16), e16_ref[...])

    def tokens(i, carry):
        for u in range(PEER_TOK_UNROLL):
            t = i * PEER_TOK_UNROLL + u
            view = _gather_view(off_ref, tab_v, t)
            h1, h2 = _split2(h_ref[t])
            dd = _dot_nt(jnp.concatenate([h1, h2], axis=0), view)
            dd = jnp.where(_select_mask(px_s[pl.ds(t, 1), :]), dd[0:8] + dd[8:16], 0.0)
            d_s[pl.ds(t, 1), :] = jnp.sum(dd, axis=0, keepdims=True)
        return carry

    lax.fori_loop(0, tb // PEER_TOK_UNROLL, tokens, 0)
    d1, d2 = _split2(d_s[...])
    g16 = g16_ref[...]
    act = _dot(d1, g16) + _dot(d2, g16)
    gelu = 0.5 * act * (1.0 + lax.erf(act * (2.0 ** -0.5)))
    c_ref[...] = gelu * wgt_ref[...]


def _peer_u(off_flat, h3, par, wgt, e16, g16, tab, tb):
    n = h3.shape[0]
    full = lambda a: pl.BlockSpec(a.shape, lambda i: (0,) * a.ndim)
    kern = functools.partial(_peer_u_kernel, tb=tb)
    tok = pl.BlockSpec((tb, PEER_NE), lambda i: (i, 0))
    return pl.pallas_call(
        kern,
        out_shape=jax.ShapeDtypeStruct((n, PEER_NE), F32),
        grid=(n // tb,),
        in_specs=[pl.BlockSpec((tb * PEER_NE // 2,), lambda i: (i,), memory_space=pltpu.SMEM),
                  pl.BlockSpec((tb, 8, LANES), lambda i: (i, 0, 0)),
                  tok, tok, full(e16), full(g16), _table_spec(tab)],
        out_specs=tok,
        scratch_shapes=[pltpu.VMEM((tb, PEER_COLS), F32), pltpu.VMEM((tb, PEER_COLS), F32)],
        compiler_params=_cparams(("arbitrary",)),
        name="peer_u",
    )(off_flat, h3, par, wgt, e16, g16, tab)


def _peer_v_kernel(off_ref, c_ref, par_ref, e16_ref, x_ref, gt_ref, lng_ref, lnb_ref, tab_v, o_ref,
                   px_s, c1_s, c2_s, *, tb, alpha):
    e16 = e16_ref[...]
    px_s[...] = _dot(par_ref[...].astype(BF16), e16)
    c1, c2 = _split2(c_ref[...])
    c1_s[...] = _dot(c1, e16)
    c2_s[...] = _dot(c2, e16)

    def tokens(i, carry):
        for u in range(PEER_TOK_UNROLL):
            t = i * PEER_TOK_UNROLL + u
            view = _gather_view(off_ref, tab_v, t)
            sel = _select_mask(px_s[pl.ds(t, 1), :])
            lhs = jnp.concatenate([jnp.where(sel, c1_s[pl.ds(t, 1), :], 0.0),
                                   jnp.where(sel, c2_s[pl.ds(t, 1), :], 0.0)], axis=0)
            out = _dot(lhs.astype(BF16), view)
            o_ref[t] = out[0:8] + out[8:16]
        return carry

    lax.fori_loop(0, tb // PEER_TOK_UNROLL, tokens, 0)
    y = alpha * x_ref[...] + gt_ref[...] * o_ref[...]
    inv_d = 1.0 / (8 * LANES)
    mu = jnp.sum(jnp.sum(y, axis=2, keepdims=True), axis=1, keepdims=True) * inv_d
    dlt = y - mu
    var = jnp.sum(jnp.sum(dlt * dlt, axis=2, keepdims=True), axis=1, keepdims=True) * inv_d
    o_ref[...] = dlt * lax.rsqrt(var + LN_EPS) * lng_ref[...] + lnb_ref[...]


def _peer_v(off_flat, cw, par, e16, x3, gt3, lng3, lnb3, tab, tb, nb, nctx_b, alpha):
    n = cw.shape[0]
    full = lambda a: pl.BlockSpec(a.shape, lambda i: (0,) * a.ndim)
    kern = functools.partial(_peer_v_kernel, tb=tb, alpha=alpha)
    tok = pl.BlockSpec((tb, PEER_NE), lambda i: (i, 0))
    return pl.pallas_call(
        kern,
        out_shape=jax.ShapeDtypeStruct((n, 8, LANES), F32),
        grid=(n // tb,),
        in_specs=[pl.BlockSpec((tb * PEER_NE // 2,), lambda i: (i,), memory_space=pltpu.SMEM),
                  tok, tok, full(e16),
                  pl.BlockSpec((tb, 8, LANES), lambda i: (i, 0, 0)),
                  pl.BlockSpec((1, 8, LANES), lambda i: (_group_index(i, nb, nctx_b), 0, 0)),
                  full(lng3), full(lnb3), _table_spec(tab)],
        out_specs=pl.BlockSpec((tb, 8, LANES), lambda i: (i, 0, 0)),
        scratch_shapes=[pltpu.VMEM((tb, PEER_COLS), F32), pltpu.VMEM((tb, PEER_COLS), F32),
                        pltpu.VMEM((tb, PEER_COLS), F32)],
        compiler_params=_cparams(("arbitrary",)),
        name="peer_v",
    )(off_flat, cw, par, e16, x3, gt3, lng3, lnb3, tab)


def _pack_table(tab):
    e, d = tab.shape
    bits = lax.bitcast_convert_type(tab.astype(BF16), jnp.uint16).astype(U32)
    words = bits[:, :d // 2] | (bits[:, d // 2:] << 16)
    return words.reshape(e * 4, LANES)


def _ln_res_kernel(x_ref, f_ref, gt_ref, lng_ref, lnb_ref, o_ref, *, alpha):
    y = alpha * x_ref[...] + gt_ref[0] * f_ref[...]
    o_ref[...] = _layer_norm(y, lng_ref[...], lnb_ref[...])


def _ln_res(x, f, gt, lng, lnb, nb, nctx_b, tm, alpha):
    n, d = x.shape
    full = lambda a: pl.BlockSpec(a.shape, lambda i: (0,) * a.ndim)
    s1 = pl.BlockSpec((tm, d), lambda i: (i, 0))
    return pl.pallas_call(
        functools.partial(_ln_res_kernel, alpha=alpha),
        out_shape=jax.ShapeDtypeStruct((n, d), F32),
        grid=(n // tm,),
        in_specs=[s1, s1,
                  pl.BlockSpec((1, 1, d), lambda i: (_group_index(i, nb, nctx_b), 0, 0)),
                  full(lng), full(lnb)],
        out_specs=s1,
        compiler_params=_cparams(("parallel",)),
        name="ln_res",
    )(x, f, gt, lng, lnb)


def _pad_heads(w, n_heads, hd):
    lead = w.shape[:-1]
    w = w.reshape(lead + (n_heads, hd))
    w = jnp.pad(w, [(0, 0)] * len(lead) + [(0, 0), (0, LANES - hd)])
    return w.reshape(lead + (n_heads * LANES,))


def _pad_head_rows(w, n_heads, hd):
    d = w.shape[-1]
    w = w.reshape(n_heads, hd, d)
    w = jnp.pad(w, [(0, 0), (0, LANES - hd), (0, 0)])
    return w.reshape(n_heads * LANES, d)


def _rope_tables(rows, rot_dim, lane_off, ctx_len):
    r_idx = jnp.repeat(jnp.arange(rows), GRID_W).astype(F32)
    c_idx = jnp.tile(jnp.arange(GRID_W), rows).astype(F32)
    n = rot_dim // 4
    inv = ROPE_THETA ** (-jnp.arange(n, dtype=F32) / n)
    ang = jnp.concatenate([r_idx[:, None] * inv, c_idx[:, None] * inv], axis=-1)
    cos = jnp.repeat(jnp.cos(ang), 2, axis=-1)
    sin = jnp.repeat(jnp.sin(ang), 2, axis=-1) * jnp.tile(jnp.array([-1.0, 1.0], F32), rot_dim // 2)
    s = ang.shape[0]
    cos_t = jnp.ones((ctx_len + s, LANES), F32).at[ctx_len:, lane_off:lane_off + rot_dim].set(cos)
    sin_t = jnp.zeros((ctx_len + s, LANES), F32).at[ctx_len:, lane_off:lane_off + rot_dim].set(sin)
    return cos_t, sin_t


def _split_cols(w, widths):
    out, start = [], 0
    for wd in widths:
        out.append(w[..., start:start + wd])
        start += wd
    return out


def _forward(x, c, ctx, c_ctx, ada_w, ada_b, w_in, mla_q_norm, mla_kv_norm, mla_w_uq, mla_w_ukv,
             rwkv_mu, rwkv_w0, rwkv_w2, rwkv_a0, rwkv_a2, rwkv_g2, rwkv_k_k, rwkv_k_a, rwkv_r_k,
             rwkv_ln_g, rwkv_ln_b, gqa_q_norm, gqa_k_norm, win_sink, w_branch, w_out, ln1_g, ln1_b,
             peer_wq, peer_keys, peer_u, peer_v, ln2_g, ln2_b):
    stages = []
    bsz, seq, d = x.shape
    ctx_len = ctx.shape[1]
    depth = ada_w.shape[0]
    alpha = (2 * depth) ** 0.25
    l_tot = ctx_len + seq
    n = bsz * l_tot
    tm = 256
    tm_feat = 128
    assert ctx_len % tm == 0 and seq % tm == 0 and seq % GRID_W == 0
    nb = l_tot // tm
    nctx_b = ctx_len // tm
    rows = seq // GRID_W
    ne = PEER_HEADS * PEER_TOPK

    xs = jnp.concatenate([ctx, x], axis=1).reshape(n, d)
    cos_m, sin_m = _rope_tables(rows, MLA_ROPE, MLA_NOPE, ctx_len)
    cos_h, sin_h = _rope_tables(rows, GQA_HEAD, 0, ctx_len)

    m_rows = 16
    cvec = jnp.zeros((m_rows, d), F32).at[:bsz].set(c).at[bsz].set(c_ctx)
    mla_in = MLA_Q_RANK + MLA_KV_RANK + MLA_ROPE
    rwkv_in = 3 * RWKV_W + 2 * RWKV_W_LORA + 2 * RWKV_A_LORA + RWKV_G_LORA
    gqa_in = (GQA_HEADS + 2 * GQA_KV_HEADS) * GQA_HEAD
    win_in = (WIN_HEADS + 2 * WIN_KV_HEADS) * WIN_HEAD
    rw_widths = (RWKV_W, RWKV_W, RWKV_W, RWKV_W_LORA, RWKV_W_LORA, RWKV_A_LORA, RWKV_A_LORA,
                 RWKV_G_LORA)

    col16 = jnp.arange(PEER_COLS)
    e16 = (col16[None, :] // 16 == jnp.arange(ne)[:, None]).astype(BF16)
    g16 = e16.T

    for lyr in range(depth):
        last = lyr == depth - 1
        mod = _ada(cvec, ada_w[lyr], ada_b[lyr])
        chunks = [mod[:, i * d:(i + 1) * d] for i in range(6)]

        def table(ch):
            lat = ch[:bsz]
            cx = jnp.broadcast_to(ch[bsz][None], (bsz, d))
            return jnp.stack([cx, lat], axis=1).reshape(bsz * 2, d)

        sh1, sc1, gt1, sh2, sc2, gt2 = [table(ch) for ch in chunks]
        mod1 = jnp.stack([1.0 + sc1, sh1], axis=1)
        mod2 = jnp.stack([1.0 + sc2, sh2], axis=1)
        gt1 = gt1[:, None, :]
        gt2 = gt2[:, None, :]

        wi = w_in[lyr]
        w_mla, w_rw, w_gq, w_wn, w_gate = _split_cols(wi, (mla_in, rwkv_in, gqa_in, win_in, N_BRANCH * d))
        zc = lambda k: jnp.zeros((d, k), F32)
        w_mla_p = jnp.concatenate([w_mla[:, :MLA_Q_RANK + MLA_KV_RANK], zc(MLA_NOPE),
                                   w_mla[:, MLA_Q_RANK + MLA_KV_RANK:], zc(LANES - MLA_NOPE - MLA_ROPE)],
                                  axis=1)
        rr, rk_, rv, rwf, rwb, raf, rab, rgi = _split_cols(w_rw, rw_widths)
        hp = lambda w: _pad_heads(w, RWKV_HEADS, RWKV_HEAD)
        w_rw_p = jnp.concatenate([hp(rr), hp(rk_), hp(rv), rwf, rwb, raf, rab, rgi], axis=1)
        mu_parts = _split_cols(rwkv_mu[lyr], rw_widths)
        mu_p = jnp.concatenate([hp(mu_parts[0]), hp(mu_parts[1]), hp(mu_parts[2])] + mu_parts[3:], axis=1)

        def gqa_cols(w, nq, nkv, hd):
            q_, k_, v_ = _split_cols(w, (nq * hd, nkv * hd, nkv * hd))
            return jnp.concatenate([_pad_heads(q_, nq, hd), _pad_heads(k_, nkv, hd),
                                    _pad_heads(v_, nkv, hd)], axis=1)

        w_gq_p = gqa_cols(w_gq, GQA_HEADS, GQA_KV_HEADS, GQA_HEAD)
        w_wn_p = gqa_cols(w_wn, WIN_HEADS, WIN_KV_HEADS, WIN_HEAD)

        pm = _modmm(xs, mod1, w_mla_p.astype(BF16), nb, nctx_b, tm)
        pr = _modmm(xs, mod1, w_rw_p.astype(BF16), nb, nctx_b, tm)
        pg = _modmm(xs, mod1, w_gq_p.astype(BF16), nb, nctx_b, tm, BF16)
        pw = _modmm(xs, mod1, w_wn_p.astype(BF16), nb, nctx_b, tm, BF16)
        gate = _modmm(xs, mod1, w_gate.astype(BF16), nb, nctx_b, tm, BF16)

        uq = mla_w_uq[lyr].reshape(MLA_Q_RANK, MLA_HEADS, MLA_NOPE + MLA_ROPE)
        uq = jnp.pad(uq, [(0, 0), (0, 0), (0, LANES - MLA_NOPE - MLA_ROPE)]).reshape(MLA_Q_RANK, -1)
        ukv = mla_w_ukv[lyr].reshape(MLA_KV_RANK, MLA_HEADS, MLA_NOPE + MLA_V)
        uk = jnp.pad(ukv[:, :, :MLA_NOPE], [(0, 0), (0, 0), (0, LANES - MLA_NOPE)]).reshape(MLA_KV_RANK, -1)
        uv = jnp.pad(ukv[:, :, MLA_NOPE:], [(0, 0), (0, 0), (0, LANES - MLA_V)]).reshape(MLA_KV_RANK, -1)
        qa, ka, va = _mla_prep(pm, mla_q_norm[lyr][None], mla_kv_norm[lyr][None], uq.astype(BF16),
                               uk.astype(BF16), uv.astype(BF16), cos_m, sin_m, nb, tm)
        r3 = lambda a: a.reshape(bsz, l_tot, a.shape[-1])
        ya = _attention(qa, r3(ka), va, ctx_len, n_q=MLA_HEADS, n_kv=MLA_HEADS, tq=256)

        zl = jnp.zeros((RWKV_W_LORA, RWKV_HEADS * LANES), F32)
        w2c = jnp.concatenate([jnp.concatenate([hp(rwkv_w2[lyr, 0]), zl], axis=1),
                               jnp.concatenate([zl, hp(rwkv_w2[lyr, 1])], axis=1)], axis=0)
        a2c = jnp.concatenate([jnp.concatenate([hp(rwkv_a2[lyr, 0]), zl], axis=1),
                               jnp.concatenate([zl, hp(rwkv_a2[lyr, 1])], axis=1)], axis=0)
        w0c = jnp.concatenate([hp(rwkv_w0[lyr, 0]), hp(rwkv_w0[lyr, 1])])[None]
        a0c = jnp.concatenate([hp(rwkv_a0[lyr, 0]), hp(rwkv_a0[lyr, 1])])[None]
        feats = _rwkv_features(pr, mu_p, w0c, w2c.astype(BF16), a0c, a2c.astype(BF16),
                               hp(rwkv_g2[lyr]).astype(BF16), hp(rwkv_k_k[lyr])[None],
                               hp(rwkv_k_a[lyr])[None], hp(rwkv_r_k[lyr].reshape(-1))[None],
                               l_tot // tm_feat, ctx_len // tm_feat, tm_feat)
        f_r, f_v, f_kk, f_g, f_bonus, f_lw, f_k, f_b = feats
        o_scan = _rwkv_scan(f_r, f_v, f_kk, f_lw, f_k, f_b, bsz, ctx_len // RWKV_CHUNK)
        yb = _rwkv_out(o_scan, f_bonus, f_g, hp(rwkv_ln_g[lyr])[None], hp(rwkv_ln_b[lyr])[None], tm)

        pad_g = lambda g: jnp.pad(g, (0, LANES - g.shape[0]))[None]
        qc, kc, vc = _gqa_prep(pg, pad_g(gqa_q_norm[lyr]), pad_g(gqa_k_norm[lyr]), cos_h, sin_h,
                               nb, tm, n_q=GQA_HEADS, n_kv=GQA_KV_HEADS, hd=GQA_HEAD,
                               scale=GQA_SCALE, qk_norm=True, transposed=True)
        yc = _attention(qc, r3(kc), vc, ctx_len, n_q=GQA_HEADS, n_kv=GQA_KV_HEADS, tq=256)

        ones_g = jnp.ones((1, LANES), F32)
        qd, kd, vd = _gqa_prep(pw, ones_g, ones_g, cos_h, sin_h, nb, tm, n_q=WIN_HEADS,
                               n_kv=WIN_KV_HEADS, hd=WIN_HEAD, scale=WIN_SCALE, qk_norm=False,
                               transposed=False)
        yd = _window_attention(r3(qd), r3(kd), r3(vd), win_sink[lyr], ctx_len,
                               n_q=WIN_HEADS, n_kv=WIN_KV_HEADS)

        wb = jnp.stack([_pad_head_rows(w_branch[lyr, i], 8, 64) for i in range(N_BRANCH)]).astype(BF16)
        x_mid = _merge(ya.reshape(n, -1), yb, yc.reshape(n, -1), yd.reshape(n, -1), gate, xs, gt1,
                       wb, w_out[lyr].astype(BF16), ln1_g[lyr][None], ln1_b[lyr][None],
                       nb, nctx_b, tm, alpha, last)
        if last:
            n_p, nb_p, nctx_p = bsz * seq, seq // tm, 0
        else:
            n_p, nb_p, nctx_p = n, nb, nctx_b

        keys = peer_keys[lyr]
        half = PEER_DQ // 2
        k1 = jnp.pad(keys[:, 0], [(0, 0), (0, 0), (0, LANES - half)]).astype(BF16)
        k2 = jnp.pad(keys[:, 1], [(0, 0), (0, 0), (LANES - half, 0)]).astype(BF16)
        h_in, idx_t, wgt_t, off_t = _peer_topk(x_mid, mod2, peer_wq[lyr].astype(BF16), k1, k2,
                                               nb_p, nctx_p, tm)
        idx = idx_t.reshape(ne, n_p).T
        wgt = wgt_t.reshape(ne, n_p).T
        off_flat = off_t.reshape(ne // 2, n_p).T.reshape(-1)
        par = (idx & 1).astype(F32)
        tb = 128
        cw = _peer_u(off_flat, h_in.reshape(n_p, 8, LANES), par, wgt, e16, g16,
                     _pack_table(peer_u[lyr]), tb)
        t3 = lambda a: a.reshape(a.shape[0], 8, LANES)
        xs = _peer_v(off_flat, cw, par, e16, t3(x_mid), t3(gt2[:, 0, :]), t3(ln2_g[lyr][None]),
                     t3(ln2_b[lyr][None]), _pack_table(peer_v[lyr]), tb,
                     nb_p * tm // tb, nctx_p * tm // tb, alpha).reshape(n_p, d)
        stages.append(dict(ya=ya, yb=yb, yc=yc, yd=yd, x_mid=x_mid, idx=idx, wgt=wgt, x_out=xs))

    return xs.reshape(bsz, seq, d), stages


def kernel(x, c, ctx, c_ctx, ada_w, ada_b, w_in, mla_q_norm, mla_kv_norm, mla_w_uq, mla_w_ukv,
           rwkv_mu, rwkv_w0, rwkv_w2, rwkv_a0, rwkv_a2, rwkv_g2, rwkv_k_k, rwkv_k_a, rwkv_r_k,
           rwkv_ln_g, rwkv_ln_b, gqa_q_norm, gqa_k_norm, win_sink, w_branch, w_out, ln1_g, ln1_b,
           peer_wq, peer_keys, peer_u, peer_v, ln2_g, ln2_b):
    out, _ = _forward(x, c, ctx, c_ctx, ada_w, ada_b, w_in, mla_q_norm, mla_kv_norm, mla_w_uq,
                      mla_w_ukv, rwkv_mu, rwkv_w0, rwkv_w2, rwkv_a0, rwkv_a2, rwkv_g2, rwkv_k_k,
                      rwkv_k_a, rwkv_r_k, rwkv_ln_g, rwkv_ln_b, gqa_q_norm, gqa_k_norm, win_sink,
                      w_branch, w_out, ln1_g, ln1_b, peer_wq, peer_keys, peer_u, peer_v, ln2_g,
                      ln2_b)
    return out
```
